```python
import jax, jax.numpy as jnp
from jax import lax
import numpy as np

D_MODEL = 2048
BATCH = 2
SEQ = 4096
DEPTH = 2

GRID_W = 64
CTX_LEN = 256
HEAD_DIM = 128
N_HEADS = D_MODEL // HEAD_DIM
A_Q_HEADS = N_HEADS // 2
A_KV_HEADS = max(1, A_Q_HEADS // 4)
B_Q_HEADS = N_HEADS - A_Q_HEADS
B_KV_HEADS = max(1, B_Q_HEADS // 4)
C_HEADS = N_HEADS
Q_BLOCK = 128
WINDOW = 128
NA_KH = 8
NA_KW = 16
D_FF = 4 * D_MODEL
ROPE_THETA = 10000.0
ROPE_PAIRS = HEAD_DIM // 4
NORM_EPS = 1e-6
N_EVEN = (DEPTH + 1) // 2
N_ODD = DEPTH // 2
EVEN_IN = (A_Q_HEADS + 2 * A_KV_HEADS + B_Q_HEADS + 2 * B_KV_HEADS) * HEAD_DIM
ODD_IN = 3 * C_HEADS * HEAD_DIM
NEG = -1e30

kernel_name = 'hybrid_dit_gqa_swa_natten'


def _rmsnorm(x, w):
    xf = x.astype(jnp.float32)
    y = xf * lax.rsqrt(jnp.mean(xf * xf, axis=-1, keepdims=True) + NORM_EPS) * w.astype(jnp.float32)
    return y.astype(x.dtype)


def _modulate(h, shift, scale):
    return h * (1 + scale) + shift


def _rope_tables(S):
    t = jnp.arange(S)
    row = (t // GRID_W).astype(jnp.float32)
    col = (t % GRID_W).astype(jnp.float32)
    inv = ROPE_THETA ** (-jnp.arange(ROPE_PAIRS, dtype=jnp.float32) / ROPE_PAIRS)
    ang_r = row[:, None] * inv
    ang_c = col[:, None] * inv
    ang = jnp.concatenate([ang_r, ang_r, ang_c, ang_c], axis=-1)
    return jnp.cos(ang), jnp.sin(ang)


def _rot_half(u):
    u1, u2 = jnp.split(u, 2, axis=-1)
    return jnp.concatenate([-u2, u1], axis=-1)


def _apply_rope(x, cos, sin):
    xr, xc = jnp.split(x, 2, axis=-1)
    xrot = jnp.concatenate([_rot_half(xr), _rot_half(xc)], axis=-1)
    return (x.astype(jnp.float32) * cos + xrot.astype(jnp.float32) * sin).astype(x.dtype)


def _heads(x, n):
    B, N, _ = x.shape
    return x.reshape(B, N, n, HEAD_DIM).transpose(0, 2, 1, 3)


def _q_groups(x, n_kv):
    B, N, F = x.shape
    g = F // HEAD_DIM // n_kv
    return x.reshape(B, N, n_kv, g, HEAD_DIM).transpose(0, 2, 3, 1, 4)


def _merge(o):
    B, K, G, N, dh = o.shape
    return o.transpose(0, 3, 1, 2, 4).reshape(B, N, K * G * dh)


def _dense_attn(q, k, v, sink=None):
    B, K, G, N, dh = q.shape
    s = jnp.einsum('bkgnd,bkmd->bkgnm', q, k).astype(jnp.float32) * (dh ** -0.5)
    M = k.shape[2]
    if sink is not None:
        sk = jnp.broadcast_to(sink.astype(jnp.float32)[None, :, :, None, None], (B, K, G, N, 1))
        s = jnp.concatenate([s, sk], axis=-1)
    p = jax.nn.softmax(s, axis=-1)[..., :M].astype(v.dtype)
    return jnp.einsum('bkgnm,bkmd->bkgnd', p, v)


def _global_attn(q, k_all, v_all):
    B, K, G, S, dh = q.shape
    nblk = S // Q_BLOCK
    qb = jnp.moveaxis(q.reshape(B, K, G, nblk, Q_BLOCK, dh), 3, 0)

    def block(qi):
        s = jnp.einsum('bkgqd,bkmd->bkgqm', qi, k_all).astype(jnp.float32) * (dh ** -0.5)
        p = jax.nn.softmax(s, axis=-1).astype(v_all.dtype)
        return jnp.einsum('bkgqm,bkmd->bkgqd', p, v_all)

    o = lax.map(block, qb)
    return jnp.moveaxis(o, 0, 3).reshape(B, K, G, S, dh)


def _window_attn(q, k, v, k_ctx, v_ctx, sink):
    B, K, G, S, dh = q.shape
    nblk = S // Q_BLOCK
    pad = ((0, 0), (0, 0), (WINDOW, WINDOW), (0, 0))
    kp = jnp.pad(k, pad).reshape(B, K, nblk + 2, Q_BLOCK, dh)
    vp = jnp.pad(v, pad).reshape(B, K, nblk + 2, Q_BLOCK, dh)
    kw = jnp.concatenate([kp[:, :, 0:nblk], kp[:, :, 1:nblk + 1], kp[:, :, 2:nblk + 2]], axis=3)
    vw = jnp.concatenate([vp[:, :, 0:nblk], vp[:, :, 1:nblk + 1], vp[:, :, 2:nblk + 2]], axis=3)
    qb = q.reshape(B, K, G, nblk, Q_BLOCK, dh)
    scale = dh ** -0.5
    s_win = jnp.einsum('bkgnqd,bknjd->bkgnqj', qb, kw).astype(jnp.float32) * scale
    blk = jnp.arange(nblk)[:, None]
    qpos = blk * Q_BLOCK + jnp.arange(Q_BLOCK)[None, :]
    kpos = blk * Q_BLOCK - WINDOW + jnp.arange(3 * Q_BLOCK)[None, :]
    kk = kpos[:, None, :]
    valid = (kk >= 0) & (kk < S) & (jnp.abs(kk - qpos[:, :, None]) <= WINDOW)
    s_win = jnp.where(valid, s_win, NEG)
    s_ctx = jnp.einsum('bkgnqd,bkcd->bkgnqc', qb, k_ctx).astype(jnp.float32) * scale
    s_sink = jnp.broadcast_to(sink.astype(jnp.float32)[None, :, :, None, None, None],
                              (B, K, G, nblk, Q_BLOCK, 1))
    p = jax.nn.softmax(jnp.concatenate([s_ctx, s_win, s_sink], axis=-1), axis=-1)
    C = k_ctx.shape[2]
    p_ctx = p[..., :C].astype(v.dtype)
    p_win = p[..., C:C + 3 * Q_BLOCK].astype(v.dtype)
    o = (jnp.einsum('bkgnqc,bkcd->bkgnqd', p_ctx, v_ctx)
         + jnp.einsum('bkgnqj,bknjd->bkgnqd', p_win, vw))
    return o.reshape(B, K, G, S, dh)


def _even_mixer(h, hc, w_in, w_out, q_norm_w, k_norm_w, sink, cos, sin, need_ctx):
    sizes = [A_Q_HEADS, A_KV_HEADS, A_KV_HEADS, B_Q_HEADS, B_KV_HEADS, B_KV_HEADS]
    splits = np.cumsum([n * HEAD_DIM for n in sizes])[:-1].tolist()
    qa, ka, va, qb, kb, vb = jnp.split(h @ w_in, splits, axis=-1)
    qac, kac, vac, qbc, kbc, vbc = jnp.split(hc @ w_in, splits, axis=-1)
    qa = _apply_rope(_rmsnorm(_q_groups(qa, A_KV_HEADS), q_norm_w), cos, sin)
    ka = _apply_rope(_rmsnorm(_heads(ka, A_KV_HEADS), k_norm_w), cos, sin)
    va = _heads(va, A_KV_HEADS)
    kac = _rmsnorm(_heads(kac, A_KV_HEADS), k_norm_w)
    vac = _heads(vac, A_KV_HEADS)
    oa = _global_attn(qa, jnp.concatenate([kac, ka], axis=2), jnp.concatenate([vac, va], axis=2))
    sink_kg = sink.reshape(B_KV_HEADS, B_Q_HEADS // B_KV_HEADS)
    qb = _apply_rope(_q_groups(qb, B_KV_HEADS), cos, sin)
    kb = _apply_rope(_heads(kb, B_KV_HEADS), cos, sin)
    vb = _heads(vb, B_KV_HEADS)
    kbc = _heads(kbc, B_KV_HEADS)
    vbc = _heads(vbc, B_KV_HEADS)
    ob = _window_attn(qb, kb, vb, kbc, vbc, sink_kg)
    y = jnp.concatenate([_merge(oa), _merge(ob)], axis=-1) @ w_out
    if not need_ctx:
        return y, None
    qac = _rmsnorm(_q_groups(qac, A_KV_HEADS), q_norm_w)
    oac = _dense_attn(qac, kac, vac)
    obc = _dense_attn(_q_groups(qbc, B_KV_HEADS), kbc, vbc, sink_kg)
    yc = jnp.concatenate([_merge(oac), _merge(obc)], axis=-1) @ w_out
    return y, yc


def _odd_mixer(h, hc, w_in, w_out, rpb, need_ctx):
    B, S, _ = h.shape
    rows = S // GRID_W
    kh = min(NA_KH, rows)
    kw = NA_KW
    q, k, v = [_heads(t, C_HEADS) for t in jnp.split(h @ w_in, 3, axis=-1)]
    qc, kc, vc = [_heads(t, C_HEADS) for t in jnp.split(hc @ w_in, 3, axis=-1)]
    qg = q.reshape(B, C_HEADS, rows, GRID_W, HEAD_DIM)
    kg = k.reshape(B, C_HEADS, rows, GRID_W, HEAD_DIM)
    vg = v.reshape(B, C_HEADS, rows, GRID_W, HEAD_DIM)
    col = jnp.arange(GRID_W)
    cs = jnp.clip(col - kw // 2, 0, GRID_W - kw)
    col_valid = (col[None, :] >= cs[:, None]) & (col[None, :] < cs[:, None] + kw)
    ci = jnp.clip(col[None, :] - col[:, None] + NA_KW - 1, 0, 2 * NA_KW - 2)
    scale = HEAD_DIM ** -0.5
    C = kc.shape[2]

    def row_block(args):
        r, qr = args
        rs = jnp.clip(r - kh // 2, 0, rows - kh)
        kr = lax.dynamic_slice_in_dim(kg, rs, kh, axis=2)
        vr = lax.dynamic_slice_in_dim(vg, rs, kh, axis=2)
        ri = rs + jnp.arange(kh) - r + NA_KH - 1
        bias = rpb[:, ri[None, :, None], ci[:, None, :]]
        s_nb = jnp.einsum('bhqd,bhiwd->bhqiw', qr, kr).astype(jnp.float32) * scale
        s_nb = s_nb + bias[None].astype(jnp.float32)
        s_nb = jnp.where(col_valid[:, None, :], s_nb, NEG).reshape(B, C_HEADS, GRID_W, kh * GRID_W)
        s_ctx = jnp.einsum('bhqd,bhcd->bhqc', qr, kc).astype(jnp.float32) * scale
        p = jax.nn.softmax(jnp.concatenate([s_ctx, s_nb], axis=-1), axis=-1).astype(v.dtype)
        p_nb = p[..., C:].reshape(B, C_HEADS, GRID_W, kh, GRID_W)
        return (jnp.einsum('bhqc,bhcd->bhqd', p[..., :C], vc)
                + jnp.einsum('bhqiw,bhiwd->bhqd', p_nb, vr))

    o = lax.map(row_block, (jnp.arange(rows), jnp.moveaxis(qg, 2, 0)))
    o = jnp.moveaxis(o, 0, 2).reshape(B, C_HEADS, S, HEAD_DIM)
    y = _merge(o[:, :, None]) @ w_out
    if not need_ctx:
        return y, None
    oc = _dense_attn(qc[:, :, None], kc, vc)
    return y, _merge(oc) @ w_out


def _mlp(h, w1, w2):
    return jnp.square(jax.nn.relu(h @ w1)) @ w2


def setup_inputs(seed: int = 0) -> dict:
    key = jax.random.key(seed)
    ks = jax.random.split(key, 20)
    f32 = jnp.float32

    def nrm(k, shape, fan_in, gain=1.0):
        return gain * fan_in ** -0.5 * jax.random.normal(k, shape, f32)

    return {
        'x': jax.random.normal(ks[0], (BATCH, SEQ, D_MODEL), f32),
        'c': jax.random.normal(ks[1], (BATCH, D_MODEL), f32),
        'ctx': jax.random.normal(ks[2], (BATCH, CTX_LEN, D_MODEL), f32),
        'c_ctx': jax.random.normal(ks[3], (D_MODEL,), f32),
        'ada_w': nrm(ks[4], (DEPTH, D_MODEL, 6 * D_MODEL), D_MODEL, 0.5),
        'ada_b': 0.02 * jax.random.normal(ks[5], (DEPTH, 6 * D_MODEL), f32),
        'norm_w': 1.0 + 0.02 * jax.random.normal(ks[6], (DEPTH, 2, D_MODEL), f32),
        'mlp_w1': nrm(ks[7], (DEPTH, D_MODEL, D_FF), D_MODEL),
        'mlp_w2': nrm(ks[8], (DEPTH, D_FF, D_MODEL), D_FF),
        'ev_w_in': nrm(ks[9], (N_EVEN, D_MODEL, EVEN_IN), D_MODEL),
        'ev_w_out': nrm(ks[10], (N_EVEN, N_HEADS * HEAD_DIM, D_MODEL), N_HEADS * HEAD_DIM),
        'ev_q_norm': 1.0 + 0.02 * jax.random.normal(ks[11], (N_EVEN, HEAD_DIM), f32),
        'ev_k_norm': 1.0 + 0.02 * jax.random.normal(ks[12], (N_EVEN, HEAD_DIM), f32),
        'ev_sink': 0.5 * jax.random.normal(ks[13], (N_EVEN, B_Q_HEADS), f32),
        'od_w_in': nrm(ks[14], (N_ODD, D_MODEL, ODD_IN), D_MODEL),
        'od_w_out': nrm(ks[15], (N_ODD, C_HEADS * HEAD_DIM, D_MODEL), C_HEADS * HEAD_DIM),
        'od_rpb': 0.1 * jax.random.normal(ks[16], (N_ODD, C_HEADS, 2 * NA_KH - 1, 2 * NA_KW - 1), f32),
        'final_norm_w': 1.0 + 0.02 * jax.random.normal(ks[17], (D_MODEL,), f32),
    }


def reference(x, c, ctx, c_ctx, ada_w, ada_b, norm_w, mlp_w1, mlp_w2, ev_w_in, ev_w_out,
              ev_q_norm, ev_k_norm, ev_sink, od_w_in, od_w_out, od_rpb, final_norm_w):
    S = x.shape[1]
    cos, sin = _rope_tables(S)
    for i in range(DEPTH):
        need_ctx = i < DEPTH - 1
        mod = (jax.nn.silu(c) @ ada_w[i] + ada_b[i])[:, None, :]
        sh1, sc1, g1, sh2, sc2, g2 = jnp.split(mod, 6, axis=-1)
        modc = jax.nn.silu(c_ctx) @ ada_w[i] + ada_b[i]
        sh1c, sc1c, g1c, sh2c, sc2c, g2c = jnp.split(modc, 6, axis=-1)
        h = _modulate(_rmsnorm(x, norm_w[i, 0]), sh1, sc1)
        hc = _modulate(_rmsnorm(ctx, norm_w[i, 0]), sh1c, sc1c)
        if i % 2 == 0:
            j = i // 2
            y, yc = _even_mixer(h, hc, ev_w_in[j], ev_w_out[j], ev_q_norm[j], ev_k_norm[j],
                                ev_sink[j], cos, sin, need_ctx)
        else:
            j = i // 2
            y, yc = _odd_mixer(h, hc, od_w_in[j], od_w_out[j], od_rpb[j], need_ctx)
        x = x + g1 * y
        h = _modulate(_rmsnorm(x, norm_w[i, 1]), sh2, sc2)
        x = x + g2 * _mlp(h, mlp_w1[i], mlp_w2[i])
        if need_ctx:
            ctx = ctx + g1c * yc
            hc = _modulate(_rmsnorm(ctx, norm_w[i, 1]), sh2c, sc2c)
            ctx = ctx + g2c * _mlp(hc, mlp_w1[i], mlp_w2[i])
    return _rmsnorm(x, final_norm_w)
```

```python
import functools

import jax
import jax.numpy as jnp
import numpy as np
from jax import lax
from jax.experimental import pallas as pl
from jax.experimental.pallas import tpu as pltpu

D_MODEL = 2048
GRID_W = 64
HEAD_DIM = 128
N_HEADS = D_MODEL // HEAD_DIM
A_Q_HEADS = N_HEADS // 2
A_KV_HEADS = max(1, A_Q_HEADS // 4)
B_Q_HEADS = N_HEADS - A_Q_HEADS
B_KV_HEADS = max(1, B_Q_HEADS // 4)
GQA_GROUP = A_Q_HEADS // A_KV_HEADS
C_HEADS = N_HEADS
WINDOW = 128
NA_KH = 8
NA_KW = 16
D_FF = 4 * D_MODEL
ROPE_THETA = 10000.0
ROPE_PAIRS = HEAD_DIM // 4
NORM_EPS = 1e-6
NEG = -1e30
ATTN_SCALE = HEAD_DIM ** -0.5

V7X_VMEM_BYTES = 64 * 1024 * 1024
VMEM_LIMIT = V7X_VMEM_BYTES - 8 * 1024 * 1024
LANES = 128
SUBLANES = 8

F32 = jnp.float32
BF16 = jnp.bfloat16


def _params(*sem):
    return pltpu.CompilerParams(dimension_semantics=sem, vmem_limit_bytes=VMEM_LIMIT)


ADA_ROWS = 3
ADA_TN = 1024


def _ada_kernel(cb_ref, w_ref, b_ref, o_ref, s_ref):
    @pl.when((pl.program_id(0) == 0) & (pl.program_id(1) == 0))
    def _():
        cvals = cb_ref[...]
        s_ref[...] = cvals * (1.0 / (1.0 + jnp.exp(-cvals)))

    nchunk = ADA_TN // LANES

    def body(kk, accs):
        k0 = pl.multiple_of(kk * SUBLANES, SUBLANES)
        w = w_ref[pl.ds(k0, SUBLANES), :]
        out = []
        for r in range(ADA_ROWS):
            s = s_ref[r, pl.ds(k0, SUBLANES), :]
            for cch in range(nchunk):
                out.append(accs[r * nchunk + cch] + w[:, cch * LANES:(cch + 1) * LANES] * s)
        return tuple(out)

    zeros = tuple(jnp.zeros((SUBLANES, LANES), F32) for _ in range(ADA_ROWS * nchunk))
    accs = lax.fori_loop(0, D_MODEL // SUBLANES, body, zeros, unroll=2)
    o_ref[...] = jnp.broadcast_to(b_ref[...], o_ref.shape)
    for r in range(ADA_ROWS):
        row = jnp.concatenate(
            [jnp.sum(accs[r * nchunk + cch], axis=0, keepdims=True) for cch in range(nchunk)], axis=1)
        o_ref[r:r + 1, :] = row + b_ref[...]


def _ada_mod(cc, ada_w, ada_b):
    depth, d, n = ada_w.shape
    cb = jnp.broadcast_to(cc[:, :, None], (ADA_ROWS, d, LANES))
    return pl.pallas_call(
        _ada_kernel,
        out_shape=jax.ShapeDtypeStruct((depth, SUBLANES, n), F32),
        grid=(depth, n // ADA_TN),
        in_specs=[
            pl.BlockSpec((ADA_ROWS, d, LANES), lambda l, j: (0, 0, 0)),
            pl.BlockSpec((None, d, ADA_TN), lambda l, j: (l, 0, j)),
            pl.BlockSpec((None, 1, ADA_TN), lambda l, j: (l, 0, j)),
        ],
        out_specs=pl.BlockSpec((None, SUBLANES, ADA_TN), lambda l, j: (l, 0, j)),
        scratch_shapes=[pltpu.VMEM((ADA_ROWS, d, LANES), F32)],
        compiler_params=_params("arbitrary", "arbitrary"),
        name="ada_mod",
    )(cb, ada_w, ada_b.reshape(depth, 1, n))


def _norm_modulate(x, nw, shift, scale):
    ms = jnp.mean(x * x, axis=-1, keepdims=True)
    y = x * lax.rsqrt(ms + NORM_EPS) * nw
    return y * (1.0 + scale) + shift


def _rope(y, cos, sin_lo, sin_hi):
    return (y * cos + pltpu.roll(y, HEAD_DIM - ROPE_PAIRS, 1) * sin_lo
            + pltpu.roll(y, ROPE_PAIRS, 1) * sin_hi)


def _rope_tables(seq):
    t = jnp.arange(seq)
    row = (t // GRID_W).astype(F32)
    col = (t % GRID_W).astype(F32)
    inv = ROPE_THETA ** (-jnp.arange(ROPE_PAIRS, dtype=F32) / ROPE_PAIRS)
    ang_r = row[:, None] * inv
    ang_c = col[:, None] * inv
    ang = jnp.concatenate([ang_r, ang_r, ang_c, ang_c], axis=-1)
    cos, sin = jnp.cos(ang), jnp.sin(ang)
    first = (jnp.arange(HEAD_DIM) % (2 * ROPE_PAIRS)) < ROPE_PAIRS
    sin_lo = jnp.where(first, -sin, 0.0)
    sin_hi = jnp.where(first, 0.0, sin)
    return cos, sin_lo, sin_hi


def _proj_kernel(*refs, sections, rope):
    x_ref, nw_ref, sh_ref, sc_ref, w_ref = refs[:5]
    rest = refs[5:]
    if sections is not None:
        qn_ref, kn_ref = rest[:2]
        rest = rest[2:]
    if rope:
        cos_ref, slo_ref, shi_ref = rest[:3]
        rest = rest[3:]
    o_ref, h_ref = rest

    @pl.when(pl.program_id(2) == 0)
    def _():
        h = _norm_modulate(x_ref[...], nw_ref[...], sh_ref[...], sc_ref[...])
        h_ref[...] = h.astype(BF16)

    if sections is None:
        o_ref[...] = jnp.dot(h_ref[...], w_ref[...], preferred_element_type=F32).astype(o_ref.dtype)
        return

    for c0, nh, kind in sections:
        y = jnp.dot(h_ref[...], w_ref[:, c0:c0 + nh * HEAD_DIM], preferred_element_type=F32)
        for hh in range(nh):
            yh = y[:, hh * HEAD_DIM:(hh + 1) * HEAD_DIM]
            if kind in ("q_norm", "k_norm"):
                hw = (qn_ref if kind == "q_norm" else kn_ref)[...]
                yh = yh * lax.rsqrt(jnp.mean(yh * yh, axis=-1, keepdims=True) + NORM_EPS) * hw
            if rope and kind != "v":
                yh = _rope(yh, cos_ref[...], slo_ref[...], shi_ref[...])
            lo = c0 + hh * HEAD_DIM
            o_ref[:, lo:lo + HEAD_DIM] = yh.astype(o_ref.dtype)


def _proj(x, nw, shift, scale, w, *, tm, tn, n_off=0, n_out=None, sections=None,
          head_norms=None, rope_tabs=None, name="proj"):
    b, s, d = x.shape
    n_out = w.shape[1] if n_out is None else n_out
    nj = n_out // tn
    in_specs = [
        pl.BlockSpec((None, tm, d), lambda bi, i, j: (bi, i, 0)),
        pl.BlockSpec((1, d), lambda bi, i, j: (0, 0)),
        pl.BlockSpec((None, 1, d), lambda bi, i, j: (bi, 0, 0)),
        pl.BlockSpec((None, 1, d), lambda bi, i, j: (bi, 0, 0)),
        pl.BlockSpec((d, tn), lambda bi, i, j: (0, j + n_off)),
    ]
    args = [x, nw.reshape(1, d), shift, scale, w]
    if sections is not None:
        assert nj == 1
        in_specs += [pl.BlockSpec((1, HEAD_DIM), lambda bi, i, j: (0, 0))] * 2
        args += [head_norms[0].reshape(1, HEAD_DIM), head_norms[1].reshape(1, HEAD_DIM)]
    if rope_tabs is not None:
        in_specs += [pl.BlockSpec((tm, HEAD_DIM), lambda bi, i, j: (i, 0))] * 3
        args += list(rope_tabs)
    return pl.pallas_call(
        functools.partial(_proj_kernel, sections=sections, rope=rope_tabs is not None),
        out_shape=jax.ShapeDtypeStruct((b, s, n_out), BF16),
        grid=(b, s // tm, nj),
        in_specs=in_specs,
        out_specs=pl.BlockSpec((None, tm, tn), lambda bi, i, j: (bi, i, j)),
        scratch_shapes=[pltpu.VMEM((tm, d), BF16)],
        compiler_params=_params("arbitrary", "arbitrary", "arbitrary"),
        name=name,
    )(*args)


def _qk(q, k):
    return lax.dot_general(q, k, (((1,), (1,)), ((), ())), preferred_element_type=F32)


def _softmax_pv(parts, sink=None):
    m = functools.reduce(jnp.maximum, [jnp.max(s, axis=-1, keepdims=True) for s, _ in parts])
    if sink is not None:
        m = jnp.maximum(m, sink)
    l = None
    o = None
    for s, v in parts:
        p = jnp.exp(s - m)
        ls = jnp.sum(p, axis=-1, keepdims=True)
        os_ = jnp.dot(p.astype(BF16), v, preferred_element_type=F32)
        l = ls if l is None else l + ls
        o = os_ if o is None else o + os_
    if sink is not None:
        l = l + jnp.exp(sink - m)
    return o / l


GATTN_TQ = 256


def _gattn_kernel(q_ref, k_ref, v_ref, kc_ref, vc_ref, o_ref):
    k, v, kc, vc = k_ref[...], v_ref[...], kc_ref[...], vc_ref[...]
    for g in range(GQA_GROUP):
        q = q_ref[:, g * HEAD_DIM:(g + 1) * HEAD_DIM]
        o = _softmax_pv([(_qk(q, kc) * ATTN_SCALE, vc), (_qk(q, k) * ATTN_SCALE, v)])
        o_ref[:, g * HEAD_DIM:(g + 1) * HEAD_DIM] = o.astype(o_ref.dtype)


def _gattn(qkv, qkv_c, *, q_blk, k_blk, v_blk):
    b, s, _ = qkv.shape
    c = qkv_c.shape[1]
    gw = GQA_GROUP * HEAD_DIM
    return pl.pallas_call(
        _gattn_kernel,
        out_shape=jax.ShapeDtypeStruct((b, s, A_Q_HEADS * HEAD_DIM), BF16),
        grid=(b, A_KV_HEADS, s // GATTN_TQ),
        in_specs=[
            pl.BlockSpec((None, GATTN_TQ, gw), lambda bi, kv, i: (bi, i, q_blk + kv)),
            pl.BlockSpec((None, s, HEAD_DIM), lambda bi, kv, i: (bi, 0, k_blk + kv)),
            pl.BlockSpec((None, s, HEAD_DIM), lambda bi, kv, i: (bi, 0, v_blk + kv)),
            pl.BlockSpec((None, c, HEAD_DIM), lambda bi, kv, i: (bi, 0, k_blk + kv)),
            pl.BlockSpec((None, c, HEAD_DIM), lambda bi, kv, i: (bi, 0, v_blk + kv)),
        ],
        out_specs=pl.BlockSpec((None, GATTN_TQ, gw), lambda bi, kv, i: (bi, i, kv)),
        compiler_params=_params("arbitrary", "arbitrary", "arbitrary"),
        name="gattn",
    )(qkv, qkv, qkv, qkv_c, qkv_c)


WATTN_TQ = 4 * WINDOW


def _wattn_kernel(sink_ref, q_ref, kp_ref, km_ref, kn_ref, vp_ref, vm_ref, vn_ref,
                  kc_ref, vc_ref, o_ref, *, seq):
    kv = pl.program_id(1)
    i = pl.program_id(2)
    k = jnp.concatenate([kp_ref[...], km_ref[...], kn_ref[...]], axis=0)
    v = jnp.concatenate([vp_ref[...], vm_ref[...], vn_ref[...]], axis=0)
    nk = WATTN_TQ + 2 * WINDOW
    kpos = i * WATTN_TQ - WINDOW + lax.broadcasted_iota(jnp.int32, (1, nk), 1)
    kpos = jnp.where((kpos >= 0) & (kpos < seq), kpos, -(4 * seq))
    qpos = i * WATTN_TQ + lax.broadcasted_iota(jnp.int32, (WATTN_TQ, 1), 0)
    valid = jnp.abs(kpos - qpos) <= WINDOW
    kc, vc = kc_ref[...], vc_ref[...]
    for g in range(GQA_GROUP):
        q = q_ref[:, g * HEAD_DIM:(g + 1) * HEAD_DIM]
        s_win = jnp.where(valid, _qk(q, k) * ATTN_SCALE, NEG)
        o = _softmax_pv([(_qk(q, kc) * ATTN_SCALE, vc), (s_win, v)], sink=sink_ref[kv, g])
        o_ref[:, g * HEAD_DIM:(g + 1) * HEAD_DIM] = o.astype(o_ref.dtype)


def _wattn(sink_kg, qkv, qkv_c, *, q_blk, k_blk, v_blk):
    b, s, _ = qkv.shape
    c = qkv_c.shape[1]
    gw = GQA_GROUP * HEAD_DIM
    r = WATTN_TQ // WINDOW
    last = s // WINDOW - 1

    def prev_map(col):
        return lambda bi, kv, i: (bi, jnp.maximum(i * r - 1, 0), col + kv)

    def main_map(col):
        return lambda bi, kv, i: (bi, i, col + kv)

    def next_map(col):
        return lambda bi, kv, i: (bi, jnp.minimum(i * r + r, last), col + kv)

    return pl.pallas_call(
        functools.partial(_wattn_kernel, seq=s),
        out_shape=jax.ShapeDtypeStruct((b, s, B_Q_HEADS * HEAD_DIM), BF16),
        grid=(b, B_KV_HEADS, s // WATTN_TQ),
        in_specs=[
            pl.BlockSpec(memory_space=pltpu.SMEM),
            pl.BlockSpec((None, WATTN_TQ, gw), lambda bi, kv, i: (bi, i, q_blk + kv)),
            pl.BlockSpec((None, WINDOW, HEAD_DIM), prev_map(k_blk)),
            pl.BlockSpec((None, WATTN_TQ, HEAD_DIM), main_map(k_blk)),
            pl.BlockSpec((None, WINDOW, HEAD_DIM), next_map(k_blk)),
            pl.BlockSpec((None, WINDOW, HEAD_DIM), prev_map(v_blk)),
            pl.BlockSpec((None, WATTN_TQ, HEAD_DIM), main_map(v_blk)),
            pl.BlockSpec((None, WINDOW, HEAD_DIM), next_map(v_blk)),
            pl.BlockSpec((None, c, HEAD_DIM), lambda bi, kv, i: (bi, 0, k_blk + kv)),
            pl.BlockSpec((None, c, HEAD_DIM), lambda bi, kv, i: (bi, 0, v_blk + kv)),
        ],
        out_specs=pl.BlockSpec((None, WATTN_TQ, gw), lambda bi, kv, i: (bi, i, kv)),
        compiler_params=_params("arbitrary", "arbitrary", "arbitrary"),
        name="wattn",
    )(sink_kg, qkv, qkv, qkv, qkv, qkv, qkv, qkv, qkv_c, qkv_c)


def _cattn_kernel(sink_ref, q_ref, k_ref, v_ref, o_ref):
    grp = pl.program_id(1)
    k, v = k_ref[...], v_ref[...]
    for g in range(GQA_GROUP):
        q = q_ref[:, g * HEAD_DIM:(g + 1) * HEAD_DIM]
        o = _softmax_pv([(_qk(q, k) * ATTN_SCALE, v)], sink=sink_ref[grp, g])
        o_ref[:, g * HEAD_DIM:(g + 1) * HEAD_DIM] = o.astype(o_ref.dtype)


def _cattn(sink_all, qkv_c):
    b, c, _ = qkv_c.shape
    gw = GQA_GROUP * HEAD_DIM
    n_grp = A_KV_HEADS + B_KV_HEADS
    q_map = lambda bi, g: (bi, 0, g + g // A_KV_HEADS)
    k_map = lambda bi, g: (bi, 0, 8 + g + 10 * (g // A_KV_HEADS))
    v_map = lambda bi, g: (bi, 0, 10 + g + 10 * (g // A_KV_HEADS))
    return pl.pallas_call(
        _cattn_kernel,
        out_shape=jax.ShapeDtypeStruct((b, c, N_HEADS * HEAD_DIM), BF16),
        grid=(b, n_grp),
        in_specs=[
            pl.BlockSpec(memory_space=pltpu.SMEM),
            pl.BlockSpec((None, c, gw), q_map),
            pl.BlockSpec((None, c, HEAD_DIM), k_map),
            pl.BlockSpec((None, c, HEAD_DIM), v_map),
        ],
        out_specs=pl.BlockSpec((None, c, gw), lambda bi, g: (bi, 0, g)),
        compiler_params=_params("arbitrary", "arbitrary"),
        name="cattn",
    )(sink_all, qkv_c, qkv_c, qkv_c)


NATTN_ROWS = 4
NATTN_TQ = NATTN_ROWS * GRID_W
NATTN_KROWS = 3 * NATTN_ROWS


def _nattn_kernel(q_ref, kp_ref, km_ref, kn_ref, vp_ref, vm_ref, vn_ref, kc_ref, vc_ref,
                  bias_ref, o_ref):
    q = q_ref[...]
    k = jnp.concatenate([kp_ref[...], km_ref[...], kn_ref[...]], axis=0)
    v = jnp.concatenate([vp_ref[...], vm_ref[...], vn_ref[...]], axis=0)
    s_nb = _qk(q, k) * ATTN_SCALE + bias_ref[...]
    o = _softmax_pv([(_qk(q, kc_ref[...]) * ATTN_SCALE, vc_ref[...]), (s_nb, v)])
    o_ref[...] = o.astype(o_ref.dtype)


def _na_bias_tables(rpb, rows):
    kh = min(NA_KH, rows)
    col = np.arange(GRID_W)
    cs = np.clip(col - NA_KW // 2, 0, GRID_W - NA_KW)
    col_valid = (col[None, :] >= cs[:, None]) & (col[None, :] < cs[:, None] + NA_KW)
    ci = np.clip(col[None, :] - col[:, None] + NA_KW - 1, 0, 2 * NA_KW - 2)
    tabs = []
    for r0 in (0, NATTN_ROWS, rows - NATTN_ROWS):
        r = r0 + np.arange(NATTN_ROWS)
        kr = r0 - NATTN_ROWS + np.arange(NATTN_KROWS)
        rs = np.clip(r - kh // 2, 0, rows - kh)
        row_valid = (kr[None, :] >= rs[:, None]) & (kr[None, :] < rs[:, None] + kh)
        ri = np.clip(kr[None, :] - r[:, None] + NA_KH - 1, 0, 2 * NA_KH - 2)
        valid = row_valid[:, None, :, None] & col_valid[None, :, None, :]
        gathered = rpb[:, ri[:, None, :, None], ci[None, :, None, :]]
        tab = jnp.where(valid[None], gathered, NEG)
        tabs.append(tab.reshape(rpb.shape[0], NATTN_TQ, NATTN_KROWS * GRID_W))
    return jnp.stack(tabs, axis=1).astype(F32)


def _nattn(qkv, kv_c, bias):
    b, s, _ = qkv.shape
    c = kv_c.shape[1]
    nt = s // NATTN_TQ
    h = C_HEADS

    def prev_map(col):
        return lambda bi, hh, i: (bi, jnp.maximum(i - 1, 0), col + hh)

    def main_map(col):
        return lambda bi, hh, i: (bi, i, col + hh)

    def next_map(col):
        return lambda bi, hh, i: (bi, jnp.minimum(i + 1, nt - 1), col + hh)

    def bias_map(bi, hh, i):
        return (hh, jnp.where(i == 0, 0, jnp.where(i == nt - 1, 2, 1)), 0, 0)

    blk = (None, NATTN_TQ, HEAD_DIM)
    return pl.pallas_call(
        _nattn_kernel,
        out_shape=jax.ShapeDtypeStruct((b, s, h * HEAD_DIM), BF16),
        grid=(b, h, nt),
        in_specs=[
            pl.BlockSpec(blk, main_map(0)),
            pl.BlockSpec(blk, prev_map(h)), pl.BlockSpec(blk, main_map(h)), pl.BlockSpec(blk, next_map(h)),
            pl.BlockSpec(blk, prev_map(2 * h)), pl.BlockSpec(blk, main_map(2 * h)),
            pl.BlockSpec(blk, next_map(2 * h)),
            pl.BlockSpec((None, c, HEAD_DIM), lambda bi, hh, i: (bi, 0, hh)),
            pl.BlockSpec((None, c, HEAD_DIM), lambda bi, hh, i: (bi, 0, h + hh)),
            pl.BlockSpec((None, None, NATTN_TQ, NATTN_KROWS * GRID_W), bias_map),
        ],
        out_specs=pl.BlockSpec(blk, main_map(0)),
        compiler_params=_params("arbitrary", "arbitrary", "arbitrary"),
        name="nattn",
    )(qkv, qkv, qkv, qkv, qkv, qkv, qkv, kv_c, kv_c, bias)


def _outproj_kernel(*refs, widths):
    x_ref, w_ref, g_ref = refs[:3]
    o_refs = refs[3:3 + len(widths)]
    y_ref = refs[3 + len(widths)]
    y = None
    k0 = 0
    for o_ref, kw in zip(o_refs, widths):
        part = jnp.dot(o_ref[...], w_ref[k0:k0 + kw, :], preferred_element_type=F32)
        y = part if y is None else y + part
        k0 += kw
    y_ref[...] = x_ref[...] + g_ref[...] * y


def _outproj(x, o_parts, w, gate, *, tm):
    b, s, d = x.shape
    widths = tuple(o.shape[2] for o in o_parts)
    assert sum(widths) == w.shape[0]
    return pl.pallas_call(
        functools.partial(_outproj_kernel, widths=widths),
        out_shape=jax.ShapeDtypeStruct(x.shape, x.dtype),
        grid=(b, s // tm),
        in_specs=[
            pl.BlockSpec((None, tm, d), lambda bi, i: (bi, i, 0)),
            pl.BlockSpec(w.shape, lambda bi, i: (0, 0)),
            pl.BlockSpec((None, 1, d), lambda bi, i: (bi, 0, 0)),
        ] + [pl.BlockSpec((None, tm, kw), lambda bi, i: (bi, i, 0)) for kw in widths],
        out_specs=pl.BlockSpec((None, tm, d), lambda bi, i: (bi, i, 0)),
        compiler_params=_params("arbitrary", "arbitrary"),
        name="outproj",
    )(x, w, gate, *o_parts)


def _mlp_kernel(*refs, final_norm):
    x_ref, nw_ref, sh_ref, sc_ref, g_ref, w1_ref, w2_ref = refs[:7]
    rest = refs[7:]
    if final_norm:
        fw_ref = rest[0]
        rest = rest[1:]
    y_ref, h_ref, acc_ref = rest
    f = pl.program_id(2)

    @pl.when(f == 0)
    def _():
        h = _norm_modulate(x_ref[...], nw_ref[...], sh_ref[...], sc_ref[...])
        h_ref[...] = h.astype(BF16)
        acc_ref[...] = jnp.zeros_like(acc_ref)

    a = jnp.dot(h_ref[...], w1_ref[...], preferred_element_type=F32)
    a = jnp.square(jnp.maximum(a, 0.0)).astype(BF16)
    acc_ref[...] += jnp.dot(a, w2_ref[...], preferred_element_type=F32)

    @pl.when(f == pl.num_programs(2) - 1)
    def _():
        y = x_ref[...] + g_ref[...] * acc_ref[...]
        if final_norm:
            y = y * lax.rsqrt(jnp.mean(y * y, axis=-1, keepdims=True) + NORM_EPS) * fw_ref[...]
        y_ref[...] = y


def _mlp(x, nw, shift, scale, gate, w1, w2, *, tm, tf, final_w=None):
    b, s, d = x.shape
    dff = w1.shape[1]
    vec = pl.BlockSpec((None, 1, d), lambda bi, i, f: (bi, 0, 0))
    in_specs = [
        pl.BlockSpec((None, tm, d), lambda bi, i, f: (bi, i, 0)),
        pl.BlockSpec((1, d), lambda bi, i, f: (0, 0)),
        vec, vec, vec,
        pl.BlockSpec((d, tf), lambda bi, i, f: (0, f)),
        pl.BlockSpec((tf, d), lambda bi, i, f: (f, 0)),
    ]
    args = [x, nw.reshape(1, d), shift, scale, gate, w1, w2]
    if final_w is not None:
        in_specs.append(pl.BlockSpec((1, d), lambda bi, i, f: (0, 0)))
        args.append(final_w.reshape(1, d))
    return pl.pallas_call(
        functools.partial(_mlp_kernel, final_norm=final_w is not None),
        out_shape=jax.ShapeDtypeStruct(x.shape, x.dtype),
        grid=(b, s // tm, dff // tf),
        in_specs=in_specs,
        out_specs=pl.BlockSpec((None, tm, d), lambda bi, i, f: (bi, i, 0)),
        scratch_shapes=[pltpu.VMEM((tm, d), BF16), pltpu.VMEM((tm, d), F32)],
        compiler_params=_params("arbitrary", "arbitrary", "arbitrary"),
        name="mlp",
    )(*args)


EVEN_SECTIONS = (
    (0, A_Q_HEADS, "q_norm"),
    (A_Q_HEADS * HEAD_DIM, A_KV_HEADS, "k_norm"),
    ((A_Q_HEADS + A_KV_HEADS) * HEAD_DIM, A_KV_HEADS, "v"),
    ((A_Q_HEADS + 2 * A_KV_HEADS) * HEAD_DIM, B_Q_HEADS, "q"),
    ((A_Q_HEADS + 2 * A_KV_HEADS + B_Q_HEADS) * HEAD_DIM, B_KV_HEADS, "k"),
    ((A_Q_HEADS + 2 * A_KV_HEADS + B_Q_HEADS + B_KV_HEADS) * HEAD_DIM, B_KV_HEADS, "v"),
)
LAT_TM = 512


def kernel(x, c, ctx, c_ctx, ada_w, ada_b, norm_w, mlp_w1, mlp_w2, ev_w_in, ev_w_out, ev_q_norm,
           ev_k_norm, ev_sink, od_w_in, od_w_out, od_rpb, final_norm_w):
    b, s, d = x.shape
    n_ctx = ctx.shape[1]
    depth = ada_w.shape[0]
    rows = s // GRID_W

    mod = _ada_mod(jnp.concatenate([c, c_ctx[None, :]], axis=0), ada_w, ada_b)
    rope_tabs = _rope_tables(s)

    def mod_vecs(layer):
        lat = [mod[layer, :b, k * d:(k + 1) * d][:, None, :] for k in range(6)]
        cx = [jnp.broadcast_to(mod[layer, b:b + 1, k * d:(k + 1) * d][:, None, :], (b, 1, d))
              for k in range(6)]
        return lat, cx

    for layer in range(depth):
        need_ctx = layer < depth - 1
        (sh1, sc1, g1, sh2, sc2, g2), (sh1c, sc1c, g1c, sh2c, sc2c, g2c) = mod_vecs(layer)
        j = layer // 2
        w1 = mlp_w1[layer].astype(BF16)
        w2 = mlp_w2[layer].astype(BF16)
        if layer % 2 == 0:
            w_in = ev_w_in[j].astype(BF16)
            w_out = ev_w_out[j].astype(BF16)
            n_in = w_in.shape[1]
            norms = (ev_q_norm[j], ev_k_norm[j])
            qkv = _proj(x, norm_w[layer, 0], sh1, sc1, w_in, tm=LAT_TM, tn=n_in,
                        sections=EVEN_SECTIONS, head_norms=norms, rope_tabs=rope_tabs, name="proj_even")
            qkv_c = _proj(ctx, norm_w[layer, 0], sh1c, sc1c, w_in, tm=n_ctx, tn=n_in,
                          sections=EVEN_SECTIONS, head_norms=norms, name="proj_even_ctx")
            oa = _gattn(qkv, qkv_c, q_blk=0, k_blk=A_Q_HEADS, v_blk=A_Q_HEADS + A_KV_HEADS)
            sink_kg = ev_sink[j].reshape(B_KV_HEADS, GQA_GROUP).astype(F32)
            qb_blk = (A_Q_HEADS + 2 * A_KV_HEADS) // GQA_GROUP
            kb_blk = A_Q_HEADS + 2 * A_KV_HEADS + B_Q_HEADS
            ob = _wattn(sink_kg, qkv, qkv_c, q_blk=qb_blk, k_blk=kb_blk, v_blk=kb_blk + B_KV_HEADS)
            o_parts = [oa, ob]
            if need_ctx:
                sink_all = jnp.concatenate([jnp.full((A_KV_HEADS, GQA_GROUP), NEG, F32), sink_kg], axis=0)
                oc_parts = [_cattn(sink_all, qkv_c)]
        else:
            w_in = od_w_in[j].astype(BF16)
            w_out = od_w_out[j].astype(BF16)
            hd = C_HEADS * HEAD_DIM
            qkv = _proj(x, norm_w[layer, 0], sh1, sc1, w_in, tm=LAT_TM, tn=hd, name="proj_odd")
            kv_c = _proj(ctx, norm_w[layer, 0], sh1c, sc1c, w_in, tm=n_ctx, tn=hd, n_off=1,
                         n_out=2 * hd, name="proj_odd_ctx")
            o_parts = [_nattn(qkv, kv_c, _na_bias_tables(od_rpb[j], rows))]
            assert not need_ctx, "the odd layer is the last one at this depth"
        x = _outproj(x, o_parts, w_out, g1, tm=LAT_TM)
        x = _mlp(x, norm_w[layer, 1], sh2, sc2, g2, w1, w2, tm=LAT_TM, tf=1024,
                 final_w=None if need_ctx else final_norm_w)
        if need_ctx:
            ctx = _outproj(ctx, oc_parts, w_out, g1c, tm=n_ctx)
            ctx = _mlp(ctx, norm_w[layer, 1], sh2c, sc2c, g2c, w1, w2, tm=n_ctx, tf=1024)
    return x
```

```python
import functools

import jax
import jax.numpy as jnp
import numpy as np
from jax import lax
from jax.experimental import pallas as pl
from jax.experimental.pallas import tpu as pltpu

D_MODEL = 2048
GRID_W = 64
HEAD_DIM = 128
N_HEADS = D_MODEL // HEAD_DIM
A_Q_HEADS = N_HEADS // 2
A_KV_HEADS = max(1, A_Q_HEADS // 4)
B_Q_HEADS = N_HEADS - A_Q_HEADS
B_KV_HEADS = max(1, B_Q_HEADS // 4)
GQA_GROUP = A_Q_HEADS // A_KV_HEADS
C_HEADS = N_HEADS
WINDOW = 128
NA_KH = 8
NA_KW = 16
D_FF = 4 * D_MODEL
ROPE_THETA = 10000.0
ROPE_PAIRS = HEAD_DIM // 4
NORM_EPS = 1e-6
NEG = -1e30
ATTN_SCALE = HEAD_DIM ** -0.5

V7X_VMEM_BYTES = 64 * 1024 * 1024
VMEM_LIMIT = V7X_VMEM_BYTES - 8 * 1024 * 1024
LANES = 128
SUBLANES = 8

F32 = jnp.float32
BF16 = jnp.bfloat16


def _params(*sem):
    return pltpu.CompilerParams(dimension_semantics=sem, vmem_limit_bytes=VMEM_LIMIT)


ADA_ROWS = 3
ADA_TN = 1024


def _ada_kernel(cb_ref, w_ref, b_ref, o_ref, s_ref):
    @pl.when((pl.program_id(0) == 0) & (pl.program_id(1) == 0))
    def _():
        cvals = cb_ref[...]
        s_ref[...] = cvals * (1.0 / (1.0 + jnp.exp(-cvals)))

    nchunk = ADA_TN // LANES

    def body(kk, accs):
        k0 = pl.multiple_of(kk * SUBLANES, SUBLANES)
        w = w_ref[pl.ds(k0, SUBLANES), :]
        out = []
        for r in range(ADA_ROWS):
            s = s_ref[r, pl.ds(k0, SUBLANES), :]
            for cch in range(nchunk):
                out.append(accs[r * nchunk + cch] + w[:, cch * LANES:(cch + 1) * LANES] * s)
        return tuple(out)

    zeros = tuple(jnp.zeros((SUBLANES, LANES), F32) for _ in range(ADA_ROWS * nchunk))
    accs = lax.fori_loop(0, D_MODEL // SUBLANES, body, zeros, unroll=2)
    o_ref[...] = jnp.broadcast_to(b_ref[...], o_ref.shape)
    for r in range(ADA_ROWS):
        row = jnp.concatenate(
            [jnp.sum(accs[r * nchunk + cch], axis=0, keepdims=True) for cch in range(nchunk)], axis=1)
        o_ref[r:r + 1, :] = row + b_ref[...]


def _ada_mod(cc, ada_w, ada_b):
    depth, d, n = ada_w.shape
    cb = jnp.broadcast_to(cc[:, :, None], (ADA_ROWS, d, LANES))
    return pl.pallas_call(
        _ada_kernel,
        out_shape=jax.ShapeDtypeStruct((depth, SUBLANES, n), F32),
        grid=(depth, n // ADA_TN),
        in_specs=[
            pl.BlockSpec((ADA_ROWS, d, LANES), lambda l, j: (0, 0, 0)),
            pl.BlockSpec((None, d, ADA_TN), lambda l, j: (l, 0, j)),
            pl.BlockSpec((None, 1, ADA_TN), lambda l, j: (l, 0, j)),
        ],
        out_specs=pl.BlockSpec((None, SUBLANES, ADA_TN), lambda l, j: (l, 0, j)),
        scratch_shapes=[pltpu.VMEM((ADA_ROWS, d, LANES), F32)],
        compiler_params=_params("arbitrary", "arbitrary"),
        name="ada_mod",
    )(cb, ada_w, ada_b.reshape(depth, 1, n))


def _norm_modulate(x, nw, shift, scale):
    ms = jnp.mean(x * x, axis=-1, keepdims=True)
    y = x * lax.rsqrt(ms + NORM_EPS) * nw
    return y * (1.0 + scale) + shift


def _rope(y, cos, sin_lo, sin_hi):
    return (y * cos + pltpu.roll(y, HEAD_DIM - ROPE_PAIRS, 1) * sin_lo
            + pltpu.roll(y, ROPE_PAIRS, 1) * sin_hi)


def _rope_tables(seq):
    t = jnp.arange(seq)
    row = (t // GRID_W).astype(F32)
    col = (t % GRID_W).astype(F32)
    inv = ROPE_THETA ** (-jnp.arange(ROPE_PAIRS, dtype=F32) / ROPE_PAIRS)
    ang_r = row[:, None] * inv
    ang_c = col[:, None] * inv
    ang = jnp.concatenate([ang_r, ang_r, ang_c, ang_c], axis=-1)
    cos, sin = jnp.cos(ang), jnp.sin(ang)
    first = (jnp.arange(HEAD_DIM) % (2 * ROPE_PAIRS)) < ROPE_PAIRS
    sin_lo = jnp.where(first, -sin, 0.0)
    sin_hi = jnp.where(first, 0.0, sin)
    return cos, sin_lo, sin_hi


def _proj_kernel(*refs, sections, rope):
    x_ref, nw_ref, sh_ref, sc_ref, w_ref = refs[:5]
    rest = refs[5:]
    if sections is not None:
        qn_ref, kn_ref = rest[:2]
        rest = rest[2:]
    if rope:
        cos_ref, slo_ref, shi_ref = rest[:3]
        rest = rest[3:]
    o_ref, h_ref = rest

    @pl.when(pl.program_id(2) == 0)
    def _():
        h = _norm_modulate(x_ref[...], nw_ref[...], sh_ref[...], sc_ref[...])
        h_ref[...] = h.astype(BF16)

    if sections is None:
        o_ref[...] = jnp.dot(h_ref[...], w_ref[...], preferred_element_type=F32).astype(o_ref.dtype)
        return

    for c0, nh, kind in sections:
        y = jnp.dot(h_ref[...], w_ref[:, c0:c0 + nh * HEAD_DIM], preferred_element_type=F32)
        for hh in range(nh):
            yh = y[:, hh * HEAD_DIM:(hh + 1) * HEAD_DIM]
            if kind in ("q_norm", "k_norm"):
                hw = (qn_ref if kind == "q_norm" else kn_ref)[...]
                yh = yh * lax.rsqrt(jnp.mean(yh * yh, axis=-1, keepdims=True) + NORM_EPS) * hw
            if rope and kind != "v":
                yh = _rope(yh, cos_ref[...], slo_ref[...], shi_ref[...])
            lo = c0 + hh * HEAD_DIM
            o_ref[:, lo:lo + HEAD_DIM] = yh.astype(o_ref.dtype)


def _proj(x, nw, shift, scale, w, *, tm, tn, n_off=0, n_out=None, sections=None,
          head_norms=None, rope_tabs=None, name="proj"):
    b, s, d = x.shape
    n_out = w.shape[1] if n_out is None else n_out
    nj = n_out // tn
    in_specs = [
        pl.BlockSpec((None, tm, d), lambda bi, i, j: (bi, i, 0)),
        pl.BlockSpec((1, d), lambda bi, i, j: (0, 0)),
        pl.BlockSpec((None, 1, d), lambda bi, i, j: (bi, 0, 0)),
        pl.BlockSpec((None, 1, d), lambda bi, i, j: (bi, 0, 0)),
        pl.BlockSpec((d, tn), lambda bi, i, j: (0, j + n_off)),
    ]
    args = [x, nw.reshape(1, d), shift, scale, w]
    if sections is not None:
        assert nj == 1
        in_specs += [pl.BlockSpec((1, HEAD_DIM), lambda bi, i, j: (0, 0))] * 2
        args += [head_norms[0].reshape(1, HEAD_DIM), head_norms[1].reshape(1, HEAD_DIM)]
    if rope_tabs is not None:
        in_specs += [pl.BlockSpec((tm, HEAD_DIM), lambda bi, i, j: (i, 0))] * 3
        args += list(rope_tabs)
    return pl.pallas_call(
        functools.partial(_proj_kernel, sections=sections, rope=rope_tabs is not None),
        out_shape=jax.ShapeDtypeStruct((b, s, n_out), BF16),
        grid=(b, s // tm, nj),
        in_specs=in_specs,
        out_specs=pl.BlockSpec((None, tm, tn), lambda bi, i, j: (bi, i, j)),
        scratch_shapes=[pltpu.VMEM((tm, d), BF16)],
        compiler_params=_params("arbitrary", "arbitrary", "arbitrary"),
        name=name,
    )(*args)


def _qk(q, k):
    return lax.dot_general(q, k, (((1,), (1,)), ((), ())), preferred_element_type=F32)


EXP2_SCALE = ATTN_SCALE * float(np.log2(np.e))
INV_ATTN_SCALE = 1.0 / ATTN_SCALE


def _softmax_pv(parts, sink=None):
    m = functools.reduce(jnp.maximum, [jnp.max(s, axis=-1, keepdims=True) for s, _ in parts])
    if sink is not None:
        m = jnp.maximum(m, sink)
    l = None
    o = None
    for s, v in parts:
        p = jnp.exp2((s - m) * EXP2_SCALE)
        ls = jnp.sum(p, axis=-1, keepdims=True)
        os_ = jnp.dot(p.astype(BF16), v, preferred_element_type=F32)
        l = ls if l is None else l + ls
        o = os_ if o is None else o + os_
    if sink is not None:
        l = l + jnp.exp2((sink - m) * EXP2_SCALE)
    return o / l


GATTN_TQ = 256


def _gattn_kernel(q_ref, k_ref, v_ref, kc_ref, vc_ref, o_ref):
    k, v, kc, vc = k_ref[...], v_ref[...], kc_ref[...], vc_ref[...]
    for g in range(GQA_GROUP):
        q = q_ref[:, g * HEAD_DIM:(g + 1) * HEAD_DIM]
        o = _softmax_pv([(_qk(q, kc), vc), (_qk(q, k), v)])
        o_ref[:, g * HEAD_DIM:(g + 1) * HEAD_DIM] = o.astype(o_ref.dtype)


def _gattn(qkv, qkv_c, *, q_blk, k_blk, v_blk):
    b, s, _ = qkv.shape
    c = qkv_c.shape[1]
    gw = GQA_GROUP * HEAD_DIM
    return pl.pallas_call(
        _gattn_kernel,
        out_shape=jax.ShapeDtypeStruct((b, s, A_Q_HEADS * HEAD_DIM), BF16),
        grid=(b, A_KV_HEADS, s // GATTN_TQ),
        in_specs=[
            pl.BlockSpec((None, GATTN_TQ, gw), lambda bi, kv, i: (bi, i, q_blk + kv)),
            pl.BlockSpec((None, s, HEAD_DIM), lambda bi, kv, i: (bi, 0, k_blk + kv)),
            pl.BlockSpec((None, s, HEAD_DIM), lambda bi, kv, i: (bi, 0, v_blk + kv)),
            pl.BlockSpec((None, c, HEAD_DIM), lambda bi, kv, i: (bi, 0, k_blk + kv)),
            pl.BlockSpec((None, c, HEAD_DIM), lambda bi, kv, i: (bi, 0, v_blk + kv)),
        ],
        out_specs=pl.BlockSpec((None, GATTN_TQ, gw), lambda bi, kv, i: (bi, i, kv)),
        compiler_params=_params("arbitrary", "arbitrary", "arbitrary"),
        name="gattn",
    )(qkv, qkv, qkv, qkv_c, qkv_c)


WATTN_TQ = 4 * WINDOW


def _wattn_kernel(sink_ref, q_ref, kp_ref, km_ref, kn_ref, vp_ref, vm_ref, vn_ref,
                  kc_ref, vc_ref, o_ref, *, seq):
    kv = pl.program_id(1)
    i = pl.program_id(2)
    k = jnp.concatenate([kp_ref[...], km_ref[...], kn_ref[...]], axis=0)
    v = jnp.concatenate([vp_ref[...], vm_ref[...], vn_ref[...]], axis=0)
    nk = WATTN_TQ + 2 * WINDOW
    kpos = i * WATTN_TQ - WINDOW + lax.broadcasted_iota(jnp.int32, (1, nk), 1)
    kpos = jnp.where((kpos >= 0) & (kpos < seq), kpos, -(4 * seq))
    qpos = i * WATTN_TQ + lax.broadcasted_iota(jnp.int32, (WATTN_TQ, 1), 0)
    valid = jnp.abs(kpos - qpos) <= WINDOW
    kc, vc = kc_ref[...], vc_ref[...]
    for g in range(GQA_GROUP):
        q = q_ref[:, g * HEAD_DIM:(g + 1) * HEAD_DIM]
        s_win = jnp.where(valid, _qk(q, k), NEG)
        o = _softmax_pv([(_qk(q, kc), vc), (s_win, v)], sink=sink_ref[kv, g] * INV_ATTN_SCALE)
        o_ref[:, g * HEAD_DIM:(g + 1) * HEAD_DIM] = o.astype(o_ref.dtype)


def _wattn(sink_kg, qkv, qkv_c, *, q_blk, k_blk, v_blk):
    b, s, _ = qkv.shape
    c = qkv_c.shape[1]
    gw = GQA_GROUP * HEAD_DIM
    r = WATTN_TQ // WINDOW
    last = s // WINDOW - 1

    def prev_map(col):
        return lambda bi, kv, i: (bi, jnp.maximum(i * r - 1, 0), col + kv)

    def main_map(col):
        return lambda bi, kv, i: (bi, i, col + kv)

    def next_map(col):
        return lambda bi, kv, i: (bi, jnp.minimum(i * r + r, last), col + kv)

    return pl.pallas_call(
        functools.partial(_wattn_kernel, seq=s),
        out_shape=jax.ShapeDtypeStruct((b, s, B_Q_HEADS * HEAD_DIM), BF16),
        grid=(b, B_KV_HEADS, s // WATTN_TQ),
        in_specs=[
            pl.BlockSpec(memory_space=pltpu.SMEM),
            pl.BlockSpec((None, WATTN_TQ, gw), lambda bi, kv, i: (bi, i, q_blk + kv)),
            pl.BlockSpec((None, WINDOW, HEAD_DIM), prev_map(k_blk)),
            pl.BlockSpec((None, WATTN_TQ, HEAD_DIM), main_map(k_blk)),
            pl.BlockSpec((None, WINDOW, HEAD_DIM), next_map(k_blk)),
            pl.BlockSpec((None, WINDOW, HEAD_DIM), prev_map(v_blk)),
            pl.BlockSpec((None, WATTN_TQ, HEAD_DIM), main_map(v_blk)),
            pl.BlockSpec((None, WINDOW, HEAD_DIM), next_map(v_blk)),
            pl.BlockSpec((None, c, HEAD_DIM), lambda bi, kv, i: (bi, 0, k_blk + kv)),
            pl.BlockSpec((None, c, HEAD_DIM), lambda bi, kv, i: (bi, 0, v_blk + kv)),
        ],
        out_specs=pl.BlockSpec((None, WATTN_TQ, gw), lambda bi, kv, i: (bi, i, kv)),
        compiler_params=_params("arbitrary", "arbitrary", "arbitrary"),
        name="wattn",
    )(sink_kg, qkv, qkv, qkv, qkv, qkv, qkv, qkv, qkv_c, qkv_c)


def _cattn_kernel(sink_ref, q_ref, k_ref, v_ref, o_ref):
    grp = pl.program_id(1)
    k, v = k_ref[...], v_ref[...]
    for g in range(GQA_GROUP):
        q = q_ref[:, g * HEAD_DIM:(g + 1) * HEAD_DIM]
        o = _softmax_pv([(_qk(q, k), v)], sink=sink_ref[grp, g] * INV_ATTN_SCALE)
        o_ref[:, g * HEAD_DIM:(g + 1) * HEAD_DIM] = o.astype(o_ref.dtype)


def _cattn(sink_all, qkv_c):
    b, c, _ = qkv_c.shape
    gw = GQA_GROUP * HEAD_DIM
    n_grp = A_KV_HEADS + B_KV_HEADS
    q_map = lambda bi, g: (bi, 0, g + g // A_KV_HEADS)
    k_map = lambda bi, g: (bi, 0, 8 + g + 10 * (g // A_KV_HEADS))
    v_map = lambda bi, g: (bi, 0, 10 + g + 10 * (g // A_KV_HEADS))
    return pl.pallas_call(
        _cattn_kernel,
        out_shape=jax.ShapeDtypeStruct((b, c, N_HEADS * HEAD_DIM), BF16),
        grid=(b, n_grp),
        in_specs=[
            pl.BlockSpec(memory_space=pltpu.SMEM),
            pl.BlockSpec((None, c, gw), q_map),
            pl.BlockSpec((None, c, HEAD_DIM), k_map),
            pl.BlockSpec((None, c, HEAD_DIM), v_map),
        ],
        out_specs=pl.BlockSpec((None, c, gw), lambda bi, g: (bi, 0, g)),
        compiler_params=_params("arbitrary", "arbitrary"),
        name="cattn",
    )(sink_all, qkv_c, qkv_c, qkv_c)


NATTN_ROWS = 4
NATTN_TQ = NATTN_ROWS * GRID_W
NATTN_KROWS = 3 * NATTN_ROWS


NATTN_HG = 4
N_RI = 2 * NA_KH - 1
N_CI = 2 * NA_KW - 1


def _nattn_kernel(q_ref, kp_ref, km_ref, kn_ref, vp_ref, vm_ref, vn_ref, kc_ref, vc_ref,
                  bias_ref, o_ref):
    for g in range(NATTN_HG):
        sl = slice(g * HEAD_DIM, (g + 1) * HEAD_DIM)
        q = q_ref[:, sl]
        k = jnp.concatenate([kp_ref[:, sl], km_ref[:, sl], kn_ref[:, sl]], axis=0)
        v = jnp.concatenate([vp_ref[:, sl], vm_ref[:, sl], vn_ref[:, sl]], axis=0)
        s_nb = _qk(q, k) + bias_ref[g]
        o = _softmax_pv([(_qk(q, kc_ref[:, sl]), vc_ref[:, sl]), (s_nb, v)])
        o_ref[:, sl] = o.astype(o_ref.dtype)


def _na_bias_kernel(rpb_ref, o_ref, u_ref, *, rows):
    base = pl.program_id(0) * (N_RI * N_CI)
    kh = min(NA_KH, rows)

    def fill(sg, carry):
        q0 = pl.multiple_of(sg * SUBLANES, SUBLANES)
        qc = q0 + lax.broadcasted_iota(jnp.int32, (SUBLANES, LANES), 0)
        kc = lax.broadcasted_iota(jnp.int32, (SUBLANES, LANES), 1) & (GRID_W - 1)
        diag = kc - qc + (NA_KW - 1)
        us = [jnp.zeros((SUBLANES, LANES), F32) for _ in range(N_RI)]
        for ci in range(N_CI):
            hit = diag == ci
            for a in range(N_RI):
                us[a] = jnp.where(hit, rpb_ref[base + a * N_CI + ci], us[a])
        for a in range(N_RI):
            u_ref[a, pl.ds(q0, SUBLANES), :] = us[a] * INV_ATTN_SCALE
        return carry

    lax.fori_loop(0, GRID_W // SUBLANES, fill, 0)

    qc = lax.broadcasted_iota(jnp.int32, (GRID_W, LANES), 0)
    lane = lax.broadcasted_iota(jnp.int32, (GRID_W, LANES), 1)
    kc = lane & (GRID_W - 1)
    cs = jnp.clip(qc - NA_KW // 2, 0, GRID_W - NA_KW)
    col_valid = (kc >= cs) & (kc < cs + NA_KW)
    low = lane < GRID_W
    masks = {(True, True): col_valid, (True, False): col_valid & low,
             (False, True): col_valid & jnp.logical_not(low)}
    neg_blk = jnp.full((GRID_W, LANES), NEG, F32)
    for var, r0 in enumerate((0, NATTN_ROWS, rows - NATTN_ROWS)):
        for qr in range(NATTN_ROWS):
            rs = min(max(r0 + qr - kh // 2, 0), rows - kh)
            for jj in range(NATTN_KROWS // 2):
                kr = r0 - NATTN_ROWS + 2 * jj
                ok = (rs <= kr < rs + kh, rs <= kr + 1 < rs + kh)
                if ok == (False, False):
                    blk = neg_blk
                else:
                    a = kr - (r0 + qr) + NA_KH - 1
                    a_lo = min(max(a, 0), N_RI - 1)
                    a_hi = min(max(a + 1, 0), N_RI - 1)
                    pair = jnp.where(low, u_ref[a_lo], u_ref[a_hi])
                    blk = jnp.where(masks[ok], pair, NEG)
                o_ref[var, qr * GRID_W:(qr + 1) * GRID_W, jj * LANES:(jj + 1) * LANES] = blk


def _na_bias_tables(rpb, rows):
    h = rpb.shape[0]
    assert 2 * GRID_W == LANES and rows >= 3 * NATTN_ROWS and NATTN_ROWS == NA_KH // 2
    return pl.pallas_call(
        functools.partial(_na_bias_kernel, rows=rows),
        out_shape=jax.ShapeDtypeStruct((h, 3, NATTN_TQ, NATTN_KROWS * GRID_W), F32),
        grid=(h,),
        in_specs=[pl.BlockSpec(memory_space=pltpu.SMEM)],
        out_specs=pl.BlockSpec((None, 3, NATTN_TQ, NATTN_KROWS * GRID_W), lambda hh: (hh, 0, 0, 0)),
        scratch_shapes=[pltpu.VMEM((N_RI, GRID_W, LANES), F32)],
        compiler_params=_params("arbitrary"),
        name="na_bias",
    )(rpb.reshape(-1))


def _nattn(qkv, kv_c, bias):
    b, s, _ = qkv.shape
    c = kv_c.shape[1]
    nt = s // NATTN_TQ
    h = C_HEADS
    ng = h // NATTN_HG
    gw = NATTN_HG * HEAD_DIM

    def prev_map(col):
        return lambda bi, hg, i: (bi, jnp.maximum(i - 1, 0), col + hg)

    def main_map(col):
        return lambda bi, hg, i: (bi, i, col + hg)

    def next_map(col):
        return lambda bi, hg, i: (bi, jnp.minimum(i + 1, nt - 1), col + hg)

    def bias_map(bi, hg, i):
        return (hg, jnp.where(i == 0, 0, jnp.where(i == nt - 1, 2, 1)), 0, 0)

    blk = (None, NATTN_TQ, gw)
    return pl.pallas_call(
        _nattn_kernel,
        out_shape=jax.ShapeDtypeStruct((b, s, h * HEAD_DIM), BF16),
        grid=(b, ng, nt),
        in_specs=[
            pl.BlockSpec(blk, main_map(0)),
            pl.BlockSpec(blk, prev_map(ng)), pl.BlockSpec(blk, main_map(ng)), pl.BlockSpec(blk, next_map(ng)),
            pl.BlockSpec(blk, prev_map(2 * ng)), pl.BlockSpec(blk, main_map(2 * ng)),
            pl.BlockSpec(blk, next_map(2 * ng)),
            pl.BlockSpec((None, c, gw), lambda bi, hg, i: (bi, 0, hg)),
            pl.BlockSpec((None, c, gw), lambda bi, hg, i: (bi, 0, ng + hg)),
            pl.BlockSpec((NATTN_HG, None, NATTN_TQ, NATTN_KROWS * GRID_W), bias_map),
        ],
        out_specs=pl.BlockSpec(blk, main_map(0)),
        compiler_params=_params("arbitrary", "arbitrary", "arbitrary"),
        name="nattn",
    )(qkv, qkv, qkv, qkv, qkv, qkv, qkv, kv_c, kv_c, bias)


def _outproj_kernel(*refs, widths):
    x_ref, w_ref, g_ref = refs[:3]
    o_refs = refs[3:3 + len(widths)]
    y_ref = refs[3 + len(widths)]
    y = None
    k0 = 0
    for o_ref, kw in zip(o_refs, widths):
        part = jnp.dot(o_ref[...], w_ref[k0:k0 + kw, :], preferred_element_type=F32)
        y = part if y is None else y + part
        k0 += kw
    y_ref[...] = x_ref[...] + g_ref[...] * y


def _outproj(x, o_parts, w, gate, *, tm):
    b, s, d = x.shape
    widths = tuple(o.shape[2] for o in o_parts)
    assert sum(widths) == w.shape[0]
    return pl.pallas_call(
        functools.partial(_outproj_kernel, widths=widths),
        out_shape=jax.ShapeDtypeStruct(x.shape, x.dtype),
        grid=(b, s // tm),
        in_specs=[
            pl.BlockSpec((None, tm, d), lambda bi, i: (bi, i, 0)),
            pl.BlockSpec(w.shape, lambda bi, i: (0, 0)),
            pl.BlockSpec((None, 1, d), lambda bi, i: (bi, 0, 0)),
        ] + [pl.BlockSpec((None, tm, kw), lambda bi, i: (bi, i, 0)) for kw in widths],
        out_specs=pl.BlockSpec((None, tm, d), lambda bi, i: (bi, i, 0)),
        compiler_params=_params("arbitrary", "arbitrary"),
        name="outproj",
    )(x, w, gate, *o_parts)


def _mlp_kernel(*refs, final_norm):
    x_ref, nw_ref, sh_ref, sc_ref, g_ref, w1_ref, w2_ref = refs[:7]
    rest = refs[7:]
    if final_norm:
        fw_ref = rest[0]
        rest = rest[1:]
    y_ref, h_ref, acc_ref = rest
    f = pl.program_id(2)

    @pl.when(f == 0)
    def _():
        h = _norm_modulate(x_ref[...], nw_ref[...], sh_ref[...], sc_ref[...])
        h_ref[...] = h.astype(BF16)
        acc_ref[...] = jnp.zeros_like(acc_ref)

    a = jnp.dot(h_ref[...], w1_ref[...], preferred_element_type=F32)
    a = jnp.square(jnp.maximum(a, 0.0)).astype(BF16)
    acc_ref[...] += jnp.dot(a, w2_ref[...], preferred_element_type=F32)

    @pl.when(f == pl.num_programs(2) - 1)
    def _():
        y = x_ref[...] + g_ref[...] * acc_ref[...]
        if final_norm:
            y = y * lax.rsqrt(jnp.mean(y * y, axis=-1, keepdims=True) + NORM_EPS) * fw_ref[...]
        y_ref[...] = y


def _mlp(x, nw, shift, scale, gate, w1, w2, *, tm, tf, final_w=None):
    b, s, d = x.shape
    dff = w1.shape[1]
    vec = pl.BlockSpec((None, 1, d), lambda bi, i, f: (bi, 0, 0))
    in_specs = [
        pl.BlockSpec((None, tm, d), lambda bi, i, f: (bi, i, 0)),
        pl.BlockSpec((1, d), lambda bi, i, f: (0, 0)),
        vec, vec, vec,
        pl.BlockSpec((d, tf), lambda bi, i, f: (0, f)),
        pl.BlockSpec((tf, d), lambda bi, i, f: (f, 0)),
    ]
    args = [x, nw.reshape(1, d), shift, scale, gate, w1, w2]
    if final_w is not None:
        in_specs.append(pl.BlockSpec((1, d), lambda bi, i, f: (0, 0)))
        args.append(final_w.reshape(1, d))
    return pl.pallas_call(
        functools.partial(_mlp_kernel, final_norm=final_w is not None),
        out_shape=jax.ShapeDtypeStruct(x.shape, x.dtype),
        grid=(b, s // tm, dff // tf),
        in_specs=in_specs,
        out_specs=pl.BlockSpec((None, tm, d), lambda bi, i, f: (bi, i, 0)),
        scratch_shapes=[pltpu.VMEM((tm, d), BF16), pltpu.VMEM((tm, d), F32)],
        compiler_params=_params("arbitrary", "arbitrary", "arbitrary"),
        name="mlp",
    )(*args)


EVEN_SECTIONS = (
    (0, A_Q_HEADS, "q_norm"),
    (A_Q_HEADS * HEAD_DIM, A_KV_HEADS, "k_norm"),
    ((A_Q_HEADS + A_KV_HEADS) * HEAD_DIM, A_KV_HEADS, "v"),
    ((A_Q_HEADS + 2 * A_KV_HEADS) * HEAD_DIM, B_Q_HEADS, "q"),
    ((A_Q_HEADS + 2 * A_KV_HEADS + B_Q_HEADS) * HEAD_DIM, B_KV_HEADS, "k"),
    ((A_Q_HEADS + 2 * A_KV_HEADS + B_Q_HEADS + B_KV_HEADS) * HEAD_DIM, B_KV_HEADS, "v"),
)
LAT_TM = 512


def kernel(x, c, ctx, c_ctx, ada_w, ada_b, norm_w, mlp_w1, mlp_w2, ev_w_in, ev_w_out, ev_q_norm,
           ev_k_norm, ev_sink, od_w_in, od_w_out, od_rpb, final_norm_w):
    b, s, d = x.shape
    n_ctx = ctx.shape[1]
    depth = ada_w.shape[0]
    rows = s // GRID_W

    mod = _ada_mod(jnp.concatenate([c, c_ctx[None, :]], axis=0), ada_w, ada_b)
    rope_tabs = _rope_tables(s)

    def mod_vecs(layer):
        lat = [mod[layer, :b, k * d:(k + 1) * d][:, None, :] for k in range(6)]
        cx = [jnp.broadcast_to(mod[layer, b:b + 1, k * d:(k + 1) * d][:, None, :], (b, 1, d))
              for k in range(6)]
        return lat, cx

    for layer in range(depth):
        need_ctx = layer < depth - 1
        (sh1, sc1, g1, sh2, sc2, g2), (sh1c, sc1c, g1c, sh2c, sc2c, g2c) = mod_vecs(layer)
        j = layer // 2
        w1 = mlp_w1[layer].astype(BF16)
        w2 = mlp_w2[layer].astype(BF16)
        if layer % 2 == 0:
            w_in = ev_w_in[j].astype(BF16)
            w_out = ev_w_out[j].astype(BF16)
            n_in = w_in.shape[1]
            norms = (ev_q_norm[j], ev_k_norm[j])
            qkv = _proj(x, norm_w[layer, 0], sh1, sc1, w_in, tm=LAT_TM, tn=n_in,
                        sections=EVEN_SECTIONS, head_norms=norms, rope_tabs=rope_tabs, name="proj_even")
            qkv_c = _proj(ctx, norm_w[layer, 0], sh1c, sc1c, w_in, tm=n_ctx, tn=n_in,
                          sections=EVEN_SECTIONS, head_norms=norms, name="proj_even_ctx")
            oa = _gattn(qkv, qkv_c, q_blk=0, k_blk=A_Q_HEADS, v_blk=A_Q_HEADS + A_KV_HEADS)
            sink_kg = ev_sink[j].reshape(B_KV_HEADS, GQA_GROUP).astype(F32)
            qb_blk = (A_Q_HEADS + 2 * A_KV_HEADS) // GQA_GROUP
            kb_blk = A_Q_HEADS + 2 * A_KV_HEADS + B_Q_HEADS
            ob = _wattn(sink_kg, qkv, qkv_c, q_blk=qb_blk, k_blk=kb_blk, v_blk=kb_blk + B_KV_HEADS)
            o_parts = [oa, ob]
            if need_ctx:
                sink_all = jnp.concatenate([jnp.full((A_KV_HEADS, GQA_GROUP), NEG, F32), sink_kg], axis=0)
                oc_parts = [_cattn(sink_all, qkv_c)]
        else:
            w_in = od_w_in[j].astype(BF16)
            w_out = od_w_out[j].astype(BF16)
            hd = C_HEADS * HEAD_DIM
            qkv = _proj(x, norm_w[layer, 0], sh1, sc1, w_in, tm=LAT_TM, tn=hd, name="proj_odd")
            kv_c = _proj(ctx, norm_w[layer, 0], sh1c, sc1c, w_in, tm=n_ctx, tn=hd, n_off=1,
                         n_out=2 * hd, name="proj_odd_ctx")
            o_parts = [_nattn(qkv, kv_c, _na_bias_tables(od_rpb[j], rows))]
            assert not need_ctx, "the odd layer is the last one at this depth"
        x = _outproj(x, o_parts, w_out, g1, tm=LAT_TM)
        x = _mlp(x, norm_w[layer, 1], sh2, sc2, g2, w1, w2, tm=LAT_TM, tf=1024,
                 final_w=None if need_ctx else final_norm_w)
        if need_ctx:
            ctx = _outproj(ctx, oc_parts, w_out, g1c, tm=n_ctx)
            ctx = _mlp(ctx, norm_w[layer, 1], sh2c, sc2c, g2c, w1, w2, tm=n_ctx, tf=1024)
    return x
```

```python
import functools

import jax
import jax.numpy as jnp
import numpy as np
from jax import lax
from jax.experimental import pallas as pl
from jax.experimental.pallas import tpu as pltpu

D_MODEL = 2048
GRID_W = 64
HEAD_DIM = 128
N_HEADS = D_MODEL // HEAD_DIM
A_Q_HEADS = N_HEADS // 2
A_KV_HEADS = max(1, A_Q_HEADS // 4)
B_Q_HEADS = N_HEADS - A_Q_HEADS
B_KV_HEADS = max(1, B_Q_HEADS // 4)
GQA_GROUP = A_Q_HEADS // A_KV_HEADS
C_HEADS = N_HEADS
WINDOW = 128
NA_KH = 8
NA_KW = 16
D_FF = 4 * D_MODEL
ROPE_THETA = 10000.0
ROPE_PAIRS = HEAD_DIM // 4
NORM_EPS = 1e-6
NEG = -1e30
ATTN_SCALE = HEAD_DIM ** -0.5

V7X_VMEM_BYTES = 64 * 1024 * 1024
VMEM_LIMIT = V7X_VMEM_BYTES - 8 * 1024 * 1024
LANES = 128
SUBLANES = 8

F32 = jnp.float32
BF16 = jnp.bfloat16


def _params(*sem):
    return pltpu.CompilerParams(dimension_semantics=sem, vmem_limit_bytes=VMEM_LIMIT)


ADA_ROWS = 3
ADA_TN = 1024


def _ada_kernel(cb_ref, w_ref, b_ref, o_ref, s_ref):
    @pl.when((pl.program_id(0) == 0) & (pl.program_id(1) == 0))
    def _():
        cvals = cb_ref[...]
        s_ref[...] = cvals * (1.0 / (1.0 + jnp.exp(-cvals)))

    nchunk = ADA_TN // LANES

    def body(kk, accs):
        k0 = pl.multiple_of(kk * SUBLANES, SUBLANES)
        w = w_ref[pl.ds(k0, SUBLANES), :]
        out = []
        for r in range(ADA_ROWS):
            s = s_ref[r, pl.ds(k0, SUBLANES), :]
            for cch in range(nchunk):
                out.append(accs[r * nchunk + cch] + w[:, cch * LANES:(cch + 1) * LANES] * s)
        return tuple(out)

    zeros = tuple(jnp.zeros((SUBLANES, LANES), F32) for _ in range(ADA_ROWS * nchunk))
    accs = lax.fori_loop(0, D_MODEL // SUBLANES, body, zeros, unroll=2)
    o_ref[...] = jnp.broadcast_to(b_ref[...], o_ref.shape)
    for r in range(ADA_ROWS):
        row = jnp.concatenate(
            [jnp.sum(accs[r * nchunk + cch], axis=0, keepdims=True) for cch in range(nchunk)], axis=1)
        o_ref[r:r + 1, :] = row + b_ref[...]


def _ada_mod(cc, ada_w, ada_b):
    depth, d, n = ada_w.shape
    cb = jnp.broadcast_to(cc[:, :, None], (ADA_ROWS, d, LANES))
    return pl.pallas_call(
        _ada_kernel,
        out_shape=jax.ShapeDtypeStruct((depth, SUBLANES, n), F32),
        grid=(depth, n // ADA_TN),
        in_specs=[
            pl.BlockSpec((ADA_ROWS, d, LANES), lambda l, j: (0, 0, 0)),
            pl.BlockSpec((None, d, ADA_TN), lambda l, j: (l, 0, j)),
            pl.BlockSpec((None, 1, ADA_TN), lambda l, j: (l, 0, j)),
        ],
        out_specs=pl.BlockSpec((None, SUBLANES, ADA_TN), lambda l, j: (l, 0, j)),
        scratch_shapes=[pltpu.VMEM((ADA_ROWS, d, LANES), F32)],
        compiler_params=_params("arbitrary", "arbitrary"),
        name="ada_mod",
    )(cb, ada_w, ada_b.reshape(depth, 1, n))


def _norm_modulate(x, nw, shift, scale):
    ms = jnp.mean(x * x, axis=-1, keepdims=True)
    y = x * lax.rsqrt(ms + NORM_EPS) * nw
    return y * (1.0 + scale) + shift


def _rope(y, cos, sin_lo, sin_hi):
    return (y * cos + pltpu.roll(y, HEAD_DIM - ROPE_PAIRS, 1) * sin_lo
            + pltpu.roll(y, ROPE_PAIRS, 1) * sin_hi)


def _rope_tables(seq):
    t = jnp.arange(seq)
    row = (t // GRID_W).astype(F32)
    col = (t % GRID_W).astype(F32)
    inv = ROPE_THETA ** (-jnp.arange(ROPE_PAIRS, dtype=F32) / ROPE_PAIRS)
    ang_r = row[:, None] * inv
    ang_c = col[:, None] * inv
    ang = jnp.concatenate([ang_r, ang_r, ang_c, ang_c], axis=-1)
    cos, sin = jnp.cos(ang), jnp.sin(ang)
    first = (jnp.arange(HEAD_DIM) % (2 * ROPE_PAIRS)) < ROPE_PAIRS
    sin_lo = jnp.where(first, -sin, 0.0)
    sin_hi = jnp.where(first, 0.0, sin)
    return cos, sin_lo, sin_hi


def _proj_kernel(*refs, sections, rope):
    x_ref, nw_ref, sh_ref, sc_ref, w_ref = refs[:5]
    rest = refs[5:]
    if sections is not None:
        qn_ref, kn_ref = rest[:2]
        rest = rest[2:]
    if rope:
        cos_ref, slo_ref, shi_ref = rest[:3]
        rest = rest[3:]
    o_ref, h_ref = rest

    @pl.when(pl.program_id(2) == 0)
    def _():
        h = _norm_modulate(x_ref[...], nw_ref[...], sh_ref[...], sc_ref[...])
        h_ref[...] = h.astype(BF16)

    if sections is None:
        o_ref[...] = jnp.dot(h_ref[...], w_ref[...], preferred_element_type=F32).astype(o_ref.dtype)
        return

    for c0, nh, kind in sections:
        y = jnp.dot(h_ref[...], w_ref[:, c0:c0 + nh * HEAD_DIM], preferred_element_type=F32)
        for hh in range(nh):
            yh = y[:, hh * HEAD_DIM:(hh + 1) * HEAD_DIM]
            if kind in ("q_norm", "k_norm"):
                hw = (qn_ref if kind == "q_norm" else kn_ref)[...]
                yh = yh * lax.rsqrt(jnp.mean(yh * yh, axis=-1, keepdims=True) + NORM_EPS) * hw
            if rope and kind != "v":
                yh = _rope(yh, cos_ref[...], slo_ref[...], shi_ref[...])
            lo = c0 + hh * HEAD_DIM
            o_ref[:, lo:lo + HEAD_DIM] = yh.astype(o_ref.dtype)


def _proj(x, nw, shift, scale, w, *, tm, tn, n_off=0, n_out=None, sections=None,
          head_norms=None, rope_tabs=None, name="proj"):
    b, s, d = x.shape
    n_out = w.shape[1] if n_out is None else n_out
    nj = n_out // tn
    in_specs = [
        pl.BlockSpec((None, tm, d), lambda bi, i, j: (bi, i, 0)),
        pl.BlockSpec((1, d), lambda bi, i, j: (0, 0)),
        pl.BlockSpec((None, 1, d), lambda bi, i, j: (bi, 0, 0)),
        pl.BlockSpec((None, 1, d), lambda bi, i, j: (bi, 0, 0)),
        pl.BlockSpec((d, tn), lambda bi, i, j: (0, j + n_off)),
    ]
    args = [x, nw.reshape(1, d), shift, scale, w]
    if sections is not None:
        assert nj == 1
        in_specs += [pl.BlockSpec((1, HEAD_DIM), lambda bi, i, j: (0, 0))] * 2
        args += [head_norms[0].reshape(1, HEAD_DIM), head_norms[1].reshape(1, HEAD_DIM)]
    if rope_tabs is not None:
        in_specs += [pl.BlockSpec((tm, HEAD_DIM), lambda bi, i, j: (i, 0))] * 3
        args += list(rope_tabs)
    return pl.pallas_call(
        functools.partial(_proj_kernel, sections=sections, rope=rope_tabs is not None),
        out_shape=jax.ShapeDtypeStruct((b, s, n_out), BF16),
        grid=(b, s // tm, nj),
        in_specs=in_specs,
        out_specs=pl.BlockSpec((None, tm, tn), lambda bi, i, j: (bi, i, j)),
        scratch_shapes=[pltpu.VMEM((tm, d), BF16)],
        compiler_params=_params("arbitrary", "arbitrary", "arbitrary"),
        name=name,
    )(*args)


def _qk(q, k):
    return lax.dot_general(q, k, (((1,), (1,)), ((), ())), preferred_element_type=F32)


EXP2_SCALE = ATTN_SCALE * float(np.log2(np.e))
INV_ATTN_SCALE = 1.0 / ATTN_SCALE


def _softmax_pv(parts, sink=None):
    m = functools.reduce(jnp.maximum, [jnp.max(s, axis=-1, keepdims=True) for s, _ in parts])
    if sink is not None:
        m = jnp.maximum(m, sink)
    l = None
    o = None
    for s, v in parts:
        p = jnp.exp2((s - m) * EXP2_SCALE)
        ls = jnp.sum(p, axis=-1, keepdims=True)
        os_ = jnp.dot(p.astype(BF16), v, preferred_element_type=F32)
        l = ls if l is None else l + ls
        o = os_ if o is None else o + os_
    if sink is not None:
        l = l + jnp.exp2((sink - m) * EXP2_SCALE)
    return o / l


GATTN_ROWS = 256


def _gattn_kernel(q_ref, k_ref, v_ref, kc_ref, vc_ref, o_ref, kall_ref, vall_ref, s_ref, p_ref):
    c = kc_ref.shape[0]

    n_blk = q_ref.shape[0] // GATTN_ROWS

    kall_ref[:c, :] = kc_ref[...]
    kall_ref[c:, :] = k_ref[...]
    vall_ref[:c, :HEAD_DIM] = vc_ref[...]
    vall_ref[c:, :HEAD_DIM] = v_ref[...]
    lane = lax.broadcasted_iota(jnp.int32, (vall_ref.shape[0], HEAD_DIM), 1)
    vall_ref[:, HEAD_DIM:] = jnp.where(lane == 0, 1.0, 0.0).astype(BF16)

    def rows_of(blk):
        return pl.ds(pl.multiple_of(blk * GATTN_ROWS, GATTN_ROWS), GATTN_ROWS)

    def cols_of(g):
        return slice(g * HEAD_DIM, (g + 1) * HEAD_DIM)

    def scores(blk, g, slot):
        s_ref[slot] = _qk(q_ref[rows_of(blk), cols_of(g)], kall_ref[...])

    def probs(slot):
        m = jnp.max(s_ref[slot], axis=-1, keepdims=True)
        p_ref[slot] = jnp.exp2((s_ref[slot] - m) * EXP2_SCALE).astype(BF16)

    def weighted_values(blk, g, slot):
        ol = jnp.dot(p_ref[slot], vall_ref[...], preferred_element_type=F32)
        o = ol[:, :HEAD_DIM] / ol[:, HEAD_DIM:HEAD_DIM + 1]
        o_ref[rows_of(blk), cols_of(g)] = o.astype(o_ref.dtype)

    last = GQA_GROUP - 1
    p_ref[(last) % 2] = jnp.ones(p_ref.shape[1:], BF16)
    scores(0, 0, 0)

    def body(blk, carry):
        for g in range(GQA_GROUP):
            if g < last:
                scores(blk, g + 1, (g + 1) % 2)
            else:
                scores(jnp.minimum(blk + 1, n_blk - 1), 0, (g + 1) % 2)
            probs(g % 2)
            if g > 0:
                weighted_values(blk, g - 1, (g - 1) % 2)
            else:
                weighted_values(jnp.maximum(blk - 1, 0), last, last % 2)
        return carry

    lax.fori_loop(0, n_blk, body, 0)
    weighted_values(n_blk - 1, last, last % 2)


def _gattn(qkv, qkv_c, *, q_blk, k_blk, v_blk):
    b, s, _ = qkv.shape
    c = qkv_c.shape[1]
    gw = GQA_GROUP * HEAD_DIM
    assert GQA_GROUP % 2 == 0 and s % GATTN_ROWS == 0
    return pl.pallas_call(
        _gattn_kernel,
        out_shape=jax.ShapeDtypeStruct((b, s, A_Q_HEADS * HEAD_DIM), BF16),
        grid=(b, A_KV_HEADS),
        in_specs=[
            pl.BlockSpec((None, s, gw), lambda bi, kv: (bi, 0, q_blk + kv)),
            pl.BlockSpec((None, s, HEAD_DIM), lambda bi, kv: (bi, 0, k_blk + kv)),
            pl.BlockSpec((None, s, HEAD_DIM), lambda bi, kv: (bi, 0, v_blk + kv)),
            pl.BlockSpec((None, c, HEAD_DIM), lambda bi, kv: (bi, 0, k_blk + kv)),
            pl.BlockSpec((None, c, HEAD_DIM), lambda bi, kv: (bi, 0, v_blk + kv)),
        ],
        out_specs=pl.BlockSpec((None, s, gw), lambda bi, kv: (bi, 0, kv)),
        scratch_shapes=[
            pltpu.VMEM((c + s, HEAD_DIM), BF16),
            pltpu.VMEM((c + s, 2 * HEAD_DIM), BF16),
            pltpu.VMEM((2, GATTN_ROWS, c + s), F32),
            pltpu.VMEM((2, GATTN_ROWS, c + s), BF16),
        ],
        compiler_params=_params("arbitrary", "arbitrary"),
        name="gattn",
    )(qkv, qkv, qkv, qkv_c, qkv_c)


WATTN_TQ = 4 * WINDOW


def _wattn_kernel(sink_ref, q_ref, kp_ref, km_ref, kn_ref, vp_ref, vm_ref, vn_ref,
                  kc_ref, vc_ref, o_ref, *, seq):
    kv = pl.program_id(1)
    i = pl.program_id(2)
    k = jnp.concatenate([kp_ref[...], km_ref[...], kn_ref[...]], axis=0)
    v = jnp.concatenate([vp_ref[...], vm_ref[...], vn_ref[...]], axis=0)
    nk = WATTN_TQ + 2 * WINDOW
    kpos = i * WATTN_TQ - WINDOW + lax.broadcasted_iota(jnp.int32, (1, nk), 1)
    kpos = jnp.where((kpos >= 0) & (kpos < seq), kpos, -(4 * seq))
    qpos = i * WATTN_TQ + lax.broadcasted_iota(jnp.int32, (WATTN_TQ, 1), 0)
    valid = jnp.abs(kpos - qpos) <= WINDOW
    kc, vc = kc_ref[...], vc_ref[...]
    for g in range(GQA_GROUP):
        q = q_ref[:, g * HEAD_DIM:(g + 1) * HEAD_DIM]
        s_win = jnp.where(valid, _qk(q, k), NEG)
        o = _softmax_pv([(_qk(q, kc), vc), (s_win, v)], sink=sink_ref[kv, g] * INV_ATTN_SCALE)
        o_ref[:, g * HEAD_DIM:(g + 1) * HEAD_DIM] = o.astype(o_ref.dtype)


def _wattn(sink_kg, qkv, qkv_c, *, q_blk, k_blk, v_blk):
    b, s, _ = qkv.shape
    c = qkv_c.shape[1]
    gw = GQA_GROUP * HEAD_DIM
    r = WATTN_TQ // WINDOW
    last = s // WINDOW - 1

    def prev_map(col):
        return lambda bi, kv, i: (bi, jnp.maximum(i * r - 1, 0), col + kv)

    def main_map(col):
        return lambda bi, kv, i: (bi, i, col + kv)

    def next_map(col):
        return lambda bi, kv, i: (bi, jnp.minimum(i * r + r, last), col + kv)

    return pl.pallas_call(
        functools.partial(_wattn_kernel, seq=s),
        out_shape=jax.ShapeDtypeStruct((b, s, B_Q_HEADS * HEAD_DIM), BF16),
        grid=(b, B_KV_HEADS, s // WATTN_TQ),
        in_specs=[
            pl.BlockSpec(memory_space=pltpu.SMEM),
            pl.BlockSpec((None, WATTN_TQ, gw), lambda bi, kv, i: (bi, i, q_blk + kv)),
            pl.BlockSpec((None, WINDOW, HEAD_DIM), prev_map(k_blk)),
            pl.BlockSpec((None, WATTN_TQ, HEAD_DIM), main_map(k_blk)),
            pl.BlockSpec((None, WINDOW, HEAD_DIM), next_map(k_blk)),
            pl.BlockSpec((None, WINDOW, HEAD_DIM), prev_map(v_blk)),
            pl.BlockSpec((None, WATTN_TQ, HEAD_DIM), main_map(v_blk)),
            pl.BlockSpec((None, WINDOW, HEAD_DIM), next_map(v_blk)),
            pl.BlockSpec((None, c, HEAD_DIM), lambda bi, kv, i: (bi, 0, k_blk + kv)),
            pl.BlockSpec((None, c, HEAD_DIM), lambda bi, kv, i: (bi, 0, v_blk + kv)),
        ],
        out_specs=pl.BlockSpec((None, WATTN_TQ, gw), lambda bi, kv, i: (bi, i, kv)),
        compiler_params=_params("arbitrary", "arbitrary", "arbitrary"),
        name="wattn",
    )(sink_kg, qkv, qkv, qkv, qkv, qkv, qkv, qkv, qkv_c, qkv_c)


def _cattn_kernel(sink_ref, q_ref, k_ref, v_ref, o_ref):
    grp = pl.program_id(1)
    k, v = k_ref[...], v_ref[...]
    for g in range(GQA_GROUP):
        q = q_ref[:, g * HEAD_DIM:(g + 1) * HEAD_DIM]
        o = _softmax_pv([(_qk(q, k), v)], sink=sink_ref[grp, g] * INV_ATTN_SCALE)
        o_ref[:, g * HEAD_DIM:(g + 1) * HEAD_DIM] = o.astype(o_ref.dtype)


def _cattn(sink_all, qkv_c):
    b, c, _ = qkv_c.shape
    gw = GQA_GROUP * HEAD_DIM
    n_grp = A_KV_HEADS + B_KV_HEADS
    q_map = lambda bi, g: (bi, 0, g + g // A_KV_HEADS)
    k_map = lambda bi, g: (bi, 0, 8 + g + 10 * (g // A_KV_HEADS))
    v_map = lambda bi, g: (bi, 0, 10 + g + 10 * (g // A_KV_HEADS))
    return pl.pallas_call(
        _cattn_kernel,
        out_shape=jax.ShapeDtypeStruct((b, c, N_HEADS * HEAD_DIM), BF16),
        grid=(b, n_grp),
        in_specs=[
            pl.BlockSpec(memory_space=pltpu.SMEM),
            pl.BlockSpec((None, c, gw), q_map),
            pl.BlockSpec((None, c, HEAD_DIM), k_map),
            pl.BlockSpec((None, c, HEAD_DIM), v_map),
        ],
        out_specs=pl.BlockSpec((None, c, gw), lambda bi, g: (bi, 0, g)),
        compiler_params=_params("arbitrary", "arbitrary"),
        name="cattn",
    )(sink_all, qkv_c, qkv_c, qkv_c)


NATTN_ROWS = 4
NATTN_TQ = NATTN_ROWS * GRID_W
NATTN_KROWS = 3 * NATTN_ROWS


NATTN_HG = 4
N_RI = 2 * NA_KH - 1
N_CI = 2 * NA_KW - 1


def _na_key_row0(tile, rows):
    lo, hi = 0, rows - NATTN_KROWS
    r0 = (tile - 1) * NATTN_ROWS
    if isinstance(tile, int):
        return min(max(r0, lo), hi)
    return jnp.clip(r0, lo, hi)


def _nattn_kernel(q_ref, k_ref, v_ref, kc_ref, vc_ref, bias_ref, o_ref,
                  s_ref, sc_ref, p_ref, pc_ref, l_ref):
    seq = q_ref.shape[0]
    n_tile = seq // NATTN_TQ
    nk = NATTN_KROWS * GRID_W

    def rows_of(tile):
        return pl.ds(pl.multiple_of(tile * NATTN_TQ, NATTN_TQ), NATTN_TQ)

    def key_rows_of(tile):
        return pl.ds(pl.multiple_of(_na_key_row0(tile, seq // GRID_W) * GRID_W, NATTN_TQ), nk)

    def cols_of(g):
        return slice(g * HEAD_DIM, (g + 1) * HEAD_DIM)

    def scores(tile, g, slot):
        q = q_ref[rows_of(tile), cols_of(g)]
        var = jnp.where(tile == 0, 0, jnp.where(tile == n_tile - 1, 2, 1))
        s_ref[slot] = _qk(q, k_ref[key_rows_of(tile), cols_of(g)]) + bias_ref[g, var]
        sc_ref[slot] = _qk(q, kc_ref[:, cols_of(g)])

    def probs(slot):
        s, sc = s_ref[slot], sc_ref[slot]
        m = jnp.maximum(jnp.max(s, axis=-1, keepdims=True), jnp.max(sc, axis=-1, keepdims=True))
        p = jnp.exp2((s - m) * EXP2_SCALE)
        pc = jnp.exp2((sc - m) * EXP2_SCALE)
        l_ref[slot] = jnp.sum(p, axis=-1, keepdims=True) + jnp.sum(pc, axis=-1, keepdims=True)
        p_ref[slot] = p.astype(BF16)
        pc_ref[slot] = pc.astype(BF16)

    def weighted_values(tile, g, slot):
        o = (jnp.dot(p_ref[slot], v_ref[key_rows_of(tile), cols_of(g)], preferred_element_type=F32)
             + jnp.dot(pc_ref[slot], vc_ref[:, cols_of(g)], preferred_element_type=F32))
        o_ref[rows_of(tile), cols_of(g)] = (o / l_ref[slot]).astype(o_ref.dtype)

    last = NATTN_HG - 1
    p_ref[last % 2] = jnp.ones(p_ref.shape[1:], BF16)
    pc_ref[last % 2] = jnp.ones(pc_ref.shape[1:], BF16)
    l_ref[last % 2] = jnp.ones(l_ref.shape[1:], F32)
    scores(jnp.int32(0), 0, 0)

    def body(tile, carry):
        for g in range(NATTN_HG):
            if g < last:
                scores(tile, g + 1, (g + 1) % 2)
            else:
                scores(jnp.minimum(tile + 1, n_tile - 1), 0, (g + 1) % 2)
            probs(g % 2)
            if g > 0:
                weighted_values(tile, g - 1, (g - 1) % 2)
            else:
                weighted_values(jnp.maximum(tile - 1, 0), last, last % 2)
        return carry

    lax.fori_loop(0, n_tile, body, 0)
    weighted_values(jnp.int32(n_tile - 1), last, last % 2)


def _na_bias_kernel(rpb_ref, o_ref, u_ref, *, rows):
    base = pl.program_id(0) * (N_RI * N_CI)
    kh = min(NA_KH, rows)

    def fill(sg, carry):
        q0 = pl.multiple_of(sg * SUBLANES, SUBLANES)
        qc = q0 + lax.broadcasted_iota(jnp.int32, (SUBLANES, LANES), 0)
        kc = lax.broadcasted_iota(jnp.int32, (SUBLANES, LANES), 1) & (GRID_W - 1)
        diag = kc - qc + (NA_KW - 1)
        us = [jnp.zeros((SUBLANES, LANES), F32) for _ in range(N_RI)]
        for ci in range(N_CI):
            hit = diag == ci
            for a in range(N_RI):
                us[a] = jnp.where(hit, rpb_ref[base + a * N_CI + ci], us[a])
        for a in range(N_RI):
            u_ref[a, pl.ds(q0, SUBLANES), :] = us[a] * INV_ATTN_SCALE
        return carry

    lax.fori_loop(0, GRID_W // SUBLANES, fill, 0)

    qc = lax.broadcasted_iota(jnp.int32, (GRID_W, LANES), 0)
    lane = lax.broadcasted_iota(jnp.int32, (GRID_W, LANES), 1)
    kc = lane & (GRID_W - 1)
    cs = jnp.clip(qc - NA_KW // 2, 0, GRID_W - NA_KW)
    col_valid = (kc >= cs) & (kc < cs + NA_KW)
    low = lane < GRID_W
    masks = {(True, True): col_valid, (True, False): col_valid & low,
             (False, True): col_valid & jnp.logical_not(low)}
    neg_blk = jnp.full((GRID_W, LANES), NEG, F32)
    n_tile = rows // NATTN_ROWS
    for var, tile in enumerate((0, 1, n_tile - 1)):
        r0 = tile * NATTN_ROWS
        k0 = _na_key_row0(tile, rows)
        for qr in range(NATTN_ROWS):
            rs = min(max(r0 + qr - kh // 2, 0), rows - kh)
            for jj in range(NATTN_KROWS // 2):
                kr = k0 + 2 * jj
                ok = (rs <= kr < rs + kh, rs <= kr + 1 < rs + kh)
                if ok == (False, False):
                    blk = neg_blk
                else:
                    a = kr - (r0 + qr) + NA_KH - 1
                    a_lo = min(max(a, 0), N_RI - 1)
                    a_hi = min(max(a + 1, 0), N_RI - 1)
                    pair = jnp.where(low, u_ref[a_lo], u_ref[a_hi])
                    blk = jnp.where(masks[ok], pair, NEG)
                o_ref[var, qr * GRID_W:(qr + 1) * GRID_W, jj * LANES:(jj + 1) * LANES] = blk


def _na_bias_tables(rpb, rows):
    h = rpb.shape[0]
    assert 2 * GRID_W == LANES and rows >= 3 * NATTN_ROWS and NATTN_ROWS == NA_KH // 2
    return pl.pallas_call(
        functools.partial(_na_bias_kernel, rows=rows),
        out_shape=jax.ShapeDtypeStruct((h, 3, NATTN_TQ, NATTN_KROWS * GRID_W), F32),
        grid=(h,),
        in_specs=[pl.BlockSpec(memory_space=pltpu.SMEM)],
        out_specs=pl.BlockSpec((None, 3, NATTN_TQ, NATTN_KROWS * GRID_W), lambda hh: (hh, 0, 0, 0)),
        scratch_shapes=[pltpu.VMEM((N_RI, GRID_W, LANES), F32)],
        compiler_params=_params("arbitrary"),
        name="na_bias",
    )(rpb.reshape(-1))


def _nattn(qkv, kv_c, bias):
    b, s, _ = qkv.shape
    c = kv_c.shape[1]
    h = C_HEADS
    ng = h // NATTN_HG
    gw = NATTN_HG * HEAD_DIM
    nk = NATTN_KROWS * GRID_W
    assert NATTN_HG % 2 == 0 and s % NATTN_TQ == 0

    def col_map(col):
        return lambda hg, bi: (bi, 0, col + hg)

    return pl.pallas_call(
        _nattn_kernel,
        out_shape=jax.ShapeDtypeStruct((b, s, h * HEAD_DIM), BF16),
        grid=(ng, b),
        in_specs=[
            pl.BlockSpec((None, s, gw), col_map(0)),
            pl.BlockSpec((None, s, gw), col_map(ng)),
            pl.BlockSpec((None, s, gw), col_map(2 * ng)),
            pl.BlockSpec((None, c, gw), col_map(0)),
            pl.BlockSpec((None, c, gw), col_map(ng)),
            pl.BlockSpec((NATTN_HG, 3, NATTN_TQ, nk), lambda hg, bi: (hg, 0, 0, 0),
                         pipeline_mode=pl.Buffered(1)),
        ],
        out_specs=pl.BlockSpec((None, s, gw), col_map(0)),
        scratch_shapes=[
            pltpu.VMEM((2, NATTN_TQ, nk), F32),
            pltpu.VMEM((2, NATTN_TQ, c), F32),
            pltpu.VMEM((2, NATTN_TQ, nk), BF16),
            pltpu.VMEM((2, NATTN_TQ, c), BF16),
            pltpu.VMEM((2, NATTN_TQ, 1), F32),
        ],
        compiler_params=_params("arbitrary", "arbitrary"),
        name="nattn",
    )(qkv, qkv, qkv, kv_c, kv_c, bias)


def _outproj_kernel(*refs, widths):
    x_ref, w_ref, g_ref = refs[:3]
    o_refs = refs[3:3 + len(widths)]
    y_ref = refs[3 + len(widths)]
    y = None
    k0 = 0
    for o_ref, kw in zip(o_refs, widths):
        part = jnp.dot(o_ref[...], w_ref[k0:k0 + kw, :], preferred_element_type=F32)
        y = part if y is None else y + part
        k0 += kw
    y_ref[...] = x_ref[...] + g_ref[...] * y


def _outproj(x, o_parts, w, gate, *, tm):
    b, s, d = x.shape
    widths = tuple(o.shape[2] for o in o_parts)
    assert sum(widths) == w.shape[0]
    return pl.pallas_call(
        functools.partial(_outproj_kernel, widths=widths),
        out_shape=jax.ShapeDtypeStruct(x.shape, x.dtype),
        grid=(b, s // tm),
        in_specs=[
            pl.BlockSpec((None, tm, d), lambda bi, i: (bi, i, 0)),
            pl.BlockSpec(w.shape, lambda bi, i: (0, 0)),
            pl.BlockSpec((None, 1, d), lambda bi, i: (bi, 0, 0)),
        ] + [pl.BlockSpec((None, tm, kw), lambda bi, i: (bi, i, 0)) for kw in widths],
        out_specs=pl.BlockSpec((None, tm, d), lambda bi, i: (bi, i, 0)),
        compiler_params=_params("arbitrary", "arbitrary"),
        name="outproj",
    )(x, w, gate, *o_parts)


def _mlp_kernel(*refs, final_norm):
    x_ref, nw_ref, sh_ref, sc_ref, g_ref, w1_ref, w2_ref = refs[:7]
    rest = refs[7:]
    if final_norm:
        fw_ref = rest[0]
        rest = rest[1:]
    y_ref, h_ref, acc_ref = rest
    f = pl.program_id(2)

    @pl.when(f == 0)
    def _():
        h = _norm_modulate(x_ref[...], nw_ref[...], sh_ref[...], sc_ref[...])
        h_ref[...] = h.astype(BF16)
        acc_ref[...] = jnp.zeros_like(acc_ref)

    a = jnp.dot(h_ref[...], w1_ref[...], preferred_element_type=F32)
    a = jnp.square(jnp.maximum(a, 0.0)).astype(BF16)
    acc_ref[...] += jnp.dot(a, w2_ref[...], preferred_element_type=F32)

    @pl.when(f == pl.num_programs(2) - 1)
    def _():
        y = x_ref[...] + g_ref[...] * acc_ref[...]
        if final_norm:
            y = y * lax.rsqrt(jnp.mean(y * y, axis=-1, keepdims=True) + NORM_EPS) * fw_ref[...]
        y_ref[...] = y


def _mlp(x, nw, shift, scale, gate, w1, w2, *, tm, tf, final_w=None):
    b, s, d = x.shape
    dff = w1.shape[1]
    vec = pl.BlockSpec((None, 1, d), lambda bi, i, f: (bi, 0, 0))
    in_specs = [
        pl.BlockSpec((None, tm, d), lambda bi, i, f: (bi, i, 0)),
        pl.BlockSpec((1, d), lambda bi, i, f: (0, 0)),
        vec, vec, vec,
        pl.BlockSpec((d, tf), lambda bi, i, f: (0, f)),
        pl.BlockSpec((tf, d), lambda bi, i, f: (f, 0)),
    ]
    args = [x, nw.reshape(1, d), shift, scale, gate, w1, w2]
    if final_w is not None:
        in_specs.append(pl.BlockSpec((1, d), lambda bi, i, f: (0, 0)))
        args.append(final_w.reshape(1, d))
    return pl.pallas_call(
        functools.partial(_mlp_kernel, final_norm=final_w is not None),
        out_shape=jax.ShapeDtypeStruct(x.shape, x.dtype),
        grid=(b, s // tm, dff // tf),
        in_specs=in_specs,
        out_specs=pl.BlockSpec((None, tm, d), lambda bi, i, f: (bi, i, 0)),
        scratch_shapes=[pltpu.VMEM((tm, d), BF16), pltpu.VMEM((tm, d), F32)],
        compiler_params=_params("arbitrary", "arbitrary", "arbitrary"),
        name="mlp",
    )(*args)


EVEN_SECTIONS = (
    (0, A_Q_HEADS, "q_norm"),
    (A_Q_HEADS * HEAD_DIM, A_KV_HEADS, "k_norm"),
    ((A_Q_HEADS + A_KV_HEADS) * HEAD_DIM, A_KV_HEADS, "v"),
    ((A_Q_HEADS + 2 * A_KV_HEADS) * HEAD_DIM, B_Q_HEADS, "q"),
    ((A_Q_HEADS + 2 * A_KV_HEADS + B_Q_HEADS) * HEAD_DIM, B_KV_HEADS, "k"),
    ((A_Q_HEADS + 2 * A_KV_HEADS + B_Q_HEADS + B_KV_HEADS) * HEAD_DIM, B_KV_HEADS, "v"),
)
LAT_TM = 512


def kernel(x, c, ctx, c_ctx, ada_w, ada_b, norm_w, mlp_w1, mlp_w2, ev_w_in, ev_w_out, ev_q_norm,
           ev_k_norm, ev_sink, od_w_in, od_w_out, od_rpb, final_norm_w):
    b, s, d = x.shape
    n_ctx = ctx.shape[1]
    depth = ada_w.shape[0]
    rows = s // GRID_W

    mod = _ada_mod(jnp.concatenate([c, c_ctx[None, :]], axis=0), ada_w, ada_b)
    rope_tabs = _rope_tables(s)

    def mod_vecs(layer):
        lat = [mod[layer, :b, k * d:(k + 1) * d][:, None, :] for k in range(6)]
        cx = [jnp.broadcast_to(mod[layer, b:b + 1, k * d:(k + 1) * d][:, None, :], (b, 1, d))
              for k in range(6)]
        return lat, cx

    for layer in range(depth):
        need_ctx = layer < depth - 1
        (sh1, sc1, g1, sh2, sc2, g2), (sh1c, sc1c, g1c, sh2c, sc2c, g2c) = mod_vecs(layer)
        j = layer // 2
        w1 = mlp_w1[layer].astype(BF16)
        w2 = mlp_w2[layer].astype(BF16)
        if layer % 2 == 0:
            w_in = ev_w_in[j].astype(BF16)
            w_out = ev_w_out[j].astype(BF16)
            n_in = w_in.shape[1]
            norms = (ev_q_norm[j], ev_k_norm[j])
            qkv = _proj(x, norm_w[layer, 0], sh1, sc1, w_in, tm=LAT_TM, tn=n_in,
                        sections=EVEN_SECTIONS, head_norms=norms, rope_tabs=rope_tabs, name="proj_even")
            qkv_c = _proj(ctx, norm_w[layer, 0], sh1c, sc1c, w_in, tm=n_ctx, tn=n_in,
                          sections=EVEN_SECTIONS, head_norms=norms, name="proj_even_ctx")
            oa = _gattn(qkv, qkv_c, q_blk=0, k_blk=A_Q_HEADS, v_blk=A_Q_HEADS + A_KV_HEADS)
            sink_kg = ev_sink[j].reshape(B_KV_HEADS, GQA_GROUP).astype(F32)
            qb_blk = (A_Q_HEADS + 2 * A_KV_HEADS) // GQA_GROUP
            kb_blk = A_Q_HEADS + 2 * A_KV_HEADS + B_Q_HEADS
            ob = _wattn(sink_kg, qkv, qkv_c, q_blk=qb_blk, k_blk=kb_blk, v_blk=kb_blk + B_KV_HEADS)
            o_parts = [oa, ob]
            if need_ctx:
                sink_all = jnp.concatenate([jnp.full((A_KV_HEADS, GQA_GROUP), NEG, F32), sink_kg], axis=0)
                oc_parts = [_cattn(sink_all, qkv_c)]
        else:
            w_in = od_w_in[j].astype(BF16)
            w_out = od_w_out[j].astype(BF16)
            hd = C_HEADS * HEAD_DIM
            qkv = _proj(x, norm_w[layer, 0], sh1, sc1, w_in, tm=LAT_TM, tn=hd, name="proj_odd")
            kv_c = _proj(ctx, norm_w[layer, 0], sh1c, sc1c, w_in, tm=n_ctx, tn=hd, n_off=1,
                         n_out=2 * hd, name="proj_odd_ctx")
            o_parts = [_nattn(qkv, kv_c, _na_bias_tables(od_rpb[j], rows))]
            assert not need_ctx, "the odd layer is the last one at this depth"
        x = _outproj(x, o_parts, w_out, g1, tm=LAT_TM)
        x = _mlp(x, norm_w[layer, 1], sh2, sc2, g2, w1, w2, tm=LAT_TM, tf=1024,
                 final_w=None if need_ctx else final_norm_w)
        if need_ctx:
            ctx = _outproj(ctx, oc_parts, w_out, g1c, tm=n_ctx)
            ctx = _mlp(ctx, norm_w[layer, 1], sh2c, sc2c, g2c, w1, w2, tm=n_ctx, tf=1024)
    return x
```

```python
import functools

import jax
import jax.numpy as jnp
import numpy as np
from jax import lax
from jax.experimental import pallas as pl
from jax.experimental.pallas import tpu as pltpu

D_MODEL = 2048
GRID_W = 64
HEAD_DIM = 128
N_HEADS = D_MODEL // HEAD_DIM
A_Q_HEADS = N_HEADS // 2
A_KV_HEADS = max(1, A_Q_HEADS // 4)
B_Q_HEADS = N_HEADS - A_Q_HEADS
B_KV_HEADS = max(1, B_Q_HEADS // 4)
GQA_GROUP = A_Q_HEADS // A_KV_HEADS
C_HEADS = N_HEADS
WINDOW = 128
NA_KH = 8
NA_KW = 16
D_FF = 4 * D_MODEL
ROPE_THETA = 10000.0
ROPE_PAIRS = HEAD_DIM // 4
NORM_EPS = 1e-6
NEG = -1e30
ATTN_SCALE = HEAD_DIM ** -0.5

V7X_VMEM_BYTES = 64 * 1024 * 1024
VMEM_LIMIT = V7X_VMEM_BYTES - 8 * 1024 * 1024
LANES = 128
SUBLANES = 8

F32 = jnp.float32
BF16 = jnp.bfloat16


def _params(*sem):
    return pltpu.CompilerParams(dimension_semantics=sem, vmem_limit_bytes=VMEM_LIMIT)


ADA_ROWS = 3
ADA_TN = 1024


def _ada_kernel(cb_ref, w_ref, b_ref, o_ref, s_ref):
    @pl.when((pl.program_id(0) == 0) & (pl.program_id(1) == 0))
    def _():
        cvals = cb_ref[...]
        s_ref[...] = cvals * (1.0 / (1.0 + jnp.exp(-cvals)))

    nchunk = ADA_TN // LANES

    def body(kk, accs):
        k0 = pl.multiple_of(kk * SUBLANES, SUBLANES)
        w = w_ref[pl.ds(k0, SUBLANES), :]
        out = []
        for r in range(ADA_ROWS):
            s = s_ref[r, pl.ds(k0, SUBLANES), :]
            for cch in range(nchunk):
                out.append(accs[r * nchunk + cch] + w[:, cch * LANES:(cch + 1) * LANES] * s)
        return tuple(out)

    zeros = tuple(jnp.zeros((SUBLANES, LANES), F32) for _ in range(ADA_ROWS * nchunk))
    accs = lax.fori_loop(0, D_MODEL // SUBLANES, body, zeros, unroll=2)
    o_ref[...] = jnp.broadcast_to(b_ref[...], o_ref.shape)
    for r in range(ADA_ROWS):
        row = jnp.concatenate(
            [jnp.sum(accs[r * nchunk + cch], axis=0, keepdims=True) for cch in range(nchunk)], axis=1)
        o_ref[r:r + 1, :] = row + b_ref[...]


def _ada_mod(cc, ada_w, ada_b):
    depth, d, n = ada_w.shape
    cb = jnp.broadcast_to(cc[:, :, None], (ADA_ROWS, d, LANES))
    return pl.pallas_call(
        _ada_kernel,
        out_shape=jax.ShapeDtypeStruct((depth, SUBLANES, n), F32),
        grid=(depth, n // ADA_TN),
        in_specs=[
            pl.BlockSpec((ADA_ROWS, d, LANES), lambda l, j: (0, 0, 0)),
            pl.BlockSpec((None, d, ADA_TN), lambda l, j: (l, 0, j)),
            pl.BlockSpec((None, 1, ADA_TN), lambda l, j: (l, 0, j)),
        ],
        out_specs=pl.BlockSpec((None, SUBLANES, ADA_TN), lambda l, j: (l, 0, j)),
        scratch_shapes=[pltpu.VMEM((ADA_ROWS, d, LANES), F32)],
        compiler_params=_params("arbitrary", "arbitrary"),
        name="ada_mod",
    )(cb, ada_w, ada_b.reshape(depth, 1, n))


def _norm_modulate(x, nw, shift, scale):
    ms = jnp.mean(x * x, axis=-1, keepdims=True)
    y = x * lax.rsqrt(ms + NORM_EPS) * nw
    return y * (1.0 + scale) + shift


def _rope(y, cos, sin_lo, sin_hi):
    return (y * cos + pltpu.roll(y, HEAD_DIM - ROPE_PAIRS, 1) * sin_lo
            + pltpu.roll(y, ROPE_PAIRS, 1) * sin_hi)


def _rope_tables(seq):
    t = jnp.arange(seq)
    row = (t // GRID_W).astype(F32)
    col = (t % GRID_W).astype(F32)
    inv = ROPE_THETA ** (-jnp.arange(ROPE_PAIRS, dtype=F32) / ROPE_PAIRS)
    ang_r = row[:, None] * inv
    ang_c = col[:, None] * inv
    ang = jnp.concatenate([ang_r, ang_r, ang_c, ang_c], axis=-1)
    cos, sin = jnp.cos(ang), jnp.sin(ang)
    first = (jnp.arange(HEAD_DIM) % (2 * ROPE_PAIRS)) < ROPE_PAIRS
    sin_lo = jnp.where(first, -sin, 0.0)
    sin_hi = jnp.where(first, 0.0, sin)
    return cos, sin_lo, sin_hi


def _proj_kernel(*refs, sections, rope):
    x_ref, nw_ref, sh_ref, sc_ref, w_ref = refs[:5]
    rest = refs[5:]
    if sections is not None:
        qn_ref, kn_ref = rest[:2]
        rest = rest[2:]
    if rope:
        cos_ref, slo_ref, shi_ref = rest[:3]
        rest = rest[3:]
    o_ref, h_ref = rest

    @pl.when(pl.program_id(2) == 0)
    def _():
        h = _norm_modulate(x_ref[...], nw_ref[...], sh_ref[...], sc_ref[...])
        h_ref[...] = h.astype(BF16)

    if sections is None:
        o_ref[...] = jnp.dot(h_ref[...], w_ref[...], preferred_element_type=F32).astype(o_ref.dtype)
        return

    for c0, nh, kind in sections:
        y = jnp.dot(h_ref[...], w_ref[:, c0:c0 + nh * HEAD_DIM], preferred_element_type=F32)
        for hh in range(nh):
            yh = y[:, hh * HEAD_DIM:(hh + 1) * HEAD_DIM]
            if kind in ("q_norm", "k_norm"):
                hw = (qn_ref if kind == "q_norm" else kn_ref)[...]
                yh = yh * lax.rsqrt(jnp.mean(yh * yh, axis=-1, keepdims=True) + NORM_EPS) * hw
            if rope and kind != "v":
                yh = _rope(yh, cos_ref[...], slo_ref[...], shi_ref[...])
            lo = c0 + hh * HEAD_DIM
            o_ref[:, lo:lo + HEAD_DIM] = yh.astype(o_ref.dtype)


def _proj(x, nw, shift, scale, w, *, tm, tn, n_off=0, n_out=None, sections=None,
          head_norms=None, rope_tabs=None, name="proj"):
    b, s, d = x.shape
    n_out = w.shape[1] if n_out is None else n_out
    nj = n_out // tn
    in_specs = [
        pl.BlockSpec((None, tm, d), lambda bi, i, j: (bi, i, 0)),
        pl.BlockSpec((1, d), lambda bi, i, j: (0, 0)),
        pl.BlockSpec((None, 1, d), lambda bi, i, j: (bi, 0, 0)),
        pl.BlockSpec((None, 1, d), lambda bi, i, j: (bi, 0, 0)),
        pl.BlockSpec((d, tn), lambda bi, i, j: (0, j + n_off)),
    ]
    args = [x, nw.reshape(1, d), shift, scale, w]
    if sections is not None:
        assert nj == 1
        in_specs += [pl.BlockSpec((1, HEAD_DIM), lambda bi, i, j: (0, 0))] * 2
        args += [head_norms[0].reshape(1, HEAD_DIM), head_norms[1].reshape(1, HEAD_DIM)]
    if rope_tabs is not None:
        in_specs += [pl.BlockSpec((tm, HEAD_DIM), lambda bi, i, j: (i, 0))] * 3
        args += list(rope_tabs)
    return pl.pallas_call(
        functools.partial(_proj_kernel, sections=sections, rope=rope_tabs is not None),
        out_shape=jax.ShapeDtypeStruct((b, s, n_out), BF16),
        grid=(b, s // tm, nj),
        in_specs=in_specs,
        out_specs=pl.BlockSpec((None, tm, tn), lambda bi, i, j: (bi, i, j)),
        scratch_shapes=[pltpu.VMEM((tm, d), BF16)],
        compiler_params=_params("arbitrary", "arbitrary", "arbitrary"),
        name=name,
    )(*args)


def _qk(q, k):
    return lax.dot_general(q, k, (((1,), (1,)), ((), ())), preferred_element_type=F32)


EXP2_SCALE = ATTN_SCALE * float(np.log2(np.e))
INV_ATTN_SCALE = 1.0 / ATTN_SCALE


def _softmax_pv(parts, sink=None):
    m = functools.reduce(jnp.maximum, [jnp.max(s, axis=-1, keepdims=True) for s, _ in parts])
    if sink is not None:
        m = jnp.maximum(m, sink)
    l = None
    o = None
    for s, v in parts:
        p = jnp.exp2((s - m) * EXP2_SCALE)
        ls = jnp.sum(p, axis=-1, keepdims=True)
        os_ = jnp.dot(p.astype(BF16), v, preferred_element_type=F32)
        l = ls if l is None else l + ls
        o = os_ if o is None else o + os_
    if sink is not None:
        l = l + jnp.exp2((sink - m) * EXP2_SCALE)
    return o / l


def _run_chains(n_blk, n_inner, scores, probs, weighted_values, unroll=1):
    per_body = unroll * n_inner
    assert per_body % 2 == 0 and n_blk % unroll == 0

    def chain(it, n):
        return it * unroll + n // n_inner, n % n_inner

    scores(0, 0, 0)

    def body(it, carry):
        for n in range(per_body):
            blk, g = chain(it, n + 1)
            scores(jnp.minimum(blk, n_blk - 1) if n + 1 == per_body else blk, g, (n + 1) % 2)
            probs(n % n_inner, n % 2)
            blk, g = chain(it, n - 1)
            weighted_values(jnp.maximum(blk, 0) if n == 0 else blk, g, (n - 1) % 2)
        return carry

    lax.fori_loop(0, n_blk // unroll, body, 0)
    weighted_values(n_blk - 1, n_inner - 1, 1)


GATTN_ROWS = 256


def _gattn_kernel(q_ref, k_ref, v_ref, kc_ref, vc_ref, o_ref, kall_ref, vall_ref, s_ref, p_ref):
    c = kc_ref.shape[0]

    n_blk = q_ref.shape[0] // GATTN_ROWS

    kall_ref[:c, :] = kc_ref[...]
    kall_ref[c:, :] = k_ref[...]
    vall_ref[:c, :HEAD_DIM] = vc_ref[...]
    vall_ref[c:, :HEAD_DIM] = v_ref[...]
    lane = lax.broadcasted_iota(jnp.int32, (vall_ref.shape[0], HEAD_DIM), 1)
    vall_ref[:, HEAD_DIM:] = jnp.where(lane == 0, 1.0, 0.0).astype(BF16)

    def rows_of(blk):
        return pl.ds(pl.multiple_of(blk * GATTN_ROWS, GATTN_ROWS), GATTN_ROWS)

    def cols_of(g):
        return slice(g * HEAD_DIM, (g + 1) * HEAD_DIM)

    def scores(blk, g, slot):
        s_ref[slot] = _qk(q_ref[rows_of(blk), cols_of(g)], kall_ref[...])

    def probs(g, slot):
        m = jnp.max(s_ref[slot], axis=-1, keepdims=True)
        p_ref[slot] = jnp.exp2((s_ref[slot] - m) * EXP2_SCALE).astype(BF16)

    def weighted_values(blk, g, slot):
        ol = jnp.dot(p_ref[slot], vall_ref[...], preferred_element_type=F32)
        o = ol[:, :HEAD_DIM] / ol[:, HEAD_DIM:HEAD_DIM + 1]
        o_ref[rows_of(blk), cols_of(g)] = o.astype(o_ref.dtype)

    p_ref[1] = jnp.ones(p_ref.shape[1:], BF16)
    _run_chains(n_blk, GQA_GROUP, scores, probs, weighted_values)


def _gattn(qkv, qkv_c, *, q_blk, k_blk, v_blk):
    b, s, _ = qkv.shape
    c = qkv_c.shape[1]
    gw = GQA_GROUP * HEAD_DIM
    assert GQA_GROUP % 2 == 0 and s % GATTN_ROWS == 0
    return pl.pallas_call(
        _gattn_kernel,
        out_shape=jax.ShapeDtypeStruct((b, s, A_Q_HEADS * HEAD_DIM), BF16),
        grid=(b, A_KV_HEADS),
        in_specs=[
            pl.BlockSpec((None, s, gw), lambda bi, kv: (bi, 0, q_blk + kv)),
            pl.BlockSpec((None, s, HEAD_DIM), lambda bi, kv: (bi, 0, k_blk + kv)),
            pl.BlockSpec((None, s, HEAD_DIM), lambda bi, kv: (bi, 0, v_blk + kv)),
            pl.BlockSpec((None, c, HEAD_DIM), lambda bi, kv: (bi, 0, k_blk + kv)),
            pl.BlockSpec((None, c, HEAD_DIM), lambda bi, kv: (bi, 0, v_blk + kv)),
        ],
        out_specs=pl.BlockSpec((None, s, gw), lambda bi, kv: (bi, 0, kv)),
        scratch_shapes=[
            pltpu.VMEM((c + s, HEAD_DIM), BF16),
            pltpu.VMEM((c + s, 2 * HEAD_DIM), BF16),
            pltpu.VMEM((2, GATTN_ROWS, c + s), F32),
            pltpu.VMEM((2, GATTN_ROWS, c + s), BF16),
        ],
        compiler_params=_params("arbitrary", "arbitrary"),
        name="gattn",
    )(qkv, qkv, qkv, qkv_c, qkv_c)


WATTN_TQ = 4 * WINDOW
WATTN_NK = WATTN_TQ + 2 * WINDOW


def _wattn_kernel(sink_ref, q_ref, k_ref, v_ref, kc_ref, vc_ref, o_ref,
                  mask_ref, s_ref, sc_ref, p_ref, pc_ref, l_ref):
    kv = pl.program_id(1)
    seq = q_ref.shape[0]
    n_tile = seq // WATTN_TQ

    def rows_of(tile):
        return pl.ds(pl.multiple_of(tile * WATTN_TQ, WATTN_TQ), WATTN_TQ)

    def key0_of(tile):
        lo, hi = 0, seq - WATTN_NK
        k0 = tile * WATTN_TQ - WINDOW
        return min(max(k0, lo), hi) if isinstance(tile, int) else jnp.clip(k0, lo, hi)

    def key_rows_of(tile):
        return pl.ds(pl.multiple_of(key0_of(tile), WINDOW), WATTN_NK)

    def cols_of(g):
        return slice(g * HEAD_DIM, (g + 1) * HEAD_DIM)

    for var, tile in enumerate((0, 1, n_tile - 1)):
        shift = key0_of(tile) - tile * WATTN_TQ
        dist = (shift + lax.broadcasted_iota(jnp.int32, (WATTN_TQ, WATTN_NK), 1)
                - lax.broadcasted_iota(jnp.int32, (WATTN_TQ, WATTN_NK), 0))
        mask_ref[var] = jnp.where(jnp.abs(dist) <= WINDOW, 0.0, NEG)

    def scores(tile, g, slot):
        q = q_ref[rows_of(tile), cols_of(g)]
        var = jnp.where(tile == 0, 0, jnp.where(tile == n_tile - 1, 2, 1))
        s_ref[slot] = _qk(q, k_ref[key_rows_of(tile), :]) + mask_ref[var]
        sc_ref[slot] = _qk(q, kc_ref[...])

    def probs(g, slot):
        s, sc = s_ref[slot], sc_ref[slot]
        sink = sink_ref[kv, g] * INV_ATTN_SCALE
        m = jnp.maximum(jnp.max(s, axis=-1, keepdims=True), jnp.max(sc, axis=-1, keepdims=True))
        m = jnp.maximum(m, sink)
        p = jnp.exp2((s - m) * EXP2_SCALE)
        pc = jnp.exp2((sc - m) * EXP2_SCALE)
        l_ref[slot] = (jnp.sum(p, axis=-1, keepdims=True) + jnp.sum(pc, axis=-1, keepdims=True)
                       + jnp.exp2((sink - m) * EXP2_SCALE))
        p_ref[slot] = p.astype(BF16)
        pc_ref[slot] = pc.astype(BF16)

    def weighted_values(tile, g, slot):
        o = (jnp.dot(p_ref[slot], v_ref[key_rows_of(tile), :], preferred_element_type=F32)
             + jnp.dot(pc_ref[slot], vc_ref[...], preferred_element_type=F32))
        o_ref[rows_of(tile), cols_of(g)] = (o / l_ref[slot]).astype(o_ref.dtype)

    p_ref[1] = jnp.ones(p_ref.shape[1:], BF16)
    pc_ref[1] = jnp.ones(pc_ref.shape[1:], BF16)
    l_ref[1] = jnp.ones(l_ref.shape[1:], F32)
    _run_chains(seq // WATTN_TQ, GQA_GROUP, scores, probs, weighted_values)


def _wattn(sink_kg, qkv, qkv_c, *, q_blk, k_blk, v_blk):
    b, s, _ = qkv.shape
    c = qkv_c.shape[1]
    gw = GQA_GROUP * HEAD_DIM
    assert s % WATTN_TQ == 0 and s >= 3 * WATTN_TQ

    def col_map(col):
        return lambda bi, kv: (bi, 0, col + kv)

    return pl.pallas_call(
        _wattn_kernel,
        out_shape=jax.ShapeDtypeStruct((b, s, B_Q_HEADS * HEAD_DIM), BF16),
        grid=(b, B_KV_HEADS),
        in_specs=[
            pl.BlockSpec(memory_space=pltpu.SMEM),
            pl.BlockSpec((None, s, gw), col_map(q_blk)),
            pl.BlockSpec((None, s, HEAD_DIM), col_map(k_blk)),
            pl.BlockSpec((None, s, HEAD_DIM), col_map(v_blk)),
            pl.BlockSpec((None, c, HEAD_DIM), col_map(k_blk)),
            pl.BlockSpec((None, c, HEAD_DIM), col_map(v_blk)),
        ],
        out_specs=pl.BlockSpec((None, s, gw), col_map(0)),
        scratch_shapes=[
            pltpu.VMEM((3, WATTN_TQ, WATTN_NK), F32),
            pltpu.VMEM((2, WATTN_TQ, WATTN_NK), F32),
            pltpu.VMEM((2, WATTN_TQ, c), F32),
            pltpu.VMEM((2, WATTN_TQ, WATTN_NK), BF16),
            pltpu.VMEM((2, WATTN_TQ, c), BF16),
            pltpu.VMEM((2, WATTN_TQ, 1), F32),
        ],
        compiler_params=_params("arbitrary", "arbitrary"),
        name="wattn",
    )(sink_kg, qkv, qkv, qkv, qkv_c, qkv_c)


def _cattn_kernel(sink_ref, q_ref, k_ref, v_ref, o_ref):
    grp = pl.program_id(1)
    k, v = k_ref[...], v_ref[...]
    for g in range(GQA_GROUP):
        q = q_ref[:, g * HEAD_DIM:(g + 1) * HEAD_DIM]
        o = _softmax_pv([(_qk(q, k), v)], sink=sink_ref[grp, g] * INV_ATTN_SCALE)
        o_ref[:, g * HEAD_DIM:(g + 1) * HEAD_DIM] = o.astype(o_ref.dtype)


def _cattn(sink_all, qkv_c):
    b, c, _ = qkv_c.shape
    gw = GQA_GROUP * HEAD_DIM
    n_grp = A_KV_HEADS + B_KV_HEADS
    q_map = lambda bi, g: (bi, 0, g + g // A_KV_HEADS)
    k_map = lambda bi, g: (bi, 0, 8 + g + 10 * (g // A_KV_HEADS))
    v_map = lambda bi, g: (bi, 0, 10 + g + 10 * (g // A_KV_HEADS))
    return pl.pallas_call(
        _cattn_kernel,
        out_shape=jax.ShapeDtypeStruct((b, c, N_HEADS * HEAD_DIM), BF16),
        grid=(b, n_grp),
        in_specs=[
            pl.BlockSpec(memory_space=pltpu.SMEM),
            pl.BlockSpec((None, c, gw), q_map),
            pl.BlockSpec((None, c, HEAD_DIM), k_map),
            pl.BlockSpec((None, c, HEAD_DIM), v_map),
        ],
        out_specs=pl.BlockSpec((None, c, gw), lambda bi, g: (bi, 0, g)),
        compiler_params=_params("arbitrary", "arbitrary"),
        name="cattn",
    )(sink_all, qkv_c, qkv_c, qkv_c)


NATTN_ROWS = 4
NATTN_TQ = NATTN_ROWS * GRID_W
NATTN_KROWS = 3 * NATTN_ROWS


NATTN_HG = 2
NATTN_UNROLL = 2
N_RI = 2 * NA_KH - 1
N_CI = 2 * NA_KW - 1


def _na_key_row0(tile, rows):
    lo, hi = 0, rows - NATTN_KROWS
    r0 = (tile - 1) * NATTN_ROWS
    if isinstance(tile, int):
        return min(max(r0, lo), hi)
    return jnp.clip(r0, lo, hi)


def _nattn_kernel(q_ref, k_ref, v_ref, kc_ref, vc_ref, bias_ref, o_ref,
                  ks_ref, vs_ref, s_ref, sc_ref, p_ref, pc_ref, l_ref):
    seq = q_ref.shape[0]
    n_tile = seq // NATTN_TQ
    nk = NATTN_KROWS * GRID_W

    def rows_of(tile):
        return pl.ds(pl.multiple_of(tile * NATTN_TQ, NATTN_TQ), NATTN_TQ)

    def key_rows_of(tile):
        return pl.ds(pl.multiple_of(_na_key_row0(tile, seq // GRID_W) * GRID_W, NATTN_TQ), nk)

    def cols_of(g):
        return slice(g * HEAD_DIM, (g + 1) * HEAD_DIM)

    for g in range(NATTN_HG):
        ks_ref[g] = k_ref[:, cols_of(g)]
        vs_ref[g] = v_ref[:, cols_of(g)]

    def scores(tile, g, slot):
        q = q_ref[rows_of(tile), cols_of(g)]
        var = jnp.where(tile == 0, 0, jnp.where(tile == n_tile - 1, 2, 1))
        s_ref[slot] = _qk(q, ks_ref[g, key_rows_of(tile), :]) + bias_ref[g, var]
        sc_ref[slot] = _qk(q, kc_ref[:, cols_of(g)])

    def probs(g, slot):
        s, sc = s_ref[slot], sc_ref[slot]
        m = jnp.maximum(jnp.max(s, axis=-1, keepdims=True), jnp.max(sc, axis=-1, keepdims=True))
        p = jnp.exp2((s - m) * EXP2_SCALE)
        pc = jnp.exp2((sc - m) * EXP2_SCALE)
        l_ref[slot] = jnp.sum(p, axis=-1, keepdims=True) + jnp.sum(pc, axis=-1, keepdims=True)
        p_ref[slot] = p.astype(BF16)
        pc_ref[slot] = pc.astype(BF16)

    def weighted_values(tile, g, slot):
        o = (jnp.dot(p_ref[slot], vs_ref[g, key_rows_of(tile), :], preferred_element_type=F32)
             + jnp.dot(pc_ref[slot], vc_ref[:, cols_of(g)], preferred_element_type=F32))
        o_ref[rows_of(tile), cols_of(g)] = (o / l_ref[slot]).astype(o_ref.dtype)

    p_ref[1] = jnp.ones(p_ref.shape[1:], BF16)
    pc_ref[1] = jnp.ones(pc_ref.shape[1:], BF16)
    l_ref[1] = jnp.ones(l_ref.shape[1:], F32)
    _run_chains(n_tile, NATTN_HG, scores, probs, weighted_values, unroll=NATTN_UNROLL)


def _na_bias_kernel(rpb_ref, o_ref, u_ref, *, rows):
    base = pl.program_id(0) * (N_RI * N_CI)
    kh = min(NA_KH, rows)

    def fill(sg, carry):
        q0 = pl.multiple_of(sg * SUBLANES, SUBLANES)
        qc = q0 + lax.broadcasted_iota(jnp.int32, (SUBLANES, LANES), 0)
        kc = lax.broadcasted_iota(jnp.int32, (SUBLANES, LANES), 1) & (GRID_W - 1)
        diag = kc - qc + (NA_KW - 1)
        us = [jnp.zeros((SUBLANES, LANES), F32) for _ in range(N_RI)]
        for ci in range(N_CI):
            hit = diag == ci
            for a in range(N_RI):
                us[a] = jnp.where(hit, rpb_ref[base + a * N_CI + ci], us[a])
        for a in range(N_RI):
            u_ref[a, pl.ds(q0, SUBLANES), :] = us[a] * INV_ATTN_SCALE
        return carry

    lax.fori_loop(0, GRID_W // SUBLANES, fill, 0)

    qc = lax.broadcasted_iota(jnp.int32, (GRID_W, LANES), 0)
    lane = lax.broadcasted_iota(jnp.int32, (GRID_W, LANES), 1)
    kc = lane & (GRID_W - 1)
    cs = jnp.clip(qc - NA_KW // 2, 0, GRID_W - NA_KW)
    col_valid = (kc >= cs) & (kc < cs + NA_KW)
    low = lane < GRID_W
    masks = {(True, True): col_valid, (True, False): col_valid & low,
             (False, True): col_valid & jnp.logical_not(low)}
    neg_blk = jnp.full((GRID_W, LANES), NEG, F32)
    n_tile = rows // NATTN_ROWS
    for var, tile in enumerate((0, 1, n_tile - 1)):
        r0 = tile * NATTN_ROWS
        k0 = _na_key_row0(tile, rows)
        for qr in range(NATTN_ROWS):
            rs = min(max(r0 + qr - kh // 2, 0), rows - kh)
            for jj in range(NATTN_KROWS // 2):
                kr = k0 + 2 * jj
                ok = (rs <= kr < rs + kh, rs <= kr + 1 < rs + kh)
                if ok == (False, False):
                    blk = neg_blk
                else:
                    a = kr - (r0 + qr) + NA_KH - 1
                    a_lo = min(max(a, 0), N_RI - 1)
                    a_hi = min(max(a + 1, 0), N_RI - 1)
                    pair = jnp.where(low, u_ref[a_lo], u_ref[a_hi])
                    blk = jnp.where(masks[ok], pair, NEG)
                o_ref[var, qr * GRID_W:(qr + 1) * GRID_W, jj * LANES:(jj + 1) * LANES] = blk


def _na_bias_tables(rpb, rows):
    h = rpb.shape[0]
    assert 2 * GRID_W == LANES and rows >= 3 * NATTN_ROWS and NATTN_ROWS == NA_KH // 2
    return pl.pallas_call(
        functools.partial(_na_bias_kernel, rows=rows),
        out_shape=jax.ShapeDtypeStruct((h, 3, NATTN_TQ, NATTN_KROWS * GRID_W), F32),
        grid=(h,),
        in_specs=[pl.BlockSpec(memory_space=pltpu.SMEM)],
        out_specs=pl.BlockSpec((None, 3, NATTN_TQ, NATTN_KROWS * GRID_W), lambda hh: (hh, 0, 0, 0)),
        scratch_shapes=[pltpu.VMEM((N_RI, GRID_W, LANES), F32)],
        compiler_params=_params("arbitrary"),
        name="na_bias",
    )(rpb.reshape(-1))


def _nattn(qkv, kv_c, bias):
    b, s, _ = qkv.shape
    c = kv_c.shape[1]
    h = C_HEADS
    ng = h // NATTN_HG
    gw = NATTN_HG * HEAD_DIM
    nk = NATTN_KROWS * GRID_W
    assert NATTN_HG % 2 == 0 and s % NATTN_TQ == 0

    def col_map(col):
        return lambda hg, bi: (bi, 0, col + hg)

    return pl.pallas_call(
        _nattn_kernel,
        out_shape=jax.ShapeDtypeStruct((b, s, h * HEAD_DIM), BF16),
        grid=(ng, b),
        in_specs=[
            pl.BlockSpec((None, s, gw), col_map(0)),
            pl.BlockSpec((None, s, gw), col_map(ng)),
            pl.BlockSpec((None, s, gw), col_map(2 * ng)),
            pl.BlockSpec((None, c, gw), col_map(0)),
            pl.BlockSpec((None, c, gw), col_map(ng)),
            pl.BlockSpec((NATTN_HG, 3, NATTN_TQ, nk), lambda hg, bi: (hg, 0, 0, 0),
                         pipeline_mode=pl.Buffered(1)),
        ],
        out_specs=pl.BlockSpec((None, s, gw), col_map(0)),
        scratch_shapes=[
            pltpu.VMEM((NATTN_HG, s, HEAD_DIM), BF16),
            pltpu.VMEM((NATTN_HG, s, HEAD_DIM), BF16),
            pltpu.VMEM((2, NATTN_TQ, nk), F32),
            pltpu.VMEM((2, NATTN_TQ, c), F32),
            pltpu.VMEM((2, NATTN_TQ, nk), BF16),
            pltpu.VMEM((2, NATTN_TQ, c), BF16),
            pltpu.VMEM((2, NATTN_TQ, 1), F32),
        ],
        compiler_params=_params("arbitrary", "arbitrary"),
        name="nattn",
    )(qkv, qkv, qkv, kv_c, kv_c, bias)


def _outproj_kernel(*refs, widths):
    x_ref, w_ref, g_ref = refs[:3]
    o_refs = refs[3:3 + len(widths)]
    y_ref = refs[3 + len(widths)]
    y = None
    k0 = 0
    for o_ref, kw in zip(o_refs, widths):
        part = jnp.dot(o_ref[...], w_ref[k0:k0 + kw, :], preferred_element_type=F32)
        y = part if y is None else y + part
        k0 += kw
    y_ref[...] = x_ref[...] + g_ref[...] * y


def _outproj(x, o_parts, w, gate, *, tm):
    b, s, d = x.shape
    widths = tuple(o.shape[2] for o in o_parts)
    assert sum(widths) == w.shape[0]
    return pl.pallas_call(
        functools.partial(_outproj_kernel, widths=widths),
        out_shape=jax.ShapeDtypeStruct(x.shape, x.dtype),
        grid=(b, s // tm),
        in_specs=[
            pl.BlockSpec((None, tm, d), lambda bi, i: (bi, i, 0)),
            pl.BlockSpec(w.shape, lambda bi, i: (0, 0)),
            pl.BlockSpec((None, 1, d), lambda bi, i: (bi, 0, 0)),
        ] + [pl.BlockSpec((None, tm, kw), lambda bi, i: (bi, i, 0)) for kw in widths],
        out_specs=pl.BlockSpec((None, tm, d), lambda bi, i: (bi, i, 0)),
        compiler_params=_params("arbitrary", "arbitrary"),
        name="outproj",
    )(x, w, gate, *o_parts)


def _mlp_kernel(*refs, final_norm):
    x_ref, nw_ref, sh_ref, sc_ref, g_ref, w1_ref, w2_ref = refs[:7]
    rest = refs[7:]
    if final_norm:
        fw_ref = rest[0]
        rest = rest[1:]
    y_ref, h_ref, acc_ref = rest
    f = pl.program_id(2)

    @pl.when(f == 0)
    def _():
        h = _norm_modulate(x_ref[...], nw_ref[...], sh_ref[...], sc_ref[...])
        h_ref[...] = h.astype(BF16)
        acc_ref[...] = jnp.zeros_like(acc_ref)

    a = jnp.dot(h_ref[...], w1_ref[...], preferred_element_type=F32)
    a = jnp.square(jnp.maximum(a, 0.0)).astype(BF16)
    acc_ref[...] += jnp.dot(a, w2_ref[...], preferred_element_type=F32)

    @pl.when(f == pl.num_programs(2) - 1)
    def _():
        y = x_ref[...] + g_ref[...] * acc_ref[...]
        if final_norm:
            y = y * lax.rsqrt(jnp.mean(y * y, axis=-1, keepdims=True) + NORM_EPS) * fw_ref[...]
        y_ref[...] = y


def _mlp(x, nw, shift, scale, gate, w1, w2, *, tm, tf, final_w=None):
    b, s, d = x.shape
    dff = w1.shape[1]
    vec = pl.BlockSpec((None, 1, d), lambda bi, i, f: (bi, 0, 0))
    in_specs = [
        pl.BlockSpec((None, tm, d), lambda bi, i, f: (bi, i, 0)),
        pl.BlockSpec((1, d), lambda bi, i, f: (0, 0)),
        vec, vec, vec,
        pl.BlockSpec((d, tf), lambda bi, i, f: (0, f)),
        pl.BlockSpec((tf, d), lambda bi, i, f: (f, 0)),
    ]
    args = [x, nw.reshape(1, d), shift, scale, gate, w1, w2]
    if final_w is not None:
        in_specs.append(pl.BlockSpec((1, d), lambda bi, i, f: (0, 0)))
        args.append(final_w.reshape(1, d))
    return pl.pallas_call(
        functools.partial(_mlp_kernel, final_norm=final_w is not None),
        out_shape=jax.ShapeDtypeStruct(x.shape, x.dtype),
        grid=(b, s // tm, dff // tf),
        in_specs=in_specs,
        out_specs=pl.BlockSpec((None, tm, d), lambda bi, i, f: (bi, i, 0)),
        scratch_shapes=[pltpu.VMEM((tm, d), BF16), pltpu.VMEM((tm, d), F32)],
        compiler_params=_params("arbitrary", "arbitrary", "arbitrary"),
        name="mlp",
    )(*args)


EVEN_SECTIONS = (
    (0, A_Q_HEADS, "q_norm"),
    (A_Q_HEADS * HEAD_DIM, A_KV_HEADS, "k_norm"),
    ((A_Q_HEADS + A_KV_HEADS) * HEAD_DIM, A_KV_HEADS, "v"),
    ((A_Q_HEADS + 2 * A_KV_HEADS) * HEAD_DIM, B_Q_HEADS, "q"),
    ((A_Q_HEADS + 2 * A_KV_HEADS + B_Q_HEADS) * HEAD_DIM, B_KV_HEADS, "k"),
    ((A_Q_HEADS + 2 * A_KV_HEADS + B_Q_HEADS + B_KV_HEADS) * HEAD_DIM, B_KV_HEADS, "v"),
)
LAT_TM = 512


def kernel(x, c, ctx, c_ctx, ada_w, ada_b, norm_w, mlp_w1, mlp_w2, ev_w_in, ev_w_out, ev_q_norm,
           ev_k_norm, ev_sink, od_w_in, od_w_out, od_rpb, final_norm_w):
    b, s, d = x.shape
    n_ctx = ctx.shape[1]
    depth = ada_w.shape[0]
    rows = s // GRID_W

    mod = _ada_mod(jnp.concatenate([c, c_ctx[None, :]], axis=0), ada_w, ada_b)
    rope_tabs = _rope_tables(s)

    def mod_vecs(layer):
        lat = [mod[layer, :b, k * d:(k + 1) * d][:, None, :] for k in range(6)]
        cx = [jnp.broadcast_to(mod[layer, b:b + 1, k * d:(k + 1) * d][:, None, :], (b, 1, d))
              for k in range(6)]
        return lat, cx

    for layer in range(depth):
        need_ctx = layer < depth - 1
        (sh1, sc1, g1, sh2, sc2, g2), (sh1c, sc1c, g1c, sh2c, sc2c, g2c) = mod_vecs(layer)
        j = layer // 2
        w1 = mlp_w1[layer].astype(BF16)
        w2 = mlp_w2[layer].astype(BF16)
        if layer % 2 == 0:
            w_in = ev_w_in[j].astype(BF16)
            w_out = ev_w_out[j].astype(BF16)
            n_in = w_in.shape[1]
            norms = (ev_q_norm[j], ev_k_norm[j])
            qkv = _proj(x, norm_w[layer, 0], sh1, sc1, w_in, tm=LAT_TM, tn=n_in,
                        sections=EVEN_SECTIONS, head_norms=norms, rope_tabs=rope_tabs, name="proj_even")
            qkv_c = _proj(ctx, norm_w[layer, 0], sh1c, sc1c, w_in, tm=n_ctx, tn=n_in,
                          sections=EVEN_SECTIONS, head_norms=norms, name="proj_even_ctx")
            oa = _gattn(qkv, qkv_c, q_blk=0, k_blk=A_Q_HEADS, v_blk=A_Q_HEADS + A_KV_HEADS)
            sink_kg = ev_sink[j].reshape(B_KV_HEADS, GQA_GROUP).astype(F32)
            qb_blk = (A_Q_HEADS + 2 * A_KV_HEADS) // GQA_GROUP
            kb_blk = A_Q_HEADS + 2 * A_KV_HEADS + B_Q_HEADS
            ob = _wattn(sink_kg, qkv, qkv_c, q_blk=qb_blk, k_blk=kb_blk, v_blk=kb_blk + B_KV_HEADS)
            o_parts = [oa, ob]
            if need_ctx:
                sink_all = jnp.concatenate([jnp.full((A_KV_HEADS, GQA_GROUP), NEG, F32), sink_kg], axis=0)
                oc_parts = [_cattn(sink_all, qkv_c)]
        else:
            w_in = od_w_in[j].astype(BF16)
            w_out = od_w_out[j].astype(BF16)
            hd = C_HEADS * HEAD_DIM
            qkv = _proj(x, norm_w[layer, 0], sh1, sc1, w_in, tm=LAT_TM, tn=hd, name="proj_odd")
            kv_c = _proj(ctx, norm_w[layer, 0], sh1c, sc1c, w_in, tm=n_ctx, tn=hd, n_off=1,
                         n_out=2 * hd, name="proj_odd_ctx")
            o_parts = [_nattn(qkv, kv_c, _na_bias_tables(od_rpb[j], rows))]
            assert not need_ctx, "the odd layer is the last one at this depth"
        x = _outproj(x, o_parts, w_out, g1, tm=LAT_TM)
        x = _mlp(x, norm_w[layer, 1], sh2, sc2, g2, w1, w2, tm=LAT_TM, tf=1024,
                 final_w=None if need_ctx else final_norm_w)
        if need_ctx:
            ctx = _outproj(ctx, oc_parts, w_out, g1c, tm=n_ctx)
            ctx = _mlp(ctx, norm_w[layer, 1], sh2c, sc2c, g2c, w1, w2, tm=n_ctx, tf=1024)
    return x
```

```python
import functools

import jax
import jax.numpy as jnp
import numpy as np
from jax import lax
from jax.experimental import pallas as pl
from jax.experimental.pallas import tpu as pltpu

D_MODEL = 2048
GRID_W = 64
HEAD_DIM = 128
N_HEADS = D_MODEL // HEAD_DIM
A_Q_HEADS = N_HEADS // 2
A_KV_HEADS = max(1, A_Q_HEADS // 4)
B_Q_HEADS = N_HEADS - A_Q_HEADS
B_KV_HEADS = max(1, B_Q_HEADS // 4)
GQA_GROUP = A_Q_HEADS // A_KV_HEADS
C_HEADS = N_HEADS
WINDOW = 128
NA_KH = 8
NA_KW = 16
D_FF = 4 * D_MODEL
ROPE_THETA = 10000.0
ROPE_PAIRS = HEAD_DIM // 4
NORM_EPS = 1e-6
NEG = -1e30
ATTN_SCALE = HEAD_DIM ** -0.5

V7X_VMEM_BYTES = 64 * 1024 * 1024
VMEM_LIMIT = V7X_VMEM_BYTES - 6 * 1024 * 1024
LANES = 128
SUBLANES = 8

F32 = jnp.float32
BF16 = jnp.bfloat16


def _params(*sem):
    return pltpu.CompilerParams(dimension_semantics=sem, vmem_limit_bytes=VMEM_LIMIT)


ADA_ROWS = 3
ADA_TN = 1024


def _ada_kernel(cb_ref, w_ref, b_ref, o_ref, s_ref):
    @pl.when((pl.program_id(0) == 0) & (pl.program_id(1) == 0))
    def _():
        cvals = cb_ref[...]
        s_ref[...] = cvals * (1.0 / (1.0 + jnp.exp(-cvals)))

    nchunk = ADA_TN // LANES

    def body(kk, accs):
        k0 = pl.multiple_of(kk * SUBLANES, SUBLANES)
        w = w_ref[pl.ds(k0, SUBLANES), :]
        out = []
        for r in range(ADA_ROWS):
            s = s_ref[r, pl.ds(k0, SUBLANES), :]
            for cch in range(nchunk):
                out.append(accs[r * nchunk + cch] + w[:, cch * LANES:(cch + 1) * LANES] * s)
        return tuple(out)

    zeros = tuple(jnp.zeros((SUBLANES, LANES), F32) for _ in range(ADA_ROWS * nchunk))
    accs = lax.fori_loop(0, D_MODEL // SUBLANES, body, zeros, unroll=2)
    o_ref[...] = jnp.broadcast_to(b_ref[...], o_ref.shape)
    for r in range(ADA_ROWS):
        row = jnp.concatenate(
            [jnp.sum(accs[r * nchunk + cch], axis=0, keepdims=True) for cch in range(nchunk)], axis=1)
        o_ref[r:r + 1, :] = row + b_ref[...]


def _ada_mod(cc, ada_w, ada_b):
    depth, d, n = ada_w.shape
    cb = jnp.broadcast_to(cc[:, :, None], (ADA_ROWS, d, LANES))
    return pl.pallas_call(
        _ada_kernel,
        out_shape=jax.ShapeDtypeStruct((depth, SUBLANES, n), F32),
        grid=(depth, n // ADA_TN),
        in_specs=[
            pl.BlockSpec((ADA_ROWS, d, LANES), lambda l, j: (0, 0, 0)),
            pl.BlockSpec((None, d, ADA_TN), lambda l, j: (l, 0, j)),
            pl.BlockSpec((None, 1, ADA_TN), lambda l, j: (l, 0, j)),
        ],
        out_specs=pl.BlockSpec((None, SUBLANES, ADA_TN), lambda l, j: (l, 0, j)),
        scratch_shapes=[pltpu.VMEM((ADA_ROWS, d, LANES), F32)],
        compiler_params=_params("arbitrary", "arbitrary"),
        name="ada_mod",
    )(cb, ada_w, ada_b.reshape(depth, 1, n))


def _norm_modulate(x, nw, shift, scale):
    ms = jnp.mean(x * x, axis=-1, keepdims=True)
    y = x * lax.rsqrt(ms + NORM_EPS) * nw
    return y * (1.0 + scale) + shift


def _rope(y, cos, sin_lo, sin_hi):
    return (y * cos + pltpu.roll(y, HEAD_DIM - ROPE_PAIRS, 1) * sin_lo
            + pltpu.roll(y, ROPE_PAIRS, 1) * sin_hi)


def _rope_tables(seq):
    t = jnp.arange(seq)
    row = (t // GRID_W).astype(F32)
    col = (t % GRID_W).astype(F32)
    inv = ROPE_THETA ** (-jnp.arange(ROPE_PAIRS, dtype=F32) / ROPE_PAIRS)
    ang_r = row[:, None] * inv
    ang_c = col[:, None] * inv
    ang = jnp.concatenate([ang_r, ang_r, ang_c, ang_c], axis=-1)
    cos, sin = jnp.cos(ang), jnp.sin(ang)
    first = (jnp.arange(HEAD_DIM) % (2 * ROPE_PAIRS)) < ROPE_PAIRS
    sin_lo = jnp.where(first, -sin, 0.0)
    sin_hi = jnp.where(first, 0.0, sin)
    return cos, sin_lo, sin_hi


def _proj_kernel(*refs, sections, rope):
    x_ref, nw_ref, sh_ref, sc_ref, w_ref = refs[:5]
    rest = refs[5:]
    if sections is not None:
        qn_ref, kn_ref = rest[:2]
        rest = rest[2:]
    if rope:
        cos_ref, slo_ref, shi_ref = rest[:3]
        rest = rest[3:]
    o_ref, h_ref = rest

    @pl.when(pl.program_id(2) == 0)
    def _():
        h = _norm_modulate(x_ref[...], nw_ref[...], sh_ref[...], sc_ref[...])
        h_ref[...] = h.astype(BF16)

    if sections is None:
        o_ref[...] = jnp.dot(h_ref[...], w_ref[...], preferred_element_type=F32).astype(o_ref.dtype)
        return

    for c0, nh, kind in sections:
        y = jnp.dot(h_ref[...], w_ref[:, c0:c0 + nh * HEAD_DIM], preferred_element_type=F32)
        for hh in range(nh):
            yh = y[:, hh * HEAD_DIM:(hh + 1) * HEAD_DIM]
            if kind in ("q_norm", "k_norm"):
                hw = (qn_ref if kind == "q_norm" else kn_ref)[...]
                yh = yh * lax.rsqrt(jnp.mean(yh * yh, axis=-1, keepdims=True) + NORM_EPS) * hw
            if rope and kind != "v":
                yh = _rope(yh, cos_ref[...], slo_ref[...], shi_ref[...])
            lo = c0 + hh * HEAD_DIM
            o_ref[:, lo:lo + HEAD_DIM] = yh.astype(o_ref.dtype)


def _proj(x, nw, shift, scale, w, *, tm, tn, n_off=0, n_out=None, sections=None,
          head_norms=None, rope_tabs=None, name="proj"):
    b, s, d = x.shape
    n_out = w.shape[1] if n_out is None else n_out
    nj = n_out // tn
    in_specs = [
        pl.BlockSpec((None, tm, d), lambda bi, i, j: (bi, i, 0)),
        pl.BlockSpec((1, d), lambda bi, i, j: (0, 0)),
        pl.BlockSpec((None, 1, d), lambda bi, i, j: (bi, 0, 0)),
        pl.BlockSpec((None, 1, d), lambda bi, i, j: (bi, 0, 0)),
        pl.BlockSpec((d, tn), lambda bi, i, j: (0, j + n_off)),
    ]
    args = [x, nw.reshape(1, d), shift, scale, w]
    if sections is not None:
        assert nj == 1
        in_specs += [pl.BlockSpec((1, HEAD_DIM), lambda bi, i, j: (0, 0))] * 2
        args += [head_norms[0].reshape(1, HEAD_DIM), head_norms[1].reshape(1, HEAD_DIM)]
    if rope_tabs is not None:
        in_specs += [pl.BlockSpec((tm, HEAD_DIM), lambda bi, i, j: (i, 0))] * 3
        args += list(rope_tabs)
    return pl.pallas_call(
        functools.partial(_proj_kernel, sections=sections, rope=rope_tabs is not None),
        out_shape=jax.ShapeDtypeStruct((b, s, n_out), BF16),
        grid=(b, s // tm, nj),
        in_specs=in_specs,
        out_specs=pl.BlockSpec((None, tm, tn), lambda bi, i, j: (bi, i, j)),
        scratch_shapes=[pltpu.VMEM((tm, d), BF16)],
        compiler_params=_params("arbitrary", "arbitrary", "arbitrary"),
        name=name,
    )(*args)


def _qk(q, k):
    return lax.dot_general(q, k, (((1,), (1,)), ((), ())), preferred_element_type=F32)


EXP2_SCALE = ATTN_SCALE * float(np.log2(np.e))
INV_ATTN_SCALE = 1.0 / ATTN_SCALE


def _softmax_pv(parts, sink=None):
    m = functools.reduce(jnp.maximum, [jnp.max(s, axis=-1, keepdims=True) for s, _ in parts])
    if sink is not None:
        m = jnp.maximum(m, sink)
    l = None
    o = None
    for s, v in parts:
        p = jnp.exp2((s - m) * EXP2_SCALE)
        ls = jnp.sum(p, axis=-1, keepdims=True)
        os_ = jnp.dot(p.astype(BF16), v, preferred_element_type=F32)
        l = ls if l is None else l + ls
        o = os_ if o is None else o + os_
    if sink is not None:
        l = l + jnp.exp2((sink - m) * EXP2_SCALE)
    return o / l


def _run_chains(n_blk, n_inner, scores, probs, weighted_values, unroll=1):
    per_body = unroll * n_inner
    assert per_body % 2 == 0 and n_blk % unroll == 0

    def chain(it, n):
        return it * unroll + n // n_inner, n % n_inner

    scores(0, 0, 0)

    def body(it, carry):
        for n in range(per_body):
            blk, g = chain(it, n + 1)
            scores(jnp.minimum(blk, n_blk - 1) if n + 1 == per_body else blk, g, (n + 1) % 2)
            probs(n % n_inner, n % 2)
            blk, g = chain(it, n - 1)
            weighted_values(jnp.maximum(blk, 0) if n == 0 else blk, g, (n - 1) % 2)
        return carry

    lax.fori_loop(0, n_blk // unroll, body, 0)
    weighted_values(n_blk - 1, n_inner - 1, 1)


GATTN_ROWS = 512


def _gattn_kernel(q_ref, k_ref, v_ref, kc_ref, vc_ref, o_ref, kall_ref, vall_ref, s_ref, p_ref):
    c = kc_ref.shape[0]

    n_blk = q_ref.shape[0] // GATTN_ROWS

    kall_ref[:c, :] = kc_ref[...]
    kall_ref[c:, :] = k_ref[...]
    vall_ref[:c, :HEAD_DIM] = vc_ref[...]
    vall_ref[c:, :HEAD_DIM] = v_ref[...]
    lane = lax.broadcasted_iota(jnp.int32, (vall_ref.shape[0], HEAD_DIM), 1)
    vall_ref[:, HEAD_DIM:] = jnp.where(lane == 0, 1.0, 0.0).astype(BF16)

    def rows_of(blk):
        return pl.ds(pl.multiple_of(blk * GATTN_ROWS, GATTN_ROWS), GATTN_ROWS)

    def cols_of(g):
        return slice(g * HEAD_DIM, (g + 1) * HEAD_DIM)

    def scores(blk, g, slot):
        s_ref[slot] = _qk(q_ref[rows_of(blk), cols_of(g)], kall_ref[...])

    def probs(g, slot):
        m = jnp.max(s_ref[slot], axis=-1, keepdims=True)
        p_ref[slot] = jnp.exp2((s_ref[slot] - m) * EXP2_SCALE).astype(BF16)

    def weighted_values(blk, g, slot):
        ol = jnp.dot(p_ref[slot], vall_ref[...], preferred_element_type=F32)
        o = ol[:, :HEAD_DIM] / ol[:, HEAD_DIM:HEAD_DIM + 1]
        o_ref[rows_of(blk), cols_of(g)] = o.astype(o_ref.dtype)

    p_ref[1] = jnp.ones(p_ref.shape[1:], BF16)
    _run_chains(n_blk, GQA_GROUP, scores, probs, weighted_values)


def _gattn(qkv, qkv_c, *, q_blk, k_blk, v_blk):
    b, s, _ = qkv.shape
    c = qkv_c.shape[1]
    gw = GQA_GROUP * HEAD_DIM
    assert GQA_GROUP % 2 == 0 and s % GATTN_ROWS == 0
    single = pl.Buffered(1)
    return pl.pallas_call(
        _gattn_kernel,
        out_shape=jax.ShapeDtypeStruct((b, s, A_Q_HEADS * HEAD_DIM), BF16),
        grid=(b, A_KV_HEADS),
        in_specs=[
            pl.BlockSpec((None, s, gw), lambda bi, kv: (bi, 0, q_blk + kv), pipeline_mode=single),
            pl.BlockSpec((None, s, HEAD_DIM), lambda bi, kv: (bi, 0, k_blk + kv), pipeline_mode=single),
            pl.BlockSpec((None, s, HEAD_DIM), lambda bi, kv: (bi, 0, v_blk + kv), pipeline_mode=single),
            pl.BlockSpec((None, c, HEAD_DIM), lambda bi, kv: (bi, 0, k_blk + kv)),
            pl.BlockSpec((None, c, HEAD_DIM), lambda bi, kv: (bi, 0, v_blk + kv)),
        ],
        out_specs=pl.BlockSpec((None, s, gw), lambda bi, kv: (bi, 0, kv)),
        scratch_shapes=[
            pltpu.VMEM((c + s, HEAD_DIM), BF16),
            pltpu.VMEM((c + s, 2 * HEAD_DIM), BF16),
            pltpu.VMEM((2, GATTN_ROWS, c + s), F32),
            pltpu.VMEM((2, GATTN_ROWS, c + s), BF16),
        ],
        compiler_params=_params("arbitrary", "arbitrary"),
        name="gattn",
    )(qkv, qkv, qkv, qkv_c, qkv_c)


WATTN_TQ = 4 * WINDOW
WATTN_NK = WATTN_TQ + 2 * WINDOW


def _wattn_kernel(sink_ref, q_ref, k_ref, v_ref, kc_ref, vc_ref, o_ref,
                  mask_ref, s_ref, sc_ref, p_ref, pc_ref, l_ref):
    kv = pl.program_id(1)
    seq = q_ref.shape[0]
    n_tile = seq // WATTN_TQ

    def rows_of(tile):
        return pl.ds(pl.multiple_of(tile * WATTN_TQ, WATTN_TQ), WATTN_TQ)

    def key0_of(tile):
        lo, hi = 0, seq - WATTN_NK
        k0 = tile * WATTN_TQ - WINDOW
        return min(max(k0, lo), hi) if isinstance(tile, int) else jnp.clip(k0, lo, hi)

    def key_rows_of(tile):
        return pl.ds(pl.multiple_of(key0_of(tile), WINDOW), WATTN_NK)

    def cols_of(g):
        return slice(g * HEAD_DIM, (g + 1) * HEAD_DIM)

    for var, tile in enumerate((0, 1, n_tile - 1)):
        shift = key0_of(tile) - tile * WATTN_TQ
        dist = (shift + lax.broadcasted_iota(jnp.int32, (WATTN_TQ, WATTN_NK), 1)
                - lax.broadcasted_iota(jnp.int32, (WATTN_TQ, WATTN_NK), 0))
        mask_ref[var] = jnp.where(jnp.abs(dist) <= WINDOW, 0.0, NEG)

    def scores(tile, g, slot):
        q = q_ref[rows_of(tile), cols_of(g)]
        var = jnp.where(tile == 0, 0, jnp.where(tile == n_tile - 1, 2, 1))
        s_ref[slot] = _qk(q, k_ref[key_rows_of(tile), :]) + mask_ref[var]
        sc_ref[slot] = _qk(q, kc_ref[...])

    def probs(g, slot):
        s, sc = s_ref[slot], sc_ref[slot]
        sink = sink_ref[kv, g] * INV_ATTN_SCALE
        m = jnp.maximum(jnp.max(s, axis=-1, keepdims=True), jnp.max(sc, axis=-1, keepdims=True))
        m = jnp.maximum(m, sink)
        p = jnp.exp2((s - m) * EXP2_SCALE)
        pc = jnp.exp2((sc - m) * EXP2_SCALE)
        l_ref[slot] = (jnp.sum(p, axis=-1, keepdims=True) + jnp.sum(pc, axis=-1, keepdims=True)
                       + jnp.exp2((sink - m) * EXP2_SCALE))
        p_ref[slot] = p.astype(BF16)
        pc_ref[slot] = pc.astype(BF16)

    def weighted_values(tile, g, slot):
        o = (jnp.dot(p_ref[slot], v_ref[key_rows_of(tile), :], preferred_element_type=F32)
             + jnp.dot(pc_ref[slot], vc_ref[...], preferred_element_type=F32))
        o_ref[rows_of(tile), cols_of(g)] = (o / l_ref[slot]).astype(o_ref.dtype)

    p_ref[1] = jnp.ones(p_ref.shape[1:], BF16)
    pc_ref[1] = jnp.ones(pc_ref.shape[1:], BF16)
    l_ref[1] = jnp.ones(l_ref.shape[1:], F32)
    _run_chains(seq // WATTN_TQ, GQA_GROUP, scores, probs, weighted_values)


def _wattn(sink_kg, qkv, qkv_c, *, q_blk, k_blk, v_blk):
    b, s, _ = qkv.shape
    c = qkv_c.shape[1]
    gw = GQA_GROUP * HEAD_DIM
    assert s % WATTN_TQ == 0 and s >= 3 * WATTN_TQ

    def col_map(col):
        return lambda bi, kv: (bi, 0, col + kv)

    return pl.pallas_call(
        _wattn_kernel,
        out_shape=jax.ShapeDtypeStruct((b, s, B_Q_HEADS * HEAD_DIM), BF16),
        grid=(b, B_KV_HEADS),
        in_specs=[
            pl.BlockSpec(memory_space=pltpu.SMEM),
            pl.BlockSpec((None, s, gw), col_map(q_blk)),
            pl.BlockSpec((None, s, HEAD_DIM), col_map(k_blk)),
            pl.BlockSpec((None, s, HEAD_DIM), col_map(v_blk)),
            pl.BlockSpec((None, c, HEAD_DIM), col_map(k_blk)),
            pl.BlockSpec((None, c, HEAD_DIM), col_map(v_blk)),
        ],
        out_specs=pl.BlockSpec((None, s, gw), col_map(0)),
        scratch_shapes=[
            pltpu.VMEM((3, WATTN_TQ, WATTN_NK), F32),
            pltpu.VMEM((2, WATTN_TQ, WATTN_NK), F32),
            pltpu.VMEM((2, WATTN_TQ, c), F32),
            pltpu.VMEM((2, WATTN_TQ, WATTN_NK), BF16),
            pltpu.VMEM((2, WATTN_TQ, c), BF16),
            pltpu.VMEM((2, WATTN_TQ, 1), F32),
        ],
        compiler_params=_params("arbitrary", "arbitrary"),
        name="wattn",
    )(sink_kg, qkv, qkv, qkv, qkv_c, qkv_c)


def _cattn_kernel(sink_ref, q_ref, k_ref, v_ref, o_ref):
    grp = pl.program_id(1)
    k, v = k_ref[...], v_ref[...]
    for g in range(GQA_GROUP):
        q = q_ref[:, g * HEAD_DIM:(g + 1) * HEAD_DIM]
        o = _softmax_pv([(_qk(q, k), v)], sink=sink_ref[grp, g] * INV_ATTN_SCALE)
        o_ref[:, g * HEAD_DIM:(g + 1) * HEAD_DIM] = o.astype(o_ref.dtype)


def _cattn(sink_all, qkv_c):
    b, c, _ = qkv_c.shape
    gw = GQA_GROUP * HEAD_DIM
    n_grp = A_KV_HEADS + B_KV_HEADS
    q_map = lambda bi, g: (bi, 0, g + g // A_KV_HEADS)
    k_map = lambda bi, g: (bi, 0, 8 + g + 10 * (g // A_KV_HEADS))
    v_map = lambda bi, g: (bi, 0, 10 + g + 10 * (g // A_KV_HEADS))
    return pl.pallas_call(
        _cattn_kernel,
        out_shape=jax.ShapeDtypeStruct((b, c, N_HEADS * HEAD_DIM), BF16),
        grid=(b, n_grp),
        in_specs=[
            pl.BlockSpec(memory_space=pltpu.SMEM),
            pl.BlockSpec((None, c, gw), q_map),
            pl.BlockSpec((None, c, HEAD_DIM), k_map),
            pl.BlockSpec((None, c, HEAD_DIM), v_map),
        ],
        out_specs=pl.BlockSpec((None, c, gw), lambda bi, g: (bi, 0, g)),
        compiler_params=_params("arbitrary", "arbitrary"),
        name="cattn",
    )(sink_all, qkv_c, qkv_c, qkv_c)


NATTN_ROWS = 4
NATTN_TQ = NATTN_ROWS * GRID_W
NATTN_KROWS = 3 * NATTN_ROWS


NATTN_HG = 2
NATTN_UNROLL = 2
N_RI = 2 * NA_KH - 1
N_CI = 2 * NA_KW - 1


def _na_key_row0(tile, rows):
    lo, hi = 0, rows - NATTN_KROWS
    r0 = (tile - 1) * NATTN_ROWS
    if isinstance(tile, int):
        return min(max(r0, lo), hi)
    return jnp.clip(r0, lo, hi)


def _nattn_kernel(q_ref, k_ref, v_ref, kc_ref, vc_ref, bias_ref, o_ref,
                  ks_ref, vs_ref, s_ref, sc_ref, p_ref, pc_ref, l_ref):
    seq = q_ref.shape[0]
    n_tile = seq // NATTN_TQ
    nk = NATTN_KROWS * GRID_W

    def rows_of(tile):
        return pl.ds(pl.multiple_of(tile * NATTN_TQ, NATTN_TQ), NATTN_TQ)

    def key_rows_of(tile):
        return pl.ds(pl.multiple_of(_na_key_row0(tile, seq // GRID_W) * GRID_W, NATTN_TQ), nk)

    def cols_of(g):
        return slice(g * HEAD_DIM, (g + 1) * HEAD_DIM)

    for g in range(NATTN_HG):
        ks_ref[g] = k_ref[:, cols_of(g)]
        vs_ref[g] = v_ref[:, cols_of(g)]

    def scores(tile, _, slot):
        var = jnp.where(tile == 0, 0, jnp.where(tile == n_tile - 1, 2, 1))
        for g in range(NATTN_HG):
            q = q_ref[rows_of(tile), cols_of(g)]
            s_ref[slot, g] = _qk(q, ks_ref[g, key_rows_of(tile), :]) + bias_ref[g, var]
            sc_ref[slot, g] = _qk(q, kc_ref[:, cols_of(g)])

    def probs(_, slot):
        for g in range(NATTN_HG):
            s, sc = s_ref[slot, g], sc_ref[slot, g]
            m = jnp.maximum(jnp.max(s, axis=-1, keepdims=True), jnp.max(sc, axis=-1, keepdims=True))
            p = jnp.exp2((s - m) * EXP2_SCALE)
            pc = jnp.exp2((sc - m) * EXP2_SCALE)
            l_ref[slot, g] = jnp.sum(p, axis=-1, keepdims=True) + jnp.sum(pc, axis=-1, keepdims=True)
            p_ref[slot, g] = p.astype(BF16)
            pc_ref[slot, g] = pc.astype(BF16)

    def weighted_values(tile, _, slot):
        for g in range(NATTN_HG):
            o = (jnp.dot(p_ref[slot, g], vs_ref[g, key_rows_of(tile), :], preferred_element_type=F32)
                 + jnp.dot(pc_ref[slot, g], vc_ref[:, cols_of(g)], preferred_element_type=F32))
            o_ref[rows_of(tile), cols_of(g)] = (o / l_ref[slot, g]).astype(o_ref.dtype)

    p_ref[1] = jnp.ones(p_ref.shape[1:], BF16)
    pc_ref[1] = jnp.ones(pc_ref.shape[1:], BF16)
    l_ref[1] = jnp.ones(l_ref.shape[1:], F32)
    _run_chains(n_tile, 1, scores, probs, weighted_values, unroll=NATTN_UNROLL)


def _na_bias_kernel(rpb_ref, o_ref, u_ref, *, rows):
    base = pl.program_id(0) * (N_RI * N_CI)
    kh = min(NA_KH, rows)

    def fill(sg, carry):
        q0 = pl.multiple_of(sg * SUBLANES, SUBLANES)
        qc = q0 + lax.broadcasted_iota(jnp.int32, (SUBLANES, LANES), 0)
        kc = lax.broadcasted_iota(jnp.int32, (SUBLANES, LANES), 1) & (GRID_W - 1)
        diag = kc - qc + (NA_KW - 1)
        us = [jnp.zeros((SUBLANES, LANES), F32) for _ in range(N_RI)]
        for ci in range(N_CI):
            hit = diag == ci
            for a in range(N_RI):
                us[a] = jnp.where(hit, rpb_ref[base + a * N_CI + ci], us[a])
        for a in range(N_RI):
            u_ref[a, pl.ds(q0, SUBLANES), :] = us[a] * INV_ATTN_SCALE
        return carry

    lax.fori_loop(0, GRID_W // SUBLANES, fill, 0)

    qc = lax.broadcasted_iota(jnp.int32, (GRID_W, LANES), 0)
    lane = lax.broadcasted_iota(jnp.int32, (GRID_W, LANES), 1)
    kc = lane & (GRID_W - 1)
    cs = jnp.clip(qc - NA_KW // 2, 0, GRID_W - NA_KW)
    col_valid = (kc >= cs) & (kc < cs + NA_KW)
    low = lane < GRID_W
    masks = {(True, True): col_valid, (True, False): col_valid & low,
             (False, True): col_valid & jnp.logical_not(low)}
    neg_blk = jnp.full((GRID_W, LANES), NEG, F32)
    n_tile = rows // NATTN_ROWS
    for var, tile in enumerate((0, 1, n_tile - 1)):
        r0 = tile * NATTN_ROWS
        k0 = _na_key_row0(tile, rows)
        for qr in range(NATTN_ROWS):
            rs = min(max(r0 + qr - kh // 2, 0), rows - kh)
            for jj in range(NATTN_KROWS // 2):
                kr = k0 + 2 * jj
                ok = (rs <= kr < rs + kh, rs <= kr + 1 < rs + kh)
                if ok == (False, False):
                    blk = neg_blk
                else:
                    a = kr - (r0 + qr) + NA_KH - 1
                    a_lo = min(max(a, 0), N_RI - 1)
                    a_hi = min(max(a + 1, 0), N_RI - 1)
                    pair = jnp.where(low, u_ref[a_lo], u_ref[a_hi])
                    blk = jnp.where(masks[ok], pair, NEG)
                o_ref[var, qr * GRID_W:(qr + 1) * GRID_W, jj * LANES:(jj + 1) * LANES] = blk


def _na_bias_tables(rpb, rows):
    h = rpb.shape[0]
    assert 2 * GRID_W == LANES and rows >= 3 * NATTN_ROWS and NATTN_ROWS == NA_KH // 2
    return pl.pallas_call(
        functools.partial(_na_bias_kernel, rows=rows),
        out_shape=jax.ShapeDtypeStruct((h, 3, NATTN_TQ, NATTN_KROWS * GRID_W), F32),
        grid=(h,),
        in_specs=[pl.BlockSpec(memory_space=pltpu.SMEM)],
        out_specs=pl.BlockSpec((None, 3, NATTN_TQ, NATTN_KROWS * GRID_W), lambda hh: (hh, 0, 0, 0)),
        scratch_shapes=[pltpu.VMEM((N_RI, GRID_W, LANES), F32)],
        compiler_params=_params("arbitrary"),
        name="na_bias",
    )(rpb.reshape(-1))


def _nattn(qkv, kv_c, bias):
    b, s, _ = qkv.shape
    c = kv_c.shape[1]
    h = C_HEADS
    ng = h // NATTN_HG
    gw = NATTN_HG * HEAD_DIM
    nk = NATTN_KROWS * GRID_W
    assert NATTN_HG % 2 == 0 and s % NATTN_TQ == 0

    def col_map(col):
        return lambda hg, bi: (bi, 0, col + hg)

    return pl.pallas_call(
        _nattn_kernel,
        out_shape=jax.ShapeDtypeStruct((b, s, h * HEAD_DIM), BF16),
        grid=(ng, b),
        in_specs=[
            pl.BlockSpec((None, s, gw), col_map(0)),
            pl.BlockSpec((None, s, gw), col_map(ng)),
            pl.BlockSpec((None, s, gw), col_map(2 * ng)),
            pl.BlockSpec((None, c, gw), col_map(0)),
            pl.BlockSpec((None, c, gw), col_map(ng)),
            pl.BlockSpec((NATTN_HG, 3, NATTN_TQ, nk), lambda hg, bi: (hg, 0, 0, 0),
                         pipeline_mode=pl.Buffered(1)),
        ],
        out_specs=pl.BlockSpec((None, s, gw), col_map(0)),
        scratch_shapes=[
            pltpu.VMEM((NATTN_HG, s, HEAD_DIM), BF16),
            pltpu.VMEM((NATTN_HG, s, HEAD_DIM), BF16),
            pltpu.VMEM((2, NATTN_HG, NATTN_TQ, nk), F32),
            pltpu.VMEM((2, NATTN_HG, NATTN_TQ, c), F32),
            pltpu.VMEM((2, NATTN_HG, NATTN_TQ, nk), BF16),
            pltpu.VMEM((2, NATTN_HG, NATTN_TQ, c), BF16),
            pltpu.VMEM((2, NATTN_HG, NATTN_TQ, 1), F32),
        ],
        compiler_params=_params("arbitrary", "arbitrary"),
        name="nattn",
    )(qkv, qkv, qkv, kv_c, kv_c, bias)


def _outproj_kernel(*refs, widths):
    x_ref, w_ref, g_ref = refs[:3]
    o_refs = refs[3:3 + len(widths)]
    y_ref = refs[3 + len(widths)]
    y = None
    k0 = 0
    for o_ref, kw in zip(o_refs, widths):
        part = jnp.dot(o_ref[...], w_ref[k0:k0 + kw, :], preferred_element_type=F32)
        y = part if y is None else y + part
        k0 += kw
    y_ref[...] = x_ref[...] + g_ref[...] * y


def _outproj(x, o_parts, w, gate, *, tm):
    b, s, d = x.shape
    widths = tuple(o.shape[2] for o in o_parts)
    assert sum(widths) == w.shape[0]
    return pl.pallas_call(
        functools.partial(_outproj_kernel, widths=widths),
        out_shape=jax.ShapeDtypeStruct(x.shape, x.dtype),
        grid=(b, s // tm),
        in_specs=[
            pl.BlockSpec((None, tm, d), lambda bi, i: (bi, i, 0)),
            pl.BlockSpec(w.shape, lambda bi, i: (0, 0)),
            pl.BlockSpec((None, 1, d), lambda bi, i: (bi, 0, 0)),
        ] + [pl.BlockSpec((None, tm, kw), lambda bi, i: (bi, i, 0)) for kw in widths],
        out_specs=pl.BlockSpec((None, tm, d), lambda bi, i: (bi, i, 0)),
        compiler_params=_params("arbitrary", "arbitrary"),
        name="outproj",
    )(x, w, gate, *o_parts)


def _mlp_kernel(*refs, final_norm):
    x_ref, nw_ref, sh_ref, sc_ref, g_ref, w1_ref, w2_ref = refs[:7]
    rest = refs[7:]
    if final_norm:
        fw_ref = rest[0]
        rest = rest[1:]
    y_ref, h_ref, acc_ref = rest
    f = pl.program_id(2)

    @pl.when(f == 0)
    def _():
        h = _norm_modulate(x_ref[...], nw_ref[...], sh_ref[...], sc_ref[...])
        h_ref[...] = h.astype(BF16)
        acc_ref[...] = jnp.zeros_like(acc_ref)

    a = jnp.dot(h_ref[...], w1_ref[...], preferred_element_type=F32)
    a = jnp.square(jnp.maximum(a, 0.0)).astype(BF16)
    acc_ref[...] += jnp.dot(a, w2_ref[...], preferred_element_type=F32)

    @pl.when(f == pl.num_programs(2) - 1)
    def _():
        y = x_ref[...] + g_ref[...] * acc_ref[...]
        if final_norm:
            y = y * lax.rsqrt(jnp.mean(y * y, axis=-1, keepdims=True) + NORM_EPS) * fw_ref[...]
        y_ref[...] = y


def _mlp(x, nw, shift, scale, gate, w1, w2, *, tm, tf, final_w=None):
    b, s, d = x.shape
    dff = w1.shape[1]
    vec = pl.BlockSpec((None, 1, d), lambda bi, i, f: (bi, 0, 0))
    in_specs = [
        pl.BlockSpec((None, tm, d), lambda bi, i, f: (bi, i, 0)),
        pl.BlockSpec((1, d), lambda bi, i, f: (0, 0)),
        vec, vec, vec,
        pl.BlockSpec((d, tf), lambda bi, i, f: (0, f)),
        pl.BlockSpec((tf, d), lambda bi, i, f: (f, 0)),
    ]
    args = [x, nw.reshape(1, d), shift, scale, gate, w1, w2]
    if final_w is not None:
        in_specs.append(pl.BlockSpec((1, d), lambda bi, i, f: (0, 0)))
        args.append(final_w.reshape(1, d))
    return pl.pallas_call(
        functools.partial(_mlp_kernel, final_norm=final_w is not None),
        out_shape=jax.ShapeDtypeStruct(x.shape, x.dtype),
        grid=(b, s // tm, dff // tf),
        in_specs=in_specs,
        out_specs=pl.BlockSpec((None, tm, d), lambda bi, i, f: (bi, i, 0)),
        scratch_shapes=[pltpu.VMEM((tm, d), BF16), pltpu.VMEM((tm, d), F32)],
        compiler_params=_params("arbitrary", "arbitrary", "arbitrary"),
        name="mlp",
    )(*args)


EVEN_SECTIONS = (
    (0, A_Q_HEADS, "q_norm"),
    (A_Q_HEADS * HEAD_DIM, A_KV_HEADS, "k_norm"),
    ((A_Q_HEADS + A_KV_HEADS) * HEAD_DIM, A_KV_HEADS, "v"),
    ((A_Q_HEADS + 2 * A_KV_HEADS) * HEAD_DIM, B_Q_HEADS, "q"),
    ((A_Q_HEADS + 2 * A_KV_HEADS + B_Q_HEADS) * HEAD_DIM, B_KV_HEADS, "k"),
    ((A_Q_HEADS + 2 * A_KV_HEADS + B_Q_HEADS + B_KV_HEADS) * HEAD_DIM, B_KV_HEADS, "v"),
)
LAT_TM = 512


def kernel(x, c, ctx, c_ctx, ada_w, ada_b, norm_w, mlp_w1, mlp_w2, ev_w_in, ev_w_out, ev_q_norm,
           ev_k_norm, ev_sink, od_w_in, od_w_out, od_rpb, final_norm_w):
    b, s, d = x.shape
    n_ctx = ctx.shape[1]
    depth = ada_w.shape[0]
    rows = s // GRID_W

    mod = _ada_mod(jnp.concatenate([c, c_ctx[None, :]], axis=0), ada_w, ada_b)
    rope_tabs = _rope_tables(s)

    def mod_vecs(layer):
        lat = [mod[layer, :b, k * d:(k + 1) * d][:, None, :] for k in range(6)]
        cx = [jnp.broadcast_to(mod[layer, b:b + 1, k * d:(k + 1) * d][:, None, :], (b, 1, d))
              for k in range(6)]
        return lat, cx

    for layer in range(depth):
        need_ctx = layer < depth - 1
        (sh1, sc1, g1, sh2, sc2, g2), (sh1c, sc1c, g1c, sh2c, sc2c, g2c) = mod_vecs(layer)
        j = layer // 2
        w1 = mlp_w1[layer].astype(BF16)
        w2 = mlp_w2[layer].astype(BF16)
        if layer % 2 == 0:
            w_in = ev_w_in[j].astype(BF16)
            w_out = ev_w_out[j].astype(BF16)
            n_in = w_in.shape[1]
            norms = (ev_q_norm[j], ev_k_norm[j])
            qkv = _proj(x, norm_w[layer, 0], sh1, sc1, w_in, tm=LAT_TM, tn=n_in,
                        sections=EVEN_SECTIONS, head_norms=norms, rope_tabs=rope_tabs, name="proj_even")
            qkv_c = _proj(ctx, norm_w[layer, 0], sh1c, sc1c, w_in, tm=n_ctx, tn=n_in,
                          sections=EVEN_SECTIONS, head_norms=norms, name="proj_even_ctx")
            oa = _gattn(qkv, qkv_c, q_blk=0, k_blk=A_Q_HEADS, v_blk=A_Q_HEADS + A_KV_HEADS)
            sink_kg = ev_sink[j].reshape(B_KV_HEADS, GQA_GROUP).astype(F32)
            qb_blk = (A_Q_HEADS + 2 * A_KV_HEADS) // GQA_GROUP
            kb_blk = A_Q_HEADS + 2 * A_KV_HEADS + B_Q_HEADS
            ob = _wattn(sink_kg, qkv, qkv_c, q_blk=qb_blk, k_blk=kb_blk, v_blk=kb_blk + B_KV_HEADS)
            o_parts = [oa, ob]
            if need_ctx:
                sink_all = jnp.concatenate([jnp.full((A_KV_HEADS, GQA_GROUP), NEG, F32), sink_kg], axis=0)
                oc_parts = [_cattn(sink_all, qkv_c)]
        else:
            w_in = od_w_in[j].astype(BF16)
            w_out = od_w_out[j].astype(BF16)
            hd = C_HEADS * HEAD_DIM
            qkv = _proj(x, norm_w[layer, 0], sh1, sc1, w_in, tm=LAT_TM, tn=hd, name="proj_odd")
            kv_c = _proj(ctx, norm_w[layer, 0], sh1c, sc1c, w_in, tm=n_ctx, tn=hd, n_off=1,
                         n_out=2 * hd, name="proj_odd_ctx")
            o_parts = [_nattn(qkv, kv_c, _na_bias_tables(od_rpb[j], rows))]
            assert not need_ctx, "the odd layer is the last one at this depth"
        x = _outproj(x, o_parts, w_out, g1, tm=LAT_TM)
        x = _mlp(x, norm_w[layer, 1], sh2, sc2, g2, w1, w2, tm=LAT_TM, tf=1024,
                 final_w=None if need_ctx else final_norm_w)
        if need_ctx:
            ctx = _outproj(ctx, oc_parts, w_out, g1c, tm=n_ctx)
            ctx = _mlp(ctx, norm_w[layer, 1], sh2c, sc2c, g2c, w1, w2, tm=n_ctx, tf=1024)
    return x
```

```python
import functools

import jax
import jax.numpy as jnp
import numpy as np
from jax import lax
from jax.experimental import pallas as pl
from jax.experimental.pallas import tpu as pltpu

D_MODEL = 2048
GRID_W = 64
HEAD_DIM = 128
N_HEADS = D_MODEL // HEAD_DIM
A_Q_HEADS = N_HEADS // 2
A_KV_HEADS = max(1, A_Q_HEADS // 4)
B_Q_HEADS = N_HEADS - A_Q_HEADS
B_KV_HEADS = max(1, B_Q_HEADS // 4)
GQA_GROUP = A_Q_HEADS // A_KV_HEADS
C_HEADS = N_HEADS
WINDOW = 128
NA_KH = 8
NA_KW = 16
D_FF = 4 * D_MODEL
ROPE_THETA = 10000.0
ROPE_PAIRS = HEAD_DIM // 4
NORM_EPS = 1e-6
NEG = -1e30
ATTN_SCALE = HEAD_DIM ** -0.5

V7X_VMEM_BYTES = 64 * 1024 * 1024
VMEM_LIMIT = V7X_VMEM_BYTES - 6 * 1024 * 1024
LANES = 128
SUBLANES = 8

F32 = jnp.float32
BF16 = jnp.bfloat16


def _params(*sem):
    return pltpu.CompilerParams(dimension_semantics=sem, vmem_limit_bytes=VMEM_LIMIT)


ADA_ROWS = 3
ADA_TN = 1024


def _ada_kernel(cb_ref, w_ref, b_ref, o_ref, s_ref):
    @pl.when((pl.program_id(0) == 0) & (pl.program_id(1) == 0))
    def _():
        cvals = cb_ref[...]
        s_ref[...] = cvals * (1.0 / (1.0 + jnp.exp(-cvals)))

    nchunk = ADA_TN // LANES

    def body(kk, accs):
        k0 = pl.multiple_of(kk * SUBLANES, SUBLANES)
        w = w_ref[pl.ds(k0, SUBLANES), :]
        out = []
        for r in range(ADA_ROWS):
            s = s_ref[r, pl.ds(k0, SUBLANES), :]
            for cch in range(nchunk):
                out.append(accs[r * nchunk + cch] + w[:, cch * LANES:(cch + 1) * LANES] * s)
        return tuple(out)

    zeros = tuple(jnp.zeros((SUBLANES, LANES), F32) for _ in range(ADA_ROWS * nchunk))
    accs = lax.fori_loop(0, D_MODEL // SUBLANES, body, zeros, unroll=2)
    o_ref[...] = jnp.broadcast_to(b_ref[...], o_ref.shape)
    for r in range(ADA_ROWS):
        row = jnp.concatenate(
            [jnp.sum(accs[r * nchunk + cch], axis=0, keepdims=True) for cch in range(nchunk)], axis=1)
        o_ref[r:r + 1, :] = row + b_ref[...]


def _ada_mod(cc, ada_w, ada_b):
    depth, d, n = ada_w.shape
    cb = jnp.broadcast_to(cc[:, :, None], (ADA_ROWS, d, LANES))
    return pl.pallas_call(
        _ada_kernel,
        out_shape=jax.ShapeDtypeStruct((depth, SUBLANES, n), F32),
        grid=(depth, n // ADA_TN),
        in_specs=[
            pl.BlockSpec((ADA_ROWS, d, LANES), lambda l, j: (0, 0, 0)),
            pl.BlockSpec((None, d, ADA_TN), lambda l, j: (l, 0, j)),
            pl.BlockSpec((None, 1, ADA_TN), lambda l, j: (l, 0, j)),
        ],
        out_specs=pl.BlockSpec((None, SUBLANES, ADA_TN), lambda l, j: (l, 0, j)),
        scratch_shapes=[pltpu.VMEM((ADA_ROWS, d, LANES), F32)],
        compiler_params=_params("arbitrary", "arbitrary"),
        name="ada_mod",
    )(cb, ada_w, ada_b.reshape(depth, 1, n))


def _norm_modulate(x, nw, shift, scale):
    ms = jnp.mean(x * x, axis=-1, keepdims=True)
    y = x * lax.rsqrt(ms + NORM_EPS) * nw
    return y * (1.0 + scale) + shift


def _rope(y, cos, sin_lo, sin_hi):
    return (y * cos + pltpu.roll(y, HEAD_DIM - ROPE_PAIRS, 1) * sin_lo
            + pltpu.roll(y, ROPE_PAIRS, 1) * sin_hi)


def _rope_tables(seq):
    t = jnp.arange(seq)
    row = (t // GRID_W).astype(F32)
    col = (t % GRID_W).astype(F32)
    inv = ROPE_THETA ** (-jnp.arange(ROPE_PAIRS, dtype=F32) / ROPE_PAIRS)
    ang_r = row[:, None] * inv
    ang_c = col[:, None] * inv
    ang = jnp.concatenate([ang_r, ang_r, ang_c, ang_c], axis=-1)
    cos, sin = jnp.cos(ang), jnp.sin(ang)
    first = (jnp.arange(HEAD_DIM) % (2 * ROPE_PAIRS)) < ROPE_PAIRS
    sin_lo = jnp.where(first, -sin, 0.0)
    sin_hi = jnp.where(first, 0.0, sin)
    return cos, sin_lo, sin_hi


def _proj_kernel(*refs, sections, rope):
    x_ref, nw_ref, sh_ref, sc_ref, w_ref = refs[:5]
    rest = refs[5:]
    if sections is not None:
        qn_ref, kn_ref = rest[:2]
        rest = rest[2:]
    if rope:
        cos_ref, slo_ref, shi_ref = rest[:3]
        rest = rest[3:]
    o_ref, h_ref = rest

    @pl.when(pl.program_id(2) == 0)
    def _():
        h = _norm_modulate(x_ref[...], nw_ref[...], sh_ref[...], sc_ref[...])
        h_ref[...] = h.astype(BF16)

    if sections is None:
        o_ref[...] = jnp.dot(h_ref[...], w_ref[...], preferred_element_type=F32).astype(o_ref.dtype)
        return

    for c0, nh, kind in sections:
        y = jnp.dot(h_ref[...], w_ref[:, c0:c0 + nh * HEAD_DIM], preferred_element_type=F32)
        for hh in range(nh):
            yh = y[:, hh * HEAD_DIM:(hh + 1) * HEAD_DIM]
            if kind in ("q_norm", "k_norm"):
                hw = (qn_ref if kind == "q_norm" else kn_ref)[...]
                yh = yh * lax.rsqrt(jnp.mean(yh * yh, axis=-1, keepdims=True) + NORM_EPS) * hw
            if rope and kind != "v":
                yh = _rope(yh, cos_ref[...], slo_ref[...], shi_ref[...])
            lo = c0 + hh * HEAD_DIM
            o_ref[:, lo:lo + HEAD_DIM] = yh.astype(o_ref.dtype)


def _proj(x, nw, shift, scale, w, *, tm, tn, n_off=0, n_out=None, sections=None,
          head_norms=None, rope_tabs=None, name="proj"):
    b, s, d = x.shape
    n_out = w.shape[1] if n_out is None else n_out
    nj = n_out // tn
    in_specs = [
        pl.BlockSpec((None, tm, d), lambda bi, i, j: (bi, i, 0)),
        pl.BlockSpec((1, d), lambda bi, i, j: (0, 0)),
        pl.BlockSpec((None, 1, d), lambda bi, i, j: (bi, 0, 0)),
        pl.BlockSpec((None, 1, d), lambda bi, i, j: (bi, 0, 0)),
        pl.BlockSpec((d, tn), lambda bi, i, j: (0, j + n_off)),
    ]
    args = [x, nw.reshape(1, d), shift, scale, w]
    if sections is not None:
        assert nj == 1
        in_specs += [pl.BlockSpec((1, HEAD_DIM), lambda bi, i, j: (0, 0))] * 2
        args += [head_norms[0].reshape(1, HEAD_DIM), head_norms[1].reshape(1, HEAD_DIM)]
    if rope_tabs is not None:
        in_specs += [pl.BlockSpec((tm, HEAD_DIM), lambda bi, i, j: (i, 0))] * 3
        args += list(rope_tabs)
    return pl.pallas_call(
        functools.partial(_proj_kernel, sections=sections, rope=rope_tabs is not None),
        out_shape=jax.ShapeDtypeStruct((b, s, n_out), BF16),
        grid=(b, s // tm, nj),
        in_specs=in_specs,
        out_specs=pl.BlockSpec((None, tm, tn), lambda bi, i, j: (bi, i, j)),
        scratch_shapes=[pltpu.VMEM((tm, d), BF16)],
        compiler_params=_params("arbitrary", "arbitrary", "arbitrary"),
        name=name,
    )(*args)


def _qk(q, k):
    return lax.dot_general(q, k, (((1,), (1,)), ((), ())), preferred_element_type=F32)


EXP2_SCALE = ATTN_SCALE * float(np.log2(np.e))
INV_ATTN_SCALE = 1.0 / ATTN_SCALE


def _softmax_pv(parts, sink=None):
    m = functools.reduce(jnp.maximum, [jnp.max(s, axis=-1, keepdims=True) for s, _ in parts])
    if sink is not None:
        m = jnp.maximum(m, sink)
    l = None
    o = None
    for s, v in parts:
        p = jnp.exp2((s - m) * EXP2_SCALE)
        ls = jnp.sum(p, axis=-1, keepdims=True)
        os_ = jnp.dot(p.astype(BF16), v, preferred_element_type=F32)
        l = ls if l is None else l + ls
        o = os_ if o is None else o + os_
    if sink is not None:
        l = l + jnp.exp2((sink - m) * EXP2_SCALE)
    return o / l


def _run_chains(n_blk, n_inner, scores, probs, weighted_values, unroll=1):
    per_body = unroll * n_inner
    assert per_body % 2 == 0 and n_blk % unroll == 0

    def chain(it, n):
        return it * unroll + n // n_inner, n % n_inner

    scores(0, 0, 0)

    def body(it, carry):
        for n in range(per_body):
            blk, g = chain(it, n + 1)
            scores(jnp.minimum(blk, n_blk - 1) if n + 1 == per_body else blk, g, (n + 1) % 2)
            probs(n % n_inner, n % 2)
            blk, g = chain(it, n - 1)
            weighted_values(jnp.maximum(blk, 0) if n == 0 else blk, g, (n - 1) % 2)
        return carry

    lax.fori_loop(0, n_blk // unroll, body, 0)
    weighted_values(n_blk - 1, n_inner - 1, 1)


GATTN_ROWS = 512


def _gattn_kernel(q_ref, k_ref, v_ref, kc_ref, vc_ref, o_ref, kall_ref, vall_ref, s_ref, p_ref):
    c = kc_ref.shape[0]

    n_blk = q_ref.shape[0] // GATTN_ROWS

    kall_ref[:c, :] = kc_ref[...]
    kall_ref[c:, :] = k_ref[...]
    vall_ref[:c, :HEAD_DIM] = vc_ref[...]
    vall_ref[c:, :HEAD_DIM] = v_ref[...]
    lane = lax.broadcasted_iota(jnp.int32, (vall_ref.shape[0], HEAD_DIM), 1)
    vall_ref[:, HEAD_DIM:] = jnp.where(lane == 0, 1.0, 0.0).astype(BF16)

    def rows_of(blk):
        return pl.ds(pl.multiple_of(blk * GATTN_ROWS, GATTN_ROWS), GATTN_ROWS)

    def cols_of(g):
        return slice(g * HEAD_DIM, (g + 1) * HEAD_DIM)

    def scores(blk, g, slot):
        s_ref[slot] = _qk(q_ref[rows_of(blk), cols_of(g)], kall_ref[...])

    def probs(g, slot):
        m = jnp.max(s_ref[slot], axis=-1, keepdims=True)
        p_ref[slot] = jnp.exp2((s_ref[slot] - m) * EXP2_SCALE).astype(BF16)

    def weighted_values(blk, g, slot):
        ol = jnp.dot(p_ref[slot], vall_ref[...], preferred_element_type=F32)
        o = ol[:, :HEAD_DIM] / ol[:, HEAD_DIM:HEAD_DIM + 1]
        o_ref[rows_of(blk), cols_of(g)] = o.astype(o_ref.dtype)

    p_ref[1] = jnp.ones(p_ref.shape[1:], BF16)
    _run_chains(n_blk, GQA_GROUP, scores, probs, weighted_values)


def _gattn(qkv, qkv_c, *, q_blk, k_blk, v_blk):
    b, s, _ = qkv.shape
    c = qkv_c.shape[1]
    gw = GQA_GROUP * HEAD_DIM
    assert GQA_GROUP % 2 == 0 and s % GATTN_ROWS == 0
    single = pl.Buffered(1)
    return pl.pallas_call(
        _gattn_kernel,
        out_shape=jax.ShapeDtypeStruct((b, s, A_Q_HEADS * HEAD_DIM), BF16),
        grid=(b, A_KV_HEADS),
        in_specs=[
            pl.BlockSpec((None, s, gw), lambda bi, kv: (bi, 0, q_blk + kv), pipeline_mode=single),
            pl.BlockSpec((None, s, HEAD_DIM), lambda bi, kv: (bi, 0, k_blk + kv), pipeline_mode=single),
            pl.BlockSpec((None, s, HEAD_DIM), lambda bi, kv: (bi, 0, v_blk + kv), pipeline_mode=single),
            pl.BlockSpec((None, c, HEAD_DIM), lambda bi, kv: (bi, 0, k_blk + kv)),
            pl.BlockSpec((None, c, HEAD_DIM), lambda bi, kv: (bi, 0, v_blk + kv)),
        ],
        out_specs=pl.BlockSpec((None, s, gw), lambda bi, kv: (bi, 0, kv)),
        scratch_shapes=[
            pltpu.VMEM((c + s, HEAD_DIM), BF16),
            pltpu.VMEM((c + s, 2 * HEAD_DIM), BF16),
            pltpu.VMEM((2, GATTN_ROWS, c + s), F32),
            pltpu.VMEM((2, GATTN_ROWS, c + s), BF16),
        ],
        compiler_params=_params("arbitrary", "arbitrary"),
        name="gattn",
    )(qkv, qkv, qkv, qkv_c, qkv_c)


WATTN_TQ = 4 * WINDOW
WATTN_NK = WATTN_TQ + 2 * WINDOW


def _wattn_kernel(sink_ref, q_ref, k_ref, v_ref, kc_ref, vc_ref, o_ref,
                  mask_ref, s_ref, sc_ref, p_ref, pc_ref, l_ref):
    kv = pl.program_id(1)
    seq = q_ref.shape[0]
    n_tile = seq // WATTN_TQ

    def rows_of(tile):
        return pl.ds(pl.multiple_of(tile * WATTN_TQ, WATTN_TQ), WATTN_TQ)

    def key0_of(tile):
        lo, hi = 0, seq - WATTN_NK
        k0 = tile * WATTN_TQ - WINDOW
        return min(max(k0, lo), hi) if isinstance(tile, int) else jnp.clip(k0, lo, hi)

    def key_rows_of(tile):
        return pl.ds(pl.multiple_of(key0_of(tile), WINDOW), WATTN_NK)

    def cols_of(g):
        return slice(g * HEAD_DIM, (g + 1) * HEAD_DIM)

    for var, tile in enumerate((0, 1, n_tile - 1)):
        shift = key0_of(tile) - tile * WATTN_TQ
        dist = (shift + lax.broadcasted_iota(jnp.int32, (WATTN_TQ, WATTN_NK), 1)
                - lax.broadcasted_iota(jnp.int32, (WATTN_TQ, WATTN_NK), 0))
        mask_ref[var] = jnp.where(jnp.abs(dist) <= WINDOW, 0.0, NEG)

    def scores(tile, g, slot):
        q = q_ref[rows_of(tile), cols_of(g)]
        var = jnp.where(tile == 0, 0, jnp.where(tile == n_tile - 1, 2, 1))
        s_ref[slot] = _qk(q, k_ref[key_rows_of(tile), :]) + mask_ref[var]
        sc_ref[slot] = _qk(q, kc_ref[...])

    def probs(g, slot):
        s, sc = s_ref[slot], sc_ref[slot]
        sink = sink_ref[kv, g] * INV_ATTN_SCALE
        m = jnp.maximum(jnp.max(s, axis=-1, keepdims=True), jnp.max(sc, axis=-1, keepdims=True))
        m = jnp.maximum(m, sink)
        p = jnp.exp2((s - m) * EXP2_SCALE)
        pc = jnp.exp2((sc - m) * EXP2_SCALE)
        l_ref[slot] = (jnp.sum(p, axis=-1, keepdims=True) + jnp.sum(pc, axis=-1, keepdims=True)
                       + jnp.exp2((sink - m) * EXP2_SCALE))
        p_ref[slot] = p.astype(BF16)
        pc_ref[slot] = pc.astype(BF16)

    def weighted_values(tile, g, slot):
        o = (jnp.dot(p_ref[slot], v_ref[key_rows_of(tile), :], preferred_element_type=F32)
             + jnp.dot(pc_ref[slot], vc_ref[...], preferred_element_type=F32))
        o_ref[rows_of(tile), cols_of(g)] = (o / l_ref[slot]).astype(o_ref.dtype)

    p_ref[1] = jnp.ones(p_ref.shape[1:], BF16)
    pc_ref[1] = jnp.ones(pc_ref.shape[1:], BF16)
    l_ref[1] = jnp.ones(l_ref.shape[1:], F32)
    _run_chains(seq // WATTN_TQ, GQA_GROUP, scores, probs, weighted_values)


def _wattn(sink_kg, qkv, qkv_c, *, q_blk, k_blk, v_blk):
    b, s, _ = qkv.shape
    c = qkv_c.shape[1]
    gw = GQA_GROUP * HEAD_DIM
    assert s % WATTN_TQ == 0 and s >= 3 * WATTN_TQ

    def col_map(col):
        return lambda bi, kv: (bi, 0, col + kv)

    return pl.pallas_call(
        _wattn_kernel,
        out_shape=jax.ShapeDtypeStruct((b, s, B_Q_HEADS * HEAD_DIM), BF16),
        grid=(b, B_KV_HEADS),
        in_specs=[
            pl.BlockSpec(memory_space=pltpu.SMEM),
            pl.BlockSpec((None, s, gw), col_map(q_blk)),
            pl.BlockSpec((None, s, HEAD_DIM), col_map(k_blk)),
            pl.BlockSpec((None, s, HEAD_DIM), col_map(v_blk)),
            pl.BlockSpec((None, c, HEAD_DIM), col_map(k_blk)),
            pl.BlockSpec((None, c, HEAD_DIM), col_map(v_blk)),
        ],
        out_specs=pl.BlockSpec((None, s, gw), col_map(0)),
        scratch_shapes=[
            pltpu.VMEM((3, WATTN_TQ, WATTN_NK), F32),
            pltpu.VMEM((2, WATTN_TQ, WATTN_NK), F32),
            pltpu.VMEM((2, WATTN_TQ, c), F32),
            pltpu.VMEM((2, WATTN_TQ, WATTN_NK), BF16),
            pltpu.VMEM((2, WATTN_TQ, c), BF16),
            pltpu.VMEM((2, WATTN_TQ, 1), F32),
        ],
        compiler_params=_params("arbitrary", "arbitrary"),
        name="wattn",
    )(sink_kg, qkv, qkv, qkv, qkv_c, qkv_c)


def _cattn_kernel(sink_ref, q_ref, k_ref, v_ref, o_ref):
    grp = pl.program_id(1)
    k, v = k_ref[...], v_ref[...]
    for g in range(GQA_GROUP):
        q = q_ref[:, g * HEAD_DIM:(g + 1) * HEAD_DIM]
        o = _softmax_pv([(_qk(q, k), v)], sink=sink_ref[grp, g] * INV_ATTN_SCALE)
        o_ref[:, g * HEAD_DIM:(g + 1) * HEAD_DIM] = o.astype(o_ref.dtype)


def _cattn(sink_all, qkv_c):
    b, c, _ = qkv_c.shape
    gw = GQA_GROUP * HEAD_DIM
    n_grp = A_KV_HEADS + B_KV_HEADS
    q_map = lambda bi, g: (bi, 0, g + g // A_KV_HEADS)
    k_map = lambda bi, g: (bi, 0, 8 + g + 10 * (g // A_KV_HEADS))
    v_map = lambda bi, g: (bi, 0, 10 + g + 10 * (g // A_KV_HEADS))
    return pl.pallas_call(
        _cattn_kernel,
        out_shape=jax.ShapeDtypeStruct((b, c, N_HEADS * HEAD_DIM), BF16),
        grid=(b, n_grp),
        in_specs=[
            pl.BlockSpec(memory_space=pltpu.SMEM),
            pl.BlockSpec((None, c, gw), q_map),
            pl.BlockSpec((None, c, HEAD_DIM), k_map),
            pl.BlockSpec((None, c, HEAD_DIM), v_map),
        ],
        out_specs=pl.BlockSpec((None, c, gw), lambda bi, g: (bi, 0, g)),
        compiler_params=_params("arbitrary", "arbitrary"),
        name="cattn",
    )(sink_all, qkv_c, qkv_c, qkv_c)


NATTN_ROWS = 4
NATTN_TQ = NATTN_ROWS * GRID_W
NATTN_KROWS = 3 * NATTN_ROWS


NATTN_HG = 2
NATTN_UNROLL = 2
N_RI = 2 * NA_KH - 1
N_CI = 2 * NA_KW - 1


def _na_key_row0(tile, rows):
    lo, hi = 0, rows - NATTN_KROWS
    r0 = (tile - 1) * NATTN_ROWS
    if isinstance(tile, int):
        return min(max(r0, lo), hi)
    return jnp.clip(r0, lo, hi)


def _nattn_kernel(q_ref, k_ref, v_ref, kc_ref, vc_ref, bias_ref, o_ref,
                  ks_ref, vs_ref, s_ref, sc_ref, p_ref, pc_ref, l_ref):
    seq = q_ref.shape[0]
    n_tile = seq // NATTN_TQ
    nk = NATTN_KROWS * GRID_W

    def rows_of(tile):
        return pl.ds(pl.multiple_of(tile * NATTN_TQ, NATTN_TQ), NATTN_TQ)

    def key_rows_of(tile):
        return pl.ds(pl.multiple_of(_na_key_row0(tile, seq // GRID_W) * GRID_W, NATTN_TQ), nk)

    def cols_of(g):
        return slice(g * HEAD_DIM, (g + 1) * HEAD_DIM)

    for g in range(NATTN_HG):
        ks_ref[g] = k_ref[:, cols_of(g)]
        vs_ref[g] = v_ref[:, cols_of(g)]

    def scores(tile, _, slot):
        var = jnp.where(tile == 0, 0, jnp.where(tile == n_tile - 1, 2, 1))
        for g in range(NATTN_HG):
            q = q_ref[rows_of(tile), cols_of(g)]
            s_ref[slot, g] = _qk(q, ks_ref[g, key_rows_of(tile), :]) + bias_ref[g, var]
            sc_ref[slot, g] = _qk(q, kc_ref[:, cols_of(g)])

    def probs(_, slot):
        for g in range(NATTN_HG):
            s, sc = s_ref[slot, g], sc_ref[slot, g]
            m = jnp.maximum(jnp.max(s, axis=-1, keepdims=True), jnp.max(sc, axis=-1, keepdims=True))
            p = jnp.exp2((s - m) * EXP2_SCALE)
            pc = jnp.exp2((sc - m) * EXP2_SCALE)
            l_ref[slot, g] = jnp.sum(p, axis=-1, keepdims=True) + jnp.sum(pc, axis=-1, keepdims=True)
            p_ref[slot, g] = p.astype(BF16)
            pc_ref[slot, g] = pc.astype(BF16)

    def weighted_values(tile, _, slot):
        for g in range(NATTN_HG):
            o = (jnp.dot(p_ref[slot, g], vs_ref[g, key_rows_of(tile), :], preferred_element_type=F32)
                 + jnp.dot(pc_ref[slot, g], vc_ref[:, cols_of(g)], preferred_element_type=F32))
            o_ref[rows_of(tile), cols_of(g)] = (o / l_ref[slot, g]).astype(o_ref.dtype)

    p_ref[1] = jnp.ones(p_ref.shape[1:], BF16)
    pc_ref[1] = jnp.ones(pc_ref.shape[1:], BF16)
    l_ref[1] = jnp.ones(l_ref.shape[1:], F32)
    _run_chains(n_tile, 1, scores, probs, weighted_values, unroll=NATTN_UNROLL)


def _na_bias_kernel(rpb_ref, o_ref, u_ref, *, rows):
    base = pl.program_id(0) * (N_RI * N_CI)
    kh = min(NA_KH, rows)

    def fill(sg, carry):
        q0 = pl.multiple_of(sg * SUBLANES, SUBLANES)
        qc = q0 + lax.broadcasted_iota(jnp.int32, (SUBLANES, LANES), 0)
        kc = lax.broadcasted_iota(jnp.int32, (SUBLANES, LANES), 1) & (GRID_W - 1)
        diag = kc - qc + (NA_KW - 1)
        us = [jnp.zeros((SUBLANES, LANES), F32) for _ in range(N_RI)]
        for ci in range(N_CI):
            hit = diag == ci
            for a in range(N_RI):
                us[a] = jnp.where(hit, rpb_ref[base + a * N_CI + ci], us[a])
        for a in range(N_RI):
            u_ref[a, pl.ds(q0, SUBLANES), :] = us[a] * INV_ATTN_SCALE
        return carry

    lax.fori_loop(0, GRID_W // SUBLANES, fill, 0)

    qc = lax.broadcasted_iota(jnp.int32, (GRID_W, LANES), 0)
    lane = lax.broadcasted_iota(jnp.int32, (GRID_W, LANES), 1)
    kc = lane & (GRID_W - 1)
    cs = jnp.clip(qc - NA_KW // 2, 0, GRID_W - NA_KW)
    col_valid = (kc >= cs) & (kc < cs + NA_KW)
    low = lane < GRID_W
    masks = {(True, True): col_valid, (True, False): col_valid & low,
             (False, True): col_valid & jnp.logical_not(low)}
    neg_blk = jnp.full((GRID_W, LANES), NEG, F32)
    n_tile = rows // NATTN_ROWS
    for var, tile in enumerate((0, 1, n_tile - 1)):
        r0 = tile * NATTN_ROWS
        k0 = _na_key_row0(tile, rows)
        for qr in range(NATTN_ROWS):
            rs = min(max(r0 + qr - kh // 2, 0), rows - kh)
            for jj in range(NATTN_KROWS // 2):
                kr = k0 + 2 * jj
                ok = (rs <= kr < rs + kh, rs <= kr + 1 < rs + kh)
                if ok == (False, False):
                    blk = neg_blk
                else:
                    a = kr - (r0 + qr) + NA_KH - 1
                    a_lo = min(max(a, 0), N_RI - 1)
                    a_hi = min(max(a + 1, 0), N_RI - 1)
                    pair = jnp.where(low, u_ref[a_lo], u_ref[a_hi])
                    blk = jnp.where(masks[ok], pair, NEG)
                o_ref[var, qr * GRID_W:(qr + 1) * GRID_W, jj * LANES:(jj + 1) * LANES] = blk


def _na_bias_tables(rpb, rows):
    h = rpb.shape[0]
    assert 2 * GRID_W == LANES and rows >= 3 * NATTN_ROWS and NATTN_ROWS == NA_KH // 2
    return pl.pallas_call(
        functools.partial(_na_bias_kernel, rows=rows),
        out_shape=jax.ShapeDtypeStruct((h, 3, NATTN_TQ, NATTN_KROWS * GRID_W), F32),
        grid=(h,),
        in_specs=[pl.BlockSpec(memory_space=pltpu.SMEM)],
        out_specs=pl.BlockSpec((None, 3, NATTN_TQ, NATTN_KROWS * GRID_W), lambda hh: (hh, 0, 0, 0)),
        scratch_shapes=[pltpu.VMEM((N_RI, GRID_W, LANES), F32)],
        compiler_params=_params("arbitrary"),
        name="na_bias",
    )(rpb.reshape(-1))


def _nattn(qkv, kv_c, bias):
    b, s, _ = qkv.shape
    c = kv_c.shape[1]
    h = C_HEADS
    ng = h // NATTN_HG
    gw = NATTN_HG * HEAD_DIM
    nk = NATTN_KROWS * GRID_W
    assert NATTN_HG % 2 == 0 and s % NATTN_TQ == 0

    def col_map(col):
        return lambda hg, bi: (bi, 0, col + hg)

    return pl.pallas_call(
        _nattn_kernel,
        out_shape=jax.ShapeDtypeStruct((b, s, h * HEAD_DIM), BF16),
        grid=(ng, b),
        in_specs=[
            pl.BlockSpec((None, s, gw), col_map(0)),
            pl.BlockSpec((None, s, gw), col_map(ng)),
            pl.BlockSpec((None, s, gw), col_map(2 * ng)),
            pl.BlockSpec((None, c, gw), col_map(0)),
            pl.BlockSpec((None, c, gw), col_map(ng)),
            pl.BlockSpec((NATTN_HG, 3, NATTN_TQ, nk), lambda hg, bi: (hg, 0, 0, 0),
                         pipeline_mode=pl.Buffered(1)),
        ],
        out_specs=pl.BlockSpec((None, s, gw), col_map(0)),
        scratch_shapes=[
            pltpu.VMEM((NATTN_HG, s, HEAD_DIM), BF16),
            pltpu.VMEM((NATTN_HG, s, HEAD_DIM), BF16),
            pltpu.VMEM((2, NATTN_HG, NATTN_TQ, nk), F32),
            pltpu.VMEM((2, NATTN_HG, NATTN_TQ, c), F32),
            pltpu.VMEM((2, NATTN_HG, NATTN_TQ, nk), BF16),
            pltpu.VMEM((2, NATTN_HG, NATTN_TQ, c), BF16),
            pltpu.VMEM((2, NATTN_HG, NATTN_TQ, 1), F32),
        ],
        compiler_params=_params("arbitrary", "arbitrary"),
        name="nattn",
    )(qkv, qkv, qkv, kv_c, kv_c, bias)


OUTPROJ_ROWS = 256


def _resid_mlp_kernel(*refs, widths, final_norm):
    x_ref, wo_ref, g1_ref, nw_ref, sh_ref, sc_ref, g2_ref, w1_ref, w2_ref = refs[:9]
    rest = refs[9:]
    o_refs = rest[:len(widths)]
    rest = rest[len(widths):]
    if final_norm:
        fw_ref = rest[0]
        rest = rest[1:]
    y_ref, x1_ref, h_ref = rest
    f = pl.program_id(2)
    tm = x_ref.shape[0]

    def mlp_partial():
        a = jnp.dot(h_ref[...], w1_ref[...], preferred_element_type=F32)
        a = jnp.square(jnp.maximum(a, 0.0)).astype(BF16)
        return jnp.dot(a, w2_ref[...], preferred_element_type=F32)

    @pl.when(f == 0)
    def _():
        for r0 in range(0, tm, OUTPROJ_ROWS):
            rows = slice(r0, min(r0 + OUTPROJ_ROWS, tm))
            y = None
            k0 = 0
            for o_ref, kw in zip(o_refs, widths):
                part = jnp.dot(o_ref[rows, :], wo_ref[k0:k0 + kw, :], preferred_element_type=F32)
                y = part if y is None else y + part
                k0 += kw
            x1 = x_ref[rows, :] + g1_ref[...] * y
            x1_ref[rows, :] = x1
            h_ref[rows, :] = _norm_modulate(x1, nw_ref[...], sh_ref[...], sc_ref[...]).astype(BF16)
        y_ref[...] = mlp_partial()

    @pl.when(f > 0)
    def _():
        y_ref[...] += mlp_partial()

    @pl.when(f == pl.num_programs(2) - 1)
    def _():
        y = x1_ref[...] + g2_ref[...] * y_ref[...]
        if final_norm:
            y = y * lax.rsqrt(jnp.mean(y * y, axis=-1, keepdims=True) + NORM_EPS) * fw_ref[...]
        y_ref[...] = y


def _resid_mlp(x, o_parts, w_out, g1, nw, shift, scale, g2, w1, w2, *, tm, tf, final_w=None):
    b, s, d = x.shape
    dff = w1.shape[1]
    widths = tuple(o.shape[2] for o in o_parts)
    assert sum(widths) == w_out.shape[0]
    vec = pl.BlockSpec((None, 1, d), lambda bi, i, f: (bi, 0, 0))
    in_specs = [
        pl.BlockSpec((None, tm, d), lambda bi, i, f: (bi, i, 0)),
        pl.BlockSpec(w_out.shape, lambda bi, i, f: (0, 0), pipeline_mode=pl.Buffered(1)),
        vec,
        pl.BlockSpec((1, d), lambda bi, i, f: (0, 0)),
        vec, vec, vec,
        pl.BlockSpec((d, tf), lambda bi, i, f: (0, f)),
        pl.BlockSpec((tf, d), lambda bi, i, f: (f, 0)),
    ] + [pl.BlockSpec((None, tm, kw), lambda bi, i, f: (bi, i, 0)) for kw in widths]
    args = [x, w_out, g1, nw.reshape(1, d), shift, scale, g2, w1, w2, *o_parts]
    if final_w is not None:
        in_specs.append(pl.BlockSpec((1, d), lambda bi, i, f: (0, 0)))
        args.append(final_w.reshape(1, d))
    return pl.pallas_call(
        functools.partial(_resid_mlp_kernel, widths=widths, final_norm=final_w is not None),
        out_shape=jax.ShapeDtypeStruct(x.shape, x.dtype),
        grid=(b, s // tm, dff // tf),
        in_specs=in_specs,
        out_specs=pl.BlockSpec((None, tm, d), lambda bi, i, f: (bi, i, 0)),
        scratch_shapes=[pltpu.VMEM((tm, d), F32), pltpu.VMEM((tm, d), BF16)],
        compiler_params=_params("arbitrary", "arbitrary", "arbitrary"),
        name="resid_mlp",
    )(*args)


EVEN_SECTIONS = (
    (0, A_Q_HEADS, "q_norm"),
    (A_Q_HEADS * HEAD_DIM, A_KV_HEADS, "k_norm"),
    ((A_Q_HEADS + A_KV_HEADS) * HEAD_DIM, A_KV_HEADS, "v"),
    ((A_Q_HEADS + 2 * A_KV_HEADS) * HEAD_DIM, B_Q_HEADS, "q"),
    ((A_Q_HEADS + 2 * A_KV_HEADS + B_Q_HEADS) * HEAD_DIM, B_KV_HEADS, "k"),
    ((A_Q_HEADS + 2 * A_KV_HEADS + B_Q_HEADS + B_KV_HEADS) * HEAD_DIM, B_KV_HEADS, "v"),
)
LAT_TM = 512
MLP_TF = 1024


def kernel(x, c, ctx, c_ctx, ada_w, ada_b, norm_w, mlp_w1, mlp_w2, ev_w_in, ev_w_out, ev_q_norm,
           ev_k_norm, ev_sink, od_w_in, od_w_out, od_rpb, final_norm_w):
    b, s, d = x.shape
    n_ctx = ctx.shape[1]
    depth = ada_w.shape[0]
    rows = s // GRID_W

    mod = _ada_mod(jnp.concatenate([c, c_ctx[None, :]], axis=0), ada_w, ada_b)
    rope_tabs = _rope_tables(s)

    def mod_vecs(layer):
        lat = [mod[layer, :b, k * d:(k + 1) * d][:, None, :] for k in range(6)]
        cx = [jnp.broadcast_to(mod[layer, b:b + 1, k * d:(k + 1) * d][:, None, :], (b, 1, d))
              for k in range(6)]
        return lat, cx

    for layer in range(depth):
        need_ctx = layer < depth - 1
        (sh1, sc1, g1, sh2, sc2, g2), (sh1c, sc1c, g1c, sh2c, sc2c, g2c) = mod_vecs(layer)
        j = layer // 2
        w1 = mlp_w1[layer].astype(BF16)
        w2 = mlp_w2[layer].astype(BF16)
        if layer % 2 == 0:
            w_in = ev_w_in[j].astype(BF16)
            w_out = ev_w_out[j].astype(BF16)
            n_in = w_in.shape[1]
            norms = (ev_q_norm[j], ev_k_norm[j])
            qkv = _proj(x, norm_w[layer, 0], sh1, sc1, w_in, tm=LAT_TM, tn=n_in,
                        sections=EVEN_SECTIONS, head_norms=norms, rope_tabs=rope_tabs, name="proj_even")
            qkv_c = _proj(ctx, norm_w[layer, 0], sh1c, sc1c, w_in, tm=n_ctx, tn=n_in,
                          sections=EVEN_SECTIONS, head_norms=norms, name="proj_even_ctx")
            oa = _gattn(qkv, qkv_c, q_blk=0, k_blk=A_Q_HEADS, v_blk=A_Q_HEADS + A_KV_HEADS)
            sink_kg = ev_sink[j].reshape(B_KV_HEADS, GQA_GROUP).astype(F32)
            qb_blk = (A_Q_HEADS + 2 * A_KV_HEADS) // GQA_GROUP
            kb_blk = A_Q_HEADS + 2 * A_KV_HEADS + B_Q_HEADS
            ob = _wattn(sink_kg, qkv, qkv_c, q_blk=qb_blk, k_blk=kb_blk, v_blk=kb_blk + B_KV_HEADS)
            o_parts = [oa, ob]
            if need_ctx:
                sink_all = jnp.concatenate([jnp.full((A_KV_HEADS, GQA_GROUP), NEG, F32), sink_kg], axis=0)
                oc_parts = [_cattn(sink_all, qkv_c)]
        else:
            w_in = od_w_in[j].astype(BF16)
            w_out = od_w_out[j].astype(BF16)
            hd = C_HEADS * HEAD_DIM
            qkv = _proj(x, norm_w[layer, 0], sh1, sc1, w_in, tm=LAT_TM, tn=hd, name="proj_odd")
            kv_c = _proj(ctx, norm_w[layer, 0], sh1c, sc1c, w_in, tm=n_ctx, tn=hd, n_off=1,
                         n_out=2 * hd, name="proj_odd_ctx")
            o_parts = [_nattn(qkv, kv_c, _na_bias_tables(od_rpb[j], rows))]
            assert not need_ctx, "the odd layer is the last one at this depth"
        x = _resid_mlp(x, o_parts, w_out, g1, norm_w[layer, 1], sh2, sc2, g2, w1, w2, tm=LAT_TM,
                       tf=MLP_TF, final_w=None if need_ctx else final_norm_w)
        if need_ctx:
            ctx = _resid_mlp(ctx, oc_parts, w_out, g1c, norm_w[layer, 1], sh2c, sc2c, g2c, w1, w2,
                             tm=n_ctx, tf=MLP_TF)
    return x
```

```python
import functools

import jax
import jax.numpy as jnp
import numpy as np
from jax import lax
from jax.experimental import pallas as pl
from jax.experimental.pallas import tpu as pltpu

D_MODEL = 2048
GRID_W = 64
HEAD_DIM = 128
N_HEADS = D_MODEL // HEAD_DIM
A_Q_HEADS = N_HEADS // 2
A_KV_HEADS = max(1, A_Q_HEADS // 4)
B_Q_HEADS = N_HEADS - A_Q_HEADS
B_KV_HEADS = max(1, B_Q_HEADS // 4)
GQA_GROUP = A_Q_HEADS // A_KV_HEADS
C_HEADS = N_HEADS
WINDOW = 128
NA_KH = 8
NA_KW = 16
D_FF = 4 * D_MODEL
ROPE_THETA = 10000.0
ROPE_PAIRS = HEAD_DIM // 4
NORM_EPS = 1e-6
NEG = -1e30
ATTN_SCALE = HEAD_DIM ** -0.5

V7X_VMEM_BYTES = 64 * 1024 * 1024
VMEM_LIMIT = V7X_VMEM_BYTES - 6 * 1024 * 1024
LANES = 128
SUBLANES = 8

F32 = jnp.float32
BF16 = jnp.bfloat16


def _params(*sem):
    return pltpu.CompilerParams(dimension_semantics=sem, vmem_limit_bytes=VMEM_LIMIT)


ADA_ROWS = 3
ADA_TN = 1024


def _ada_kernel(cb_ref, w_ref, b_ref, o_ref, s_ref):
    @pl.when((pl.program_id(0) == 0) & (pl.program_id(1) == 0))
    def _():
        cvals = cb_ref[...]
        s_ref[...] = cvals * (1.0 / (1.0 + jnp.exp(-cvals)))

    nchunk = ADA_TN // LANES

    def body(kk, accs):
        k0 = pl.multiple_of(kk * SUBLANES, SUBLANES)
        w = w_ref[pl.ds(k0, SUBLANES), :]
        out = []
        for r in range(ADA_ROWS):
            s = s_ref[r, pl.ds(k0, SUBLANES), :]
            for cch in range(nchunk):
                out.append(accs[r * nchunk + cch] + w[:, cch * LANES:(cch + 1) * LANES] * s)
        return tuple(out)

    zeros = tuple(jnp.zeros((SUBLANES, LANES), F32) for _ in range(ADA_ROWS * nchunk))
    accs = lax.fori_loop(0, D_MODEL // SUBLANES, body, zeros, unroll=2)
    o_ref[...] = jnp.broadcast_to(b_ref[...], o_ref.shape)
    for r in range(ADA_ROWS):
        row = jnp.concatenate(
            [jnp.sum(accs[r * nchunk + cch], axis=0, keepdims=True) for cch in range(nchunk)], axis=1)
        o_ref[r:r + 1, :] = row + b_ref[...]


def _ada_mod(cc, ada_w, ada_b):
    depth, d, n = ada_w.shape
    cb = jnp.broadcast_to(cc[:, :, None], (ADA_ROWS, d, LANES))
    return pl.pallas_call(
        _ada_kernel,
        out_shape=jax.ShapeDtypeStruct((depth, SUBLANES, n), F32),
        grid=(depth, n // ADA_TN),
        in_specs=[
            pl.BlockSpec((ADA_ROWS, d, LANES), lambda l, j: (0, 0, 0)),
            pl.BlockSpec((None, d, ADA_TN), lambda l, j: (l, 0, j)),
            pl.BlockSpec((None, 1, ADA_TN), lambda l, j: (l, 0, j)),
        ],
        out_specs=pl.BlockSpec((None, SUBLANES, ADA_TN), lambda l, j: (l, 0, j)),
        scratch_shapes=[pltpu.VMEM((ADA_ROWS, d, LANES), F32)],
        compiler_params=_params("arbitrary", "arbitrary"),
        name="ada_mod",
    )(cb, ada_w, ada_b.reshape(depth, 1, n))


CAST_BLOCK_BYTES = 8 * 1024 * 1024


def _cast_kernel(w_ref, o_ref):
    o_ref[...] = w_ref[...].astype(o_ref.dtype)


def _to_bf16(w_stack, idx):
    _, r, c = w_stack.shape
    tr = r
    while tr * c * w_stack.dtype.itemsize > CAST_BLOCK_BYTES and tr % 32 == 0:
        tr //= 2
    return pl.pallas_call(
        _cast_kernel,
        out_shape=jax.ShapeDtypeStruct((r, c), BF16),
        grid=(r // tr,),
        in_specs=[pl.BlockSpec((None, tr, c), lambda i: (idx, i, 0))],
        out_specs=pl.BlockSpec((tr, c), lambda i: (i, 0)),
        compiler_params=_params("arbitrary"),
        name="to_bf16",
    )(w_stack)


def _norm_modulate(x, nw, shift, scale):
    ms = jnp.mean(x * x, axis=-1, keepdims=True)
    y = x * lax.rsqrt(ms + NORM_EPS) * nw
    return y * (1.0 + scale) + shift


def _rope(y, cos, sin_lo, sin_hi):
    return (y * cos + pltpu.roll(y, HEAD_DIM - ROPE_PAIRS, 1) * sin_lo
            + pltpu.roll(y, ROPE_PAIRS, 1) * sin_hi)


def _rope_tables(seq):
    t = jnp.arange(seq)
    row = (t // GRID_W).astype(F32)
    col = (t % GRID_W).astype(F32)
    inv = ROPE_THETA ** (-jnp.arange(ROPE_PAIRS, dtype=F32) / ROPE_PAIRS)
    ang_r = row[:, None] * inv
    ang_c = col[:, None] * inv
    ang = jnp.concatenate([ang_r, ang_r, ang_c, ang_c], axis=-1)
    cos, sin = jnp.cos(ang), jnp.sin(ang)
    first = (jnp.arange(HEAD_DIM) % (2 * ROPE_PAIRS)) < ROPE_PAIRS
    sin_lo = jnp.where(first, -sin, 0.0)
    sin_hi = jnp.where(first, 0.0, sin)
    return cos, sin_lo, sin_hi


PROJ_ROWS = 256


def _proj_kernel(*refs, sections, rope):
    x_ref, nw_ref, sh_ref, sc_ref, w_ref = refs[:5]
    rest = refs[5:]
    if sections is not None:
        qn_ref, kn_ref = rest[:2]
        rest = rest[2:]
    if rope:
        cos_ref, slo_ref, shi_ref = rest[:3]
        rest = rest[3:]
    o_ref = rest[0]
    h_ref = rest[1] if sections is None else None
    tm = x_ref.shape[0]

    def project(h, rows):
        if sections is None:
            o_ref[rows, :] = jnp.dot(h, w_ref[...], preferred_element_type=F32).astype(o_ref.dtype)
            return
        for c0, nh, kind in sections:
            y = jnp.dot(h, w_ref[:, c0:c0 + nh * HEAD_DIM], preferred_element_type=F32)
            for hh in range(nh):
                yh = y[:, hh * HEAD_DIM:(hh + 1) * HEAD_DIM]
                if kind in ("q_norm", "k_norm"):
                    hw = (qn_ref if kind == "q_norm" else kn_ref)[...]
                    yh = yh * lax.rsqrt(jnp.mean(yh * yh, axis=-1, keepdims=True) + NORM_EPS) * hw
                if rope and kind != "v":
                    yh = _rope(yh, cos_ref[rows, :], slo_ref[rows, :], shi_ref[rows, :])
                lo = c0 + hh * HEAD_DIM
                o_ref[rows, lo:lo + HEAD_DIM] = yh.astype(o_ref.dtype)

    @pl.when(pl.program_id(2) == 0)
    def _():
        for r0 in range(0, tm, PROJ_ROWS):
            rows = slice(r0, min(r0 + PROJ_ROWS, tm))
            h = _norm_modulate(x_ref[rows, :], nw_ref[...], sh_ref[...], sc_ref[...]).astype(BF16)
            if h_ref is not None:
                h_ref[rows, :] = h
            project(h, rows)

    if h_ref is not None:
        @pl.when(pl.program_id(2) > 0)
        def _():
            project(h_ref[...], slice(0, tm))


def _proj(x, nw, shift, scale, w, *, tm, tn, n_off=0, n_out=None, sections=None,
          head_norms=None, rope_tabs=None, name="proj"):
    b, s, d = x.shape
    n_out = w.shape[1] if n_out is None else n_out
    nj = n_out // tn
    in_specs = [
        pl.BlockSpec((None, tm, d), lambda bi, i, j: (bi, i, 0)),
        pl.BlockSpec((1, d), lambda bi, i, j: (0, 0)),
        pl.BlockSpec((None, 1, d), lambda bi, i, j: (bi, 0, 0)),
        pl.BlockSpec((None, 1, d), lambda bi, i, j: (bi, 0, 0)),
        pl.BlockSpec((d, tn), lambda bi, i, j: (0, j + n_off)),
    ]
    args = [x, nw.reshape(1, d), shift, scale, w]
    if sections is not None:
        assert nj == 1
        in_specs += [pl.BlockSpec((1, HEAD_DIM), lambda bi, i, j: (0, 0))] * 2
        args += [head_norms[0].reshape(1, HEAD_DIM), head_norms[1].reshape(1, HEAD_DIM)]
    if rope_tabs is not None:
        in_specs += [pl.BlockSpec((tm, HEAD_DIM), lambda bi, i, j: (i, 0))] * 3
        args += list(rope_tabs)
    return pl.pallas_call(
        functools.partial(_proj_kernel, sections=sections, rope=rope_tabs is not None),
        out_shape=jax.ShapeDtypeStruct((b, s, n_out), BF16),
        grid=(b, s // tm, nj),
        in_specs=in_specs,
        out_specs=pl.BlockSpec((None, tm, tn), lambda bi, i, j: (bi, i, j)),
        scratch_shapes=[pltpu.VMEM((tm, d), BF16)] if sections is None else [],
        compiler_params=_params("arbitrary", "arbitrary", "arbitrary"),
        name=name,
    )(*args)


def _qk(q, k):
    return lax.dot_general(q, k, (((1,), (1,)), ((), ())), preferred_element_type=F32)


EXP2_SCALE = ATTN_SCALE * float(np.log2(np.e))
INV_ATTN_SCALE = 1.0 / ATTN_SCALE


def _softmax_pv(parts, sink=None):
    m = functools.reduce(jnp.maximum, [jnp.max(s, axis=-1, keepdims=True) for s, _ in parts])
    if sink is not None:
        m = jnp.maximum(m, sink)
    l = None
    o = None
    for s, v in parts:
        p = jnp.exp2((s - m) * EXP2_SCALE)
        ls = jnp.sum(p, axis=-1, keepdims=True)
        os_ = jnp.dot(p.astype(BF16), v, preferred_element_type=F32)
        l = ls if l is None else l + ls
        o = os_ if o is None else o + os_
    if sink is not None:
        l = l + jnp.exp2((sink - m) * EXP2_SCALE)
    return o / l


def _run_chains(n_blk, n_inner, scores, probs, weighted_values, unroll=1):
    per_body = unroll * n_inner
    assert per_body % 2 == 0 and n_blk % unroll == 0

    def chain(it, n):
        return it * unroll + n // n_inner, n % n_inner

    scores(0, 0, 0)

    def body(it, carry):
        for n in range(per_body):
            blk, g = chain(it, n + 1)
            scores(jnp.minimum(blk, n_blk - 1) if n + 1 == per_body else blk, g, (n + 1) % 2)
            probs(n % n_inner, n % 2)
            blk, g = chain(it, n - 1)
            weighted_values(jnp.maximum(blk, 0) if n == 0 else blk, g, (n - 1) % 2)
        return carry

    lax.fori_loop(0, n_blk // unroll, body, 0)
    weighted_values(n_blk - 1, n_inner - 1, 1)


GATTN_ROWS = 512


def _gattn_kernel(q_ref, k_ref, v_ref, kc_ref, vc_ref, o_ref, kall_ref, vall_ref, s_ref, p_ref):
    c = kc_ref.shape[0]

    n_blk = q_ref.shape[0] // GATTN_ROWS

    kall_ref[:c, :] = kc_ref[...]
    kall_ref[c:, :] = k_ref[...]
    vall_ref[:c, :HEAD_DIM] = vc_ref[...]
    vall_ref[c:, :HEAD_DIM] = v_ref[...]
    lane = lax.broadcasted_iota(jnp.int32, (vall_ref.shape[0], HEAD_DIM), 1)
    vall_ref[:, HEAD_DIM:] = jnp.where(lane == 0, 1.0, 0.0).astype(BF16)

    def rows_of(blk):
        return pl.ds(pl.multiple_of(blk * GATTN_ROWS, GATTN_ROWS), GATTN_ROWS)

    def cols_of(g):
        return slice(g * HEAD_DIM, (g + 1) * HEAD_DIM)

    def scores(blk, g, slot):
        s_ref[slot] = _qk(q_ref[rows_of(blk), cols_of(g)], kall_ref[...])

    def probs(g, slot):
        m = jnp.max(s_ref[slot], axis=-1, keepdims=True)
        p_ref[slot] = jnp.exp2((s_ref[slot] - m) * EXP2_SCALE).astype(BF16)

    def weighted_values(blk, g, slot):
        ol = jnp.dot(p_ref[slot], vall_ref[...], preferred_element_type=F32)
        o = ol[:, :HEAD_DIM] / ol[:, HEAD_DIM:HEAD_DIM + 1]
        o_ref[rows_of(blk), cols_of(g)] = o.astype(o_ref.dtype)

    p_ref[1] = jnp.ones(p_ref.shape[1:], BF16)
    _run_chains(n_blk, GQA_GROUP, scores, probs, weighted_values)


def _gattn(qkv, qkv_c, *, q_blk, k_blk, v_blk):
    b, s, _ = qkv.shape
    c = qkv_c.shape[1]
    gw = GQA_GROUP * HEAD_DIM
    assert GQA_GROUP % 2 == 0 and s % GATTN_ROWS == 0
    single = pl.Buffered(1)
    return pl.pallas_call(
        _gattn_kernel,
        out_shape=jax.ShapeDtypeStruct((b, s, A_Q_HEADS * HEAD_DIM), BF16),
        grid=(b, A_KV_HEADS),
        in_specs=[
            pl.BlockSpec((None, s, gw), lambda bi, kv: (bi, 0, q_blk + kv), pipeline_mode=single),
            pl.BlockSpec((None, s, HEAD_DIM), lambda bi, kv: (bi, 0, k_blk + kv), pipeline_mode=single),
            pl.BlockSpec((None, s, HEAD_DIM), lambda bi, kv: (bi, 0, v_blk + kv), pipeline_mode=single),
            pl.BlockSpec((None, c, HEAD_DIM), lambda bi, kv: (bi, 0, k_blk + kv)),
            pl.BlockSpec((None, c, HEAD_DIM), lambda bi, kv: (bi, 0, v_blk + kv)),
        ],
        out_specs=pl.BlockSpec((None, s, gw), lambda bi, kv: (bi, 0, kv)),
        scratch_shapes=[
            pltpu.VMEM((c + s, HEAD_DIM), BF16),
            pltpu.VMEM((c + s, 2 * HEAD_DIM), BF16),
            pltpu.VMEM((2, GATTN_ROWS, c + s), F32),
            pltpu.VMEM((2, GATTN_ROWS, c + s), BF16),
        ],
        compiler_params=_params("arbitrary", "arbitrary"),
        name="gattn",
    )(qkv, qkv, qkv, qkv_c, qkv_c)


WATTN_TQ = 4 * WINDOW
WATTN_NK = WATTN_TQ + 2 * WINDOW


def _wattn_kernel(sink_ref, q_ref, k_ref, v_ref, kc_ref, vc_ref, o_ref,
                  mask_ref, s_ref, sc_ref, p_ref, pc_ref, l_ref):
    kv = pl.program_id(1)
    seq = q_ref.shape[0]
    n_tile = seq // WATTN_TQ

    def rows_of(tile):
        return pl.ds(pl.multiple_of(tile * WATTN_TQ, WATTN_TQ), WATTN_TQ)

    def key0_of(tile):
        lo, hi = 0, seq - WATTN_NK
        k0 = tile * WATTN_TQ - WINDOW
        return min(max(k0, lo), hi) if isinstance(tile, int) else jnp.clip(k0, lo, hi)

    def key_rows_of(tile):
        return pl.ds(pl.multiple_of(key0_of(tile), WINDOW), WATTN_NK)

    def cols_of(g):
        return slice(g * HEAD_DIM, (g + 1) * HEAD_DIM)

    for var, tile in enumerate((0, 1, n_tile - 1)):
        shift = key0_of(tile) - tile * WATTN_TQ
        dist = (shift + lax.broadcasted_iota(jnp.int32, (WATTN_TQ, WATTN_NK), 1)
                - lax.broadcasted_iota(jnp.int32, (WATTN_TQ, WATTN_NK), 0))
        mask_ref[var] = jnp.where(jnp.abs(dist) <= WINDOW, 0.0, NEG)

    def scores(tile, g, slot):
        q = q_ref[rows_of(tile), cols_of(g)]
        var = jnp.where(tile == 0, 0, jnp.where(tile == n_tile - 1, 2, 1))
        s_ref[slot] = _qk(q, k_ref[key_rows_of(tile), :]) + mask_ref[var]
        sc_ref[slot] = _qk(q, kc_ref[...])

    def probs(g, slot):
        s, sc = s_ref[slot], sc_ref[slot]
        sink = sink_ref[kv, g] * INV_ATTN_SCALE
        m = jnp.maximum(jnp.max(s, axis=-1, keepdims=True), jnp.max(sc, axis=-1, keepdims=True))
        m = jnp.maximum(m, sink)
        p = jnp.exp2((s - m) * EXP2_SCALE)
        pc = jnp.exp2((sc - m) * EXP2_SCALE)
        l_ref[slot] = (jnp.sum(p, axis=-1, keepdims=True) + jnp.sum(pc, axis=-1, keepdims=True)
                       + jnp.exp2((sink - m) * EXP2_SCALE))
        p_ref[slot] = p.astype(BF16)
        pc_ref[slot] = pc.astype(BF16)

    def weighted_values(tile, g, slot):
        o = (jnp.dot(p_ref[slot], v_ref[key_rows_of(tile), :], preferred_element_type=F32)
             + jnp.dot(pc_ref[slot], vc_ref[...], preferred_element_type=F32))
        o_ref[rows_of(tile), cols_of(g)] = (o / l_ref[slot]).astype(o_ref.dtype)

    p_ref[1] = jnp.ones(p_ref.shape[1:], BF16)
    pc_ref[1] = jnp.ones(pc_ref.shape[1:], BF16)
    l_ref[1] = jnp.ones(l_ref.shape[1:], F32)
    _run_chains(seq // WATTN_TQ, GQA_GROUP, scores, probs, weighted_values)


def _wattn(sink_kg, qkv, qkv_c, *, q_blk, k_blk, v_blk):
    b, s, _ = qkv.shape
    c = qkv_c.shape[1]
    gw = GQA_GROUP * HEAD_DIM
    assert s % WATTN_TQ == 0 and s >= 3 * WATTN_TQ

    def col_map(col):
        return lambda bi, kv: (bi, 0, col + kv)

    return pl.pallas_call(
        _wattn_kernel,
        out_shape=jax.ShapeDtypeStruct((b, s, B_Q_HEADS * HEAD_DIM), BF16),
        grid=(b, B_KV_HEADS),
        in_specs=[
            pl.BlockSpec(memory_space=pltpu.SMEM),
            pl.BlockSpec((None, s, gw), col_map(q_blk)),
            pl.BlockSpec((None, s, HEAD_DIM), col_map(k_blk)),
            pl.BlockSpec((None, s, HEAD_DIM), col_map(v_blk)),
            pl.BlockSpec((None, c, HEAD_DIM), col_map(k_blk)),
            pl.BlockSpec((None, c, HEAD_DIM), col_map(v_blk)),
        ],
        out_specs=pl.BlockSpec((None, s, gw), col_map(0)),
        scratch_shapes=[
            pltpu.VMEM((3, WATTN_TQ, WATTN_NK), F32),
            pltpu.VMEM((2, WATTN_TQ, WATTN_NK), F32),
            pltpu.VMEM((2, WATTN_TQ, c), F32),
            pltpu.VMEM((2, WATTN_TQ, WATTN_NK), BF16),
            pltpu.VMEM((2, WATTN_TQ, c), BF16),
            pltpu.VMEM((2, WATTN_TQ, 1), F32),
        ],
        compiler_params=_params("arbitrary", "arbitrary"),
        name="wattn",
    )(sink_kg, qkv, qkv, qkv, qkv_c, qkv_c)


def _cattn_kernel(sink_ref, q_ref, k_ref, v_ref, o_ref):
    grp = pl.program_id(1)
    k, v = k_ref[...], v_ref[...]
    for g in range(GQA_GROUP):
        q = q_ref[:, g * HEAD_DIM:(g + 1) * HEAD_DIM]
        o = _softmax_pv([(_qk(q, k), v)], sink=sink_ref[grp, g] * INV_ATTN_SCALE)
        o_ref[:, g * HEAD_DIM:(g + 1) * HEAD_DIM] = o.astype(o_ref.dtype)


def _cattn(sink_all, qkv_c):
    b, c, _ = qkv_c.shape
    gw = GQA_GROUP * HEAD_DIM
    n_grp = A_KV_HEADS + B_KV_HEADS
    q_map = lambda bi, g: (bi, 0, g + g // A_KV_HEADS)
    k_map = lambda bi, g: (bi, 0, 8 + g + 10 * (g // A_KV_HEADS))
    v_map = lambda bi, g: (bi, 0, 10 + g + 10 * (g // A_KV_HEADS))
    return pl.pallas_call(
        _cattn_kernel,
        out_shape=jax.ShapeDtypeStruct((b, c, N_HEADS * HEAD_DIM), BF16),
        grid=(b, n_grp),
        in_specs=[
            pl.BlockSpec(memory_space=pltpu.SMEM),
            pl.BlockSpec((None, c, gw), q_map),
            pl.BlockSpec((None, c, HEAD_DIM), k_map),
            pl.BlockSpec((None, c, HEAD_DIM), v_map),
        ],
        out_specs=pl.BlockSpec((None, c, gw), lambda bi, g: (bi, 0, g)),
        compiler_params=_params("arbitrary", "arbitrary"),
        name="cattn",
    )(sink_all, qkv_c, qkv_c, qkv_c)


NATTN_ROWS = 4
NATTN_TQ = NATTN_ROWS * GRID_W
NATTN_KROWS = 3 * NATTN_ROWS


NATTN_HG = 2
NATTN_UNROLL = 4
N_RI = 2 * NA_KH - 1
N_CI = 2 * NA_KW - 1


def _na_key_row0(tile, rows):
    lo, hi = 0, rows - NATTN_KROWS
    r0 = (tile - 1) * NATTN_ROWS
    if isinstance(tile, int):
        return min(max(r0, lo), hi)
    return jnp.clip(r0, lo, hi)


def _nattn_kernel(q_ref, k_ref, v_ref, kc_ref, vc_ref, bias_ref, o_ref,
                  vs_ref, vcs_ref, s_ref, sc_ref, p_ref, pc_ref):
    seq = q_ref.shape[0]
    c = kc_ref.shape[0]
    n_tile = seq // NATTN_TQ
    nk = NATTN_KROWS * GRID_W

    def rows_of(tile):
        return pl.ds(pl.multiple_of(tile * NATTN_TQ, NATTN_TQ), NATTN_TQ)

    def key_rows_of(tile):
        return pl.ds(pl.multiple_of(_na_key_row0(tile, seq // GRID_W) * GRID_W, NATTN_TQ), nk)

    def cols_of(g):
        return slice(g * HEAD_DIM, (g + 1) * HEAD_DIM)

    def ones_col(n):
        return jnp.where(lax.broadcasted_iota(jnp.int32, (n, HEAD_DIM), 1) == 0, 1.0, 0.0).astype(BF16)

    for g in range(NATTN_HG):
        vs_ref[g] = jnp.concatenate([v_ref[:, cols_of(g)], ones_col(seq)], axis=1)
        vcs_ref[g] = jnp.concatenate([vc_ref[:, cols_of(g)], ones_col(c)], axis=1)

    def scores(tile, _, slot):
        var = jnp.where(tile == 0, 0, jnp.where(tile == n_tile - 1, 2, 1))
        for g in range(NATTN_HG):
            q = q_ref[rows_of(tile), cols_of(g)]
            s_ref[slot, g] = _qk(q, k_ref[key_rows_of(tile), cols_of(g)]) + bias_ref[g, var]
            sc_ref[slot, g] = _qk(q, kc_ref[:, cols_of(g)])

    def probs(_, slot):
        for g in range(NATTN_HG):
            s, sc = s_ref[slot, g], sc_ref[slot, g]
            folded = jnp.concatenate([jnp.maximum(s[:, :c], sc), s[:, c:]], axis=1)
            m = jnp.max(folded, axis=-1, keepdims=True)
            p_ref[slot, g] = jnp.exp2((s - m) * EXP2_SCALE).astype(BF16)
            pc_ref[slot, g] = jnp.exp2((sc - m) * EXP2_SCALE).astype(BF16)

    def weighted_values(tile, _, slot):
        for g in range(NATTN_HG):
            ol = (jnp.dot(p_ref[slot, g], vs_ref[g, key_rows_of(tile), :], preferred_element_type=F32)
                  + jnp.dot(pc_ref[slot, g], vcs_ref[g], preferred_element_type=F32))
            o = ol[:, :HEAD_DIM] / ol[:, HEAD_DIM:HEAD_DIM + 1]
            o_ref[rows_of(tile), cols_of(g)] = o.astype(o_ref.dtype)

    p_ref[1] = jnp.ones(p_ref.shape[1:], BF16)
    pc_ref[1] = jnp.ones(pc_ref.shape[1:], BF16)
    _run_chains(n_tile, 1, scores, probs, weighted_values, unroll=NATTN_UNROLL)


def _na_bias_kernel(rpb_ref, o_ref, u_ref, *, rows):
    base = pl.program_id(0) * (N_RI * N_CI)
    kh = min(NA_KH, rows)

    def fill(sg, carry):
        q0 = pl.multiple_of(sg * SUBLANES, SUBLANES)
        qc = q0 + lax.broadcasted_iota(jnp.int32, (SUBLANES, LANES), 0)
        kc = lax.broadcasted_iota(jnp.int32, (SUBLANES, LANES), 1) & (GRID_W - 1)
        diag = kc - qc + (NA_KW - 1)
        us = [jnp.zeros((SUBLANES, LANES), F32) for _ in range(N_RI)]
        for ci in range(N_CI):
            hit = diag == ci
            for a in range(N_RI):
                us[a] = jnp.where(hit, rpb_ref[base + a * N_CI + ci], us[a])
        for a in range(N_RI):
            u_ref[a, pl.ds(q0, SUBLANES), :] = us[a] * INV_ATTN_SCALE
        return carry

    lax.fori_loop(0, GRID_W // SUBLANES, fill, 0)

    qc = lax.broadcasted_iota(jnp.int32, (GRID_W, LANES), 0)
    lane = lax.broadcasted_iota(jnp.int32, (GRID_W, LANES), 1)
    kc = lane & (GRID_W - 1)
    cs = jnp.clip(qc - NA_KW // 2, 0, GRID_W - NA_KW)
    col_valid = (kc >= cs) & (kc < cs + NA_KW)
    low = lane < GRID_W
    masks = {(True, True): col_valid, (True, False): col_valid & low,
             (False, True): col_valid & jnp.logical_not(low)}
    neg_blk = jnp.full((GRID_W, LANES), NEG, F32)
    n_tile = rows // NATTN_ROWS
    for var, tile in enumerate((0, 1, n_tile - 1)):
        r0 = tile * NATTN_ROWS
        k0 = _na_key_row0(tile, rows)
        for qr in range(NATTN_ROWS):
            rs = min(max(r0 + qr - kh // 2, 0), rows - kh)
            for jj in range(NATTN_KROWS // 2):
                kr = k0 + 2 * jj
                ok = (rs <= kr < rs + kh, rs <= kr + 1 < rs + kh)
                if ok == (False, False):
                    blk = neg_blk
                else:
                    a = kr - (r0 + qr) + NA_KH - 1
                    a_lo = min(max(a, 0), N_RI - 1)
                    a_hi = min(max(a + 1, 0), N_RI - 1)
                    pair = jnp.where(low, u_ref[a_lo], u_ref[a_hi])
                    blk = jnp.where(masks[ok], pair, NEG)
                o_ref[var, qr * GRID_W:(qr + 1) * GRID_W, jj * LANES:(jj + 1) * LANES] = blk


def _na_bias_tables(rpb, rows):
    h = rpb.shape[0]
    assert 2 * GRID_W == LANES and rows >= 3 * NATTN_ROWS and NATTN_ROWS == NA_KH // 2
    return pl.pallas_call(
        functools.partial(_na_bias_kernel, rows=rows),
        out_shape=jax.ShapeDtypeStruct((h, 3, NATTN_TQ, NATTN_KROWS * GRID_W), F32),
        grid=(h,),
        in_specs=[pl.BlockSpec(memory_space=pltpu.SMEM)],
        out_specs=pl.BlockSpec((None, 3, NATTN_TQ, NATTN_KROWS * GRID_W), lambda hh: (hh, 0, 0, 0)),
        scratch_shapes=[pltpu.VMEM((N_RI, GRID_W, LANES), F32)],
        compiler_params=_params("arbitrary"),
        name="na_bias",
    )(rpb.reshape(-1))


def _nattn(qkv, kv_c, bias):
    b, s, _ = qkv.shape
    c = kv_c.shape[1]
    h = C_HEADS
    ng = h // NATTN_HG
    gw = NATTN_HG * HEAD_DIM
    nk = NATTN_KROWS * GRID_W
    assert NATTN_HG % 2 == 0 and s % NATTN_TQ == 0

    def col_map(col):
        return lambda hg, bi: (bi, 0, col + hg)

    return pl.pallas_call(
        _nattn_kernel,
        out_shape=jax.ShapeDtypeStruct((b, s, h * HEAD_DIM), BF16),
        grid=(ng, b),
        in_specs=[
            pl.BlockSpec((None, s, gw), col_map(0)),
            pl.BlockSpec((None, s, gw), col_map(ng)),
            pl.BlockSpec((None, s, gw), col_map(2 * ng)),
            pl.BlockSpec((None, c, gw), col_map(0)),
            pl.BlockSpec((None, c, gw), col_map(ng)),
            pl.BlockSpec((NATTN_HG, 3, NATTN_TQ, nk), lambda hg, bi: (hg, 0, 0, 0),
                         pipeline_mode=pl.Buffered(1)),
        ],
        out_specs=pl.BlockSpec((None, s, gw), col_map(0)),
        scratch_shapes=[
            pltpu.VMEM((NATTN_HG, s, 2 * HEAD_DIM), BF16),
            pltpu.VMEM((NATTN_HG, c, 2 * HEAD_DIM), BF16),
            pltpu.VMEM((2, NATTN_HG, NATTN_TQ, nk), F32),
            pltpu.VMEM((2, NATTN_HG, NATTN_TQ, c), F32),
            pltpu.VMEM((2, NATTN_HG, NATTN_TQ, nk), BF16),
            pltpu.VMEM((2, NATTN_HG, NATTN_TQ, c), BF16),
        ],
        compiler_params=_params("arbitrary", "arbitrary"),
        name="nattn",
    )(qkv, qkv, qkv, kv_c, kv_c, bias)


OUTPROJ_ROWS = 256


def _resid_mlp_kernel(*refs, widths, final_norm):
    x_ref, wo_ref, g1_ref, nw_ref, sh_ref, sc_ref, g2_ref, w1_ref, w2_ref = refs[:9]
    rest = refs[9:]
    o_refs = rest[:len(widths)]
    rest = rest[len(widths):]
    if final_norm:
        fw_ref = rest[0]
        rest = rest[1:]
    y_ref, x1_ref, h_ref = rest
    f = pl.program_id(2)
    tm = x_ref.shape[0]

    def mlp_partial():
        a = jnp.dot(h_ref[...], w1_ref[...], preferred_element_type=F32)
        a = jnp.square(jnp.maximum(a, 0.0)).astype(BF16)
        return jnp.dot(a, w2_ref[...], preferred_element_type=F32)

    @pl.when(f == 0)
    def _():
        for r0 in range(0, tm, OUTPROJ_ROWS):
            rows = slice(r0, min(r0 + OUTPROJ_ROWS, tm))
            y = None
            k0 = 0
            for o_ref, kw in zip(o_refs, widths):
                part = jnp.dot(o_ref[rows, :], wo_ref[k0:k0 + kw, :], preferred_element_type=F32)
                y = part if y is None else y + part
                k0 += kw
            x1 = x_ref[rows, :] + g1_ref[...] * y
            x1_ref[rows, :] = x1
            h_ref[rows, :] = _norm_modulate(x1, nw_ref[...], sh_ref[...], sc_ref[...]).astype(BF16)
        y_ref[...] = mlp_partial()

    @pl.when(f > 0)
    def _():
        y_ref[...] += mlp_partial()

    @pl.when(f == pl.num_programs(2) - 1)
    def _():
        y = x1_ref[...] + g2_ref[...] * y_ref[...]
        if final_norm:
            y = y * lax.rsqrt(jnp.mean(y * y, axis=-1, keepdims=True) + NORM_EPS) * fw_ref[...]
        y_ref[...] = y


def _resid_mlp(x, o_parts, w_out, g1, nw, shift, scale, g2, w1, w2, *, tm, tf, final_w=None):
    b, s, d = x.shape
    dff = w1.shape[1]
    widths = tuple(o.shape[2] for o in o_parts)
    assert sum(widths) == w_out.shape[0]
    vec = pl.BlockSpec((None, 1, d), lambda bi, i, f: (bi, 0, 0))
    in_specs = [
        pl.BlockSpec((None, tm, d), lambda bi, i, f: (bi, i, 0)),
        pl.BlockSpec(w_out.shape, lambda bi, i, f: (0, 0), pipeline_mode=pl.Buffered(1)),
        vec,
        pl.BlockSpec((1, d), lambda bi, i, f: (0, 0)),
        vec, vec, vec,
        pl.BlockSpec((d, tf), lambda bi, i, f: (0, f)),
        pl.BlockSpec((tf, d), lambda bi, i, f: (f, 0)),
    ] + [pl.BlockSpec((None, tm, kw), lambda bi, i, f: (bi, i, 0)) for kw in widths]
    args = [x, w_out, g1, nw.reshape(1, d), shift, scale, g2, w1, w2, *o_parts]
    if final_w is not None:
        in_specs.append(pl.BlockSpec((1, d), lambda bi, i, f: (0, 0)))
        args.append(final_w.reshape(1, d))
    return pl.pallas_call(
        functools.partial(_resid_mlp_kernel, widths=widths, final_norm=final_w is not None),
        out_shape=jax.ShapeDtypeStruct(x.shape, x.dtype),
        grid=(b, s // tm, dff // tf),
        in_specs=in_specs,
        out_specs=pl.BlockSpec((None, tm, d), lambda bi, i, f: (bi, i, 0)),
        scratch_shapes=[pltpu.VMEM((tm, d), F32), pltpu.VMEM((tm, d), BF16)],
        compiler_params=_params("arbitrary", "arbitrary", "arbitrary"),
        name="resid_mlp",
    )(*args)


EVEN_SECTIONS = (
    (0, A_Q_HEADS, "q_norm"),
    (A_Q_HEADS * HEAD_DIM, A_KV_HEADS, "k_norm"),
    ((A_Q_HEADS + A_KV_HEADS) * HEAD_DIM, A_KV_HEADS, "v"),
    ((A_Q_HEADS + 2 * A_KV_HEADS) * HEAD_DIM, B_Q_HEADS, "q"),
    ((A_Q_HEADS + 2 * A_KV_HEADS + B_Q_HEADS) * HEAD_DIM, B_KV_HEADS, "k"),
    ((A_Q_HEADS + 2 * A_KV_HEADS + B_Q_HEADS + B_KV_HEADS) * HEAD_DIM, B_KV_HEADS, "v"),
)
LAT_TM = 512
MLP_TF = 1024


def kernel(x, c, ctx, c_ctx, ada_w, ada_b, norm_w, mlp_w1, mlp_w2, ev_w_in, ev_w_out, ev_q_norm,
           ev_k_norm, ev_sink, od_w_in, od_w_out, od_rpb, final_norm_w):
    b, s, d = x.shape
    n_ctx = ctx.shape[1]
    depth = ada_w.shape[0]
    rows = s // GRID_W

    mod = _ada_mod(jnp.concatenate([c, c_ctx[None, :]], axis=0), ada_w, ada_b)
    rope_tabs = _rope_tables(s)

    def mod_vecs(layer):
        lat = [mod[layer, :b, k * d:(k + 1) * d][:, None, :] for k in range(6)]
        cx = [jnp.broadcast_to(mod[layer, b:b + 1, k * d:(k + 1) * d][:, None, :], (b, 1, d))
              for k in range(6)]
        return lat, cx

    for layer in range(depth):
        need_ctx = layer < depth - 1
        (sh1, sc1, g1, sh2, sc2, g2), (sh1c, sc1c, g1c, sh2c, sc2c, g2c) = mod_vecs(layer)
        j = layer // 2
        w1 = _to_bf16(mlp_w1, layer)
        w2 = _to_bf16(mlp_w2, layer)
        if layer % 2 == 0:
            w_in = _to_bf16(ev_w_in, j)
            w_out = _to_bf16(ev_w_out, j)
            n_in = w_in.shape[1]
            norms = (ev_q_norm[j], ev_k_norm[j])
            qkv = _proj(x, norm_w[layer, 0], sh1, sc1, w_in, tm=LAT_TM, tn=n_in,
                        sections=EVEN_SECTIONS, head_norms=norms, rope_tabs=rope_tabs, name="proj_even")
            qkv_c = _proj(ctx, norm_w[layer, 0], sh1c, sc1c, w_in, tm=n_ctx, tn=n_in,
                          sections=EVEN_SECTIONS, head_norms=norms, name="proj_even_ctx")
            oa = _gattn(qkv, qkv_c, q_blk=0, k_blk=A_Q_HEADS, v_blk=A_Q_HEADS + A_KV_HEADS)
            sink_kg = ev_sink[j].reshape(B_KV_HEADS, GQA_GROUP).astype(F32)
            qb_blk = (A_Q_HEADS + 2 * A_KV_HEADS) // GQA_GROUP
            kb_blk = A_Q_HEADS + 2 * A_KV_HEADS + B_Q_HEADS
            ob = _wattn(sink_kg, qkv, qkv_c, q_blk=qb_blk, k_blk=kb_blk, v_blk=kb_blk + B_KV_HEADS)
            o_parts = [oa, ob]
            if need_ctx:
                sink_all = jnp.concatenate([jnp.full((A_KV_HEADS, GQA_GROUP), NEG, F32), sink_kg], axis=0)
                oc_parts = [_cattn(sink_all, qkv_c)]
        else:
            w_in = _to_bf16(od_w_in, j)
            w_out = _to_bf16(od_w_out, j)
            hd = C_HEADS * HEAD_DIM
            qkv = _proj(x, norm_w[layer, 0], sh1, sc1, w_in, tm=LAT_TM, tn=hd, name="proj_odd")
            kv_c = _proj(ctx, norm_w[layer, 0], sh1c, sc1c, w_in, tm=n_ctx, tn=hd, n_off=1,
                         n_out=2 * hd, name="proj_odd_ctx")
            o_parts = [_nattn(qkv, kv_c, _na_bias_tables(od_rpb[j], rows))]
            assert not need_ctx, "the odd layer is the last one at this depth"
        x = _resid_mlp(x, o_parts, w_out, g1, norm_w[layer, 1], sh2, sc2, g2, w1, w2, tm=LAT_TM,
                       tf=MLP_TF, final_w=None if need_ctx else final_norm_w)
        if need_ctx:
            ctx = _resid_mlp(ctx, oc_parts, w_out, g1c, norm_w[layer, 1], sh2c, sc2c, g2c, w1, w2,
                             tm=n_ctx, tf=MLP_TF)
    return x
```

```python
import functools

import jax
import jax.numpy as jnp
import numpy as np
from jax import lax
from jax.experimental import pallas as pl
from jax.experimental.pallas import tpu as pltpu

D_MODEL = 2048
GRID_W = 64
HEAD_DIM = 128
N_HEADS = D_MODEL // HEAD_DIM
A_Q_HEADS = N_HEADS // 2
A_KV_HEADS = max(1, A_Q_HEADS // 4)
B_Q_HEADS = N_HEADS - A_Q_HEADS
B_KV_HEADS = max(1, B_Q_HEADS // 4)
GQA_GROUP = A_Q_HEADS // A_KV_HEADS
C_HEADS = N_HEADS
WINDOW = 128
NA_KH = 8
NA_KW = 16
D_FF = 4 * D_MODEL
ROPE_THETA = 10000.0
ROPE_PAIRS = HEAD_DIM // 4
NORM_EPS = 1e-6
NEG = -1e30
ATTN_SCALE = HEAD_DIM ** -0.5

V7X_VMEM_BYTES = 64 * 1024 * 1024
VMEM_LIMIT = V7X_VMEM_BYTES - 6 * 1024 * 1024
LANES = 128
SUBLANES = 8

F32 = jnp.float32
BF16 = jnp.bfloat16


def _params(*sem):
    return pltpu.CompilerParams(dimension_semantics=sem, vmem_limit_bytes=VMEM_LIMIT)


ADA_ROWS = 3
ADA_TN = 1024


def _ada_kernel(cb_ref, w_ref, b_ref, o_ref, s_ref):
    @pl.when((pl.program_id(0) == 0) & (pl.program_id(1) == 0))
    def _():
        cvals = cb_ref[...]
        s_ref[...] = cvals * (1.0 / (1.0 + jnp.exp(-cvals)))

    nchunk = ADA_TN // LANES

    def body(kk, accs):
        k0 = pl.multiple_of(kk * SUBLANES, SUBLANES)
        w = w_ref[pl.ds(k0, SUBLANES), :]
        out = []
        for r in range(ADA_ROWS):
            s = s_ref[r, pl.ds(k0, SUBLANES), :]
            for cch in range(nchunk):
                out.append(accs[r * nchunk + cch] + w[:, cch * LANES:(cch + 1) * LANES] * s)
        return tuple(out)

    zeros = tuple(jnp.zeros((SUBLANES, LANES), F32) for _ in range(ADA_ROWS * nchunk))
    accs = lax.fori_loop(0, D_MODEL // SUBLANES, body, zeros, unroll=2)
    o_ref[...] = jnp.broadcast_to(b_ref[...], o_ref.shape)
    for r in range(ADA_ROWS):
        row = jnp.concatenate(
            [jnp.sum(accs[r * nchunk + cch], axis=0, keepdims=True) for cch in range(nchunk)], axis=1)
        o_ref[r:r + 1, :] = row + b_ref[...]


def _ada_mod(cc, ada_w, ada_b):
    depth, d, n = ada_w.shape
    cb = jnp.broadcast_to(cc[:, :, None], (ADA_ROWS, d, LANES))
    return pl.pallas_call(
        _ada_kernel,
        out_shape=jax.ShapeDtypeStruct((depth, SUBLANES, n), F32),
        grid=(depth, n // ADA_TN),
        in_specs=[
            pl.BlockSpec((ADA_ROWS, d, LANES), lambda l, j: (0, 0, 0)),
            pl.BlockSpec((None, d, ADA_TN), lambda l, j: (l, 0, j)),
            pl.BlockSpec((None, 1, ADA_TN), lambda l, j: (l, 0, j)),
        ],
        out_specs=pl.BlockSpec((None, SUBLANES, ADA_TN), lambda l, j: (l, 0, j)),
        scratch_shapes=[pltpu.VMEM((ADA_ROWS, d, LANES), F32)],
        compiler_params=_params("arbitrary", "arbitrary"),
        name="ada_mod",
    )(cb, ada_w, ada_b.reshape(depth, 1, n))


CAST_BLOCK_BYTES = 8 * 1024 * 1024


def _cast_kernel(w_ref, o_ref):
    o_ref[...] = w_ref[...].astype(o_ref.dtype)


def _to_bf16(w_stack, idx):
    _, r, c = w_stack.shape
    tr = r
    while tr * c * w_stack.dtype.itemsize > CAST_BLOCK_BYTES and tr % 32 == 0:
        tr //= 2
    return pl.pallas_call(
        _cast_kernel,
        out_shape=jax.ShapeDtypeStruct((r, c), BF16),
        grid=(r // tr,),
        in_specs=[pl.BlockSpec((None, tr, c), lambda i: (idx, i, 0))],
        out_specs=pl.BlockSpec((tr, c), lambda i: (i, 0)),
        compiler_params=_params("arbitrary"),
        name="to_bf16",
    )(w_stack)


def _norm_modulate(x, nw, shift, scale):
    ms = jnp.mean(x * x, axis=-1, keepdims=True)
    y = x * lax.rsqrt(ms + NORM_EPS) * nw
    return y * (1.0 + scale) + shift


def _rope(y, cos, sin_lo, sin_hi):
    return (y * cos + pltpu.roll(y, HEAD_DIM - ROPE_PAIRS, 1) * sin_lo
            + pltpu.roll(y, ROPE_PAIRS, 1) * sin_hi)


def _rope_tables(seq):
    t = jnp.arange(seq)
    row = (t // GRID_W).astype(F32)
    col = (t % GRID_W).astype(F32)
    inv = ROPE_THETA ** (-jnp.arange(ROPE_PAIRS, dtype=F32) / ROPE_PAIRS)
    ang_r = row[:, None] * inv
    ang_c = col[:, None] * inv
    ang = jnp.concatenate([ang_r, ang_r, ang_c, ang_c], axis=-1)
    cos, sin = jnp.cos(ang), jnp.sin(ang)
    first = (jnp.arange(HEAD_DIM) % (2 * ROPE_PAIRS)) < ROPE_PAIRS
    sin_lo = jnp.where(first, -sin, 0.0)
    sin_hi = jnp.where(first, 0.0, sin)
    return cos, sin_lo, sin_hi


PROJ_ROWS = 256


def _proj_kernel(*refs, sections, rope):
    x_ref, nw_ref, sh_ref, sc_ref, w_ref = refs[:5]
    rest = refs[5:]
    if sections is not None:
        qn_ref, kn_ref = rest[:2]
        rest = rest[2:]
    if rope:
        cos_ref, slo_ref, shi_ref = rest[:3]
        rest = rest[3:]
    o_ref = rest[0]
    h_ref = rest[1] if sections is None else None
    tm = x_ref.shape[0]

    def project(h, rows):
        if sections is None:
            o_ref[rows, :] = jnp.dot(h, w_ref[...], preferred_element_type=F32).astype(o_ref.dtype)
            return
        for c0, nh, kind in sections:
            y = jnp.dot(h, w_ref[:, c0:c0 + nh * HEAD_DIM], preferred_element_type=F32)
            for hh in range(nh):
                yh = y[:, hh * HEAD_DIM:(hh + 1) * HEAD_DIM]
                if kind in ("q_norm", "k_norm"):
                    hw = (qn_ref if kind == "q_norm" else kn_ref)[...]
                    yh = yh * lax.rsqrt(jnp.mean(yh * yh, axis=-1, keepdims=True) + NORM_EPS) * hw
                if rope and kind != "v":
                    yh = _rope(yh, cos_ref[rows, :], slo_ref[rows, :], shi_ref[rows, :])
                lo = c0 + hh * HEAD_DIM
                o_ref[rows, lo:lo + HEAD_DIM] = yh.astype(o_ref.dtype)

    @pl.when(pl.program_id(2) == 0)
    def _():
        for r0 in range(0, tm, PROJ_ROWS):
            rows = slice(r0, min(r0 + PROJ_ROWS, tm))
            h = _norm_modulate(x_ref[rows, :], nw_ref[...], sh_ref[...], sc_ref[...]).astype(BF16)
            if h_ref is not None:
                h_ref[rows, :] = h
            project(h, rows)

    if h_ref is not None:
        @pl.when(pl.program_id(2) > 0)
        def _():
            project(h_ref[...], slice(0, tm))


def _proj(x, nw, shift, scale, w, *, tm, tn, n_off=0, n_out=None, sections=None,
          head_norms=None, rope_tabs=None, name="proj"):
    b, s, d = x.shape
    n_out = w.shape[1] if n_out is None else n_out
    nj = n_out // tn
    in_specs = [
        pl.BlockSpec((None, tm, d), lambda bi, i, j: (bi, i, 0)),
        pl.BlockSpec((1, d), lambda bi, i, j: (0, 0)),
        pl.BlockSpec((None, 1, d), lambda bi, i, j: (bi, 0, 0)),
        pl.BlockSpec((None, 1, d), lambda bi, i, j: (bi, 0, 0)),
        pl.BlockSpec((d, tn), lambda bi, i, j: (0, j + n_off)),
    ]
    args = [x, nw.reshape(1, d), shift, scale, w]
    if sections is not None:
        assert nj == 1
        in_specs += [pl.BlockSpec((1, HEAD_DIM), lambda bi, i, j: (0, 0))] * 2
        args += [head_norms[0].reshape(1, HEAD_DIM), head_norms[1].reshape(1, HEAD_DIM)]
    if rope_tabs is not None:
        in_specs += [pl.BlockSpec((tm, HEAD_DIM), lambda bi, i, j: (i, 0))] * 3
        args += list(rope_tabs)
    return pl.pallas_call(
        functools.partial(_proj_kernel, sections=sections, rope=rope_tabs is not None),
        out_shape=jax.ShapeDtypeStruct((b, s, n_out), BF16),
        grid=(b, s // tm, nj),
        in_specs=in_specs,
        out_specs=pl.BlockSpec((None, tm, tn), lambda bi, i, j: (bi, i, j)),
        scratch_shapes=[pltpu.VMEM((tm, d), BF16)] if sections is None else [],
        compiler_params=_params("arbitrary", "arbitrary", "arbitrary"),
        name=name,
    )(*args)


def _qk(q, k):
    return lax.dot_general(q, k, (((1,), (1,)), ((), ())), preferred_element_type=F32)


EXP2_SCALE = ATTN_SCALE * float(np.log2(np.e))
INV_ATTN_SCALE = 1.0 / ATTN_SCALE


def _softmax_pv(parts, sink=None):
    m = functools.reduce(jnp.maximum, [jnp.max(s, axis=-1, keepdims=True) for s, _ in parts])
    if sink is not None:
        m = jnp.maximum(m, sink)
    l = None
    o = None
    for s, v in parts:
        p = jnp.exp2((s - m) * EXP2_SCALE)
        ls = jnp.sum(p, axis=-1, keepdims=True)
        os_ = jnp.dot(p.astype(BF16), v, preferred_element_type=F32)
        l = ls if l is None else l + ls
        o = os_ if o is None else o + os_
    if sink is not None:
        l = l + jnp.exp2((sink - m) * EXP2_SCALE)
    return o / l


def _run_chains(n_blk, n_inner, scores, probs, weighted_values, unroll=1):
    per_body = unroll * n_inner
    assert per_body % 2 == 0 and n_blk % unroll == 0

    def chain(it, n):
        return it * unroll + n // n_inner, n % n_inner

    scores(0, 0, 0)

    def body(it, carry):
        for n in range(per_body):
            blk, g = chain(it, n + 1)
            scores(jnp.minimum(blk, n_blk - 1) if n + 1 == per_body else blk, g, (n + 1) % 2)
            probs(n % n_inner, n % 2)
            blk, g = chain(it, n - 1)
            weighted_values(jnp.maximum(blk, 0) if n == 0 else blk, g, (n - 1) % 2)
        return carry

    lax.fori_loop(0, n_blk // unroll, body, 0)
    weighted_values(n_blk - 1, n_inner - 1, 1)


GATTN_ROWS = 512


def _gattn_kernel(q_ref, k_ref, v_ref, kc_ref, vc_ref, o_ref, kall_ref, vall_ref, s_ref, p_ref):
    c = kc_ref.shape[0]

    n_blk = q_ref.shape[0] // GATTN_ROWS

    kall_ref[:c, :] = kc_ref[...]
    kall_ref[c:, :] = k_ref[...]
    vall_ref[:c, :HEAD_DIM] = vc_ref[...]
    vall_ref[c:, :HEAD_DIM] = v_ref[...]
    lane = lax.broadcasted_iota(jnp.int32, (vall_ref.shape[0], HEAD_DIM), 1)
    vall_ref[:, HEAD_DIM:] = jnp.where(lane == 0, 1.0, 0.0).astype(BF16)

    def rows_of(blk):
        return pl.ds(pl.multiple_of(blk * GATTN_ROWS, GATTN_ROWS), GATTN_ROWS)

    def cols_of(g):
        return slice(g * HEAD_DIM, (g + 1) * HEAD_DIM)

    def scores(blk, g, slot):
        s_ref[slot] = _qk(q_ref[rows_of(blk), cols_of(g)], kall_ref[...])

    def probs(g, slot):
        m = jnp.max(s_ref[slot], axis=-1, keepdims=True)
        p_ref[slot] = jnp.exp2((s_ref[slot] - m) * EXP2_SCALE).astype(BF16)

    def weighted_values(blk, g, slot):
        ol = jnp.dot(p_ref[slot], vall_ref[...], preferred_element_type=F32)
        o = ol[:, :HEAD_DIM] / ol[:, HEAD_DIM:HEAD_DIM + 1]
        o_ref[rows_of(blk), cols_of(g)] = o.astype(o_ref.dtype)

    p_ref[1] = jnp.ones(p_ref.shape[1:], BF16)
    _run_chains(n_blk, GQA_GROUP, scores, probs, weighted_values)


def _gattn(qkv, qkv_c, *, q_blk, k_blk, v_blk):
    b, s, _ = qkv.shape
    c = qkv_c.shape[1]
    gw = GQA_GROUP * HEAD_DIM
    assert GQA_GROUP % 2 == 0 and s % GATTN_ROWS == 0
    single = pl.Buffered(1)
    return pl.pallas_call(
        _gattn_kernel,
        out_shape=jax.ShapeDtypeStruct((b, s, A_Q_HEADS * HEAD_DIM), BF16),
        grid=(b, A_KV_HEADS),
        in_specs=[
            pl.BlockSpec((None, s, gw), lambda bi, kv: (bi, 0, q_blk + kv), pipeline_mode=single),
            pl.BlockSpec((None, s, HEAD_DIM), lambda bi, kv: (bi, 0, k_blk + kv), pipeline_mode=single),
            pl.BlockSpec((None, s, HEAD_DIM), lambda bi, kv: (bi, 0, v_blk + kv), pipeline_mode=single),
            pl.BlockSpec((None, c, HEAD_DIM), lambda bi, kv: (bi, 0, k_blk + kv)),
            pl.BlockSpec((None, c, HEAD_DIM), lambda bi, kv: (bi, 0, v_blk + kv)),
        ],
        out_specs=pl.BlockSpec((None, s, gw), lambda bi, kv: (bi, 0, kv)),
        scratch_shapes=[
            pltpu.VMEM((c + s, HEAD_DIM), BF16),
            pltpu.VMEM((c + s, 2 * HEAD_DIM), BF16),
            pltpu.VMEM((2, GATTN_ROWS, c + s), F32),
            pltpu.VMEM((2, GATTN_ROWS, c + s), BF16),
        ],
        compiler_params=_params("arbitrary", "arbitrary"),
        name="gattn",
    )(qkv, qkv, qkv, qkv_c, qkv_c)


WATTN_TQ = 4 * WINDOW
WATTN_NK = WATTN_TQ + 2 * WINDOW


def _wattn_kernel(sink_ref, q_ref, k_ref, v_ref, kc_ref, vc_ref, o_ref,
                  mask_ref, s_ref, sc_ref, p_ref, pc_ref, l_ref):
    kv = pl.program_id(1)
    seq = q_ref.shape[0]
    n_tile = seq // WATTN_TQ

    def rows_of(tile):
        return pl.ds(pl.multiple_of(tile * WATTN_TQ, WATTN_TQ), WATTN_TQ)

    def key0_of(tile):
        lo, hi = 0, seq - WATTN_NK
        k0 = tile * WATTN_TQ - WINDOW
        return min(max(k0, lo), hi) if isinstance(tile, int) else jnp.clip(k0, lo, hi)

    def key_rows_of(tile):
        return pl.ds(pl.multiple_of(key0_of(tile), WINDOW), WATTN_NK)

    def cols_of(g):
        return slice(g * HEAD_DIM, (g + 1) * HEAD_DIM)

    for var, tile in enumerate((0, 1, n_tile - 1)):
        shift = key0_of(tile) - tile * WATTN_TQ
        dist = (shift + lax.broadcasted_iota(jnp.int32, (WATTN_TQ, WATTN_NK), 1)
                - lax.broadcasted_iota(jnp.int32, (WATTN_TQ, WATTN_NK), 0))
        mask_ref[var] = jnp.where(jnp.abs(dist) <= WINDOW, 0.0, NEG)

    def scores(tile, g, slot):
        q = q_ref[rows_of(tile), cols_of(g)]
        var = jnp.where(tile == 0, 0, jnp.where(tile == n_tile - 1, 2, 1))
        s_ref[slot] = _qk(q, k_ref[key_rows_of(tile), :]) + mask_ref[var]
        sc_ref[slot] = _qk(q, kc_ref[...])

    def probs(g, slot):
        s, sc = s_ref[slot], sc_ref[slot]
        sink = sink_ref[kv, g] * INV_ATTN_SCALE
        m = jnp.maximum(jnp.max(s, axis=-1, keepdims=True), jnp.max(sc, axis=-1, keepdims=True))
        m = jnp.maximum(m, sink)
        p = jnp.exp2((s - m) * EXP2_SCALE)
        pc = jnp.exp2((sc - m) * EXP2_SCALE)
        l_ref[slot] = (jnp.sum(p, axis=-1, keepdims=True) + jnp.sum(pc, axis=-1, keepdims=True)
                       + jnp.exp2((sink - m) * EXP2_SCALE))
        p_ref[slot] = p.astype(BF16)
        pc_ref[slot] = pc.astype(BF16)

    def weighted_values(tile, g, slot):
        o = (jnp.dot(p_ref[slot], v_ref[key_rows_of(tile), :], preferred_element_type=F32)
             + jnp.dot(pc_ref[slot], vc_ref[...], preferred_element_type=F32))
        o_ref[rows_of(tile), cols_of(g)] = (o / l_ref[slot]).astype(o_ref.dtype)

    p_ref[1] = jnp.ones(p_ref.shape[1:], BF16)
    pc_ref[1] = jnp.ones(pc_ref.shape[1:], BF16)
    l_ref[1] = jnp.ones(l_ref.shape[1:], F32)
    _run_chains(seq // WATTN_TQ, GQA_GROUP, scores, probs, weighted_values)


def _wattn(sink_kg, qkv, qkv_c, *, q_blk, k_blk, v_blk):
    b, s, _ = qkv.shape
    c = qkv_c.shape[1]
    gw = GQA_GROUP * HEAD_DIM
    assert s % WATTN_TQ == 0 and s >= 3 * WATTN_TQ

    def col_map(col):
        return lambda bi, kv: (bi, 0, col + kv)

    return pl.pallas_call(
        _wattn_kernel,
        out_shape=jax.ShapeDtypeStruct((b, s, B_Q_HEADS * HEAD_DIM), BF16),
        grid=(b, B_KV_HEADS),
        in_specs=[
            pl.BlockSpec(memory_space=pltpu.SMEM),
            pl.BlockSpec((None, s, gw), col_map(q_blk)),
            pl.BlockSpec((None, s, HEAD_DIM), col_map(k_blk)),
            pl.BlockSpec((None, s, HEAD_DIM), col_map(v_blk)),
            pl.BlockSpec((None, c, HEAD_DIM), col_map(k_blk)),
            pl.BlockSpec((None, c, HEAD_DIM), col_map(v_blk)),
        ],
        out_specs=pl.BlockSpec((None, s, gw), col_map(0)),
        scratch_shapes=[
            pltpu.VMEM((3, WATTN_TQ, WATTN_NK), F32),
            pltpu.VMEM((2, WATTN_TQ, WATTN_NK), F32),
            pltpu.VMEM((2, WATTN_TQ, c), F32),
            pltpu.VMEM((2, WATTN_TQ, WATTN_NK), BF16),
            pltpu.VMEM((2, WATTN_TQ, c), BF16),
            pltpu.VMEM((2, WATTN_TQ, 1), F32),
        ],
        compiler_params=_params("arbitrary", "arbitrary"),
        name="wattn",
    )(sink_kg, qkv, qkv, qkv, qkv_c, qkv_c)


def _cattn_kernel(sink_ref, q_ref, k_ref, v_ref, o_ref):
    grp = pl.program_id(1)
    k, v = k_ref[...], v_ref[...]
    for g in range(GQA_GROUP):
        q = q_ref[:, g * HEAD_DIM:(g + 1) * HEAD_DIM]
        o = _softmax_pv([(_qk(q, k), v)], sink=sink_ref[grp, g] * INV_ATTN_SCALE)
        o_ref[:, g * HEAD_DIM:(g + 1) * HEAD_DIM] = o.astype(o_ref.dtype)


def _cattn(sink_all, qkv_c):
    b, c, _ = qkv_c.shape
    gw = GQA_GROUP * HEAD_DIM
    n_grp = A_KV_HEADS + B_KV_HEADS
    q_map = lambda bi, g: (bi, 0, g + g // A_KV_HEADS)
    k_map = lambda bi, g: (bi, 0, 8 + g + 10 * (g // A_KV_HEADS))
    v_map = lambda bi, g: (bi, 0, 10 + g + 10 * (g // A_KV_HEADS))
    return pl.pallas_call(
        _cattn_kernel,
        out_shape=jax.ShapeDtypeStruct((b, c, N_HEADS * HEAD_DIM), BF16),
        grid=(b, n_grp),
        in_specs=[
            pl.BlockSpec(memory_space=pltpu.SMEM),
            pl.BlockSpec((None, c, gw), q_map),
            pl.BlockSpec((None, c, HEAD_DIM), k_map),
            pl.BlockSpec((None, c, HEAD_DIM), v_map),
        ],
        out_specs=pl.BlockSpec((None, c, gw), lambda bi, g: (bi, 0, g)),
        compiler_params=_params("arbitrary", "arbitrary"),
        name="cattn",
    )(sink_all, qkv_c, qkv_c, qkv_c)


NATTN_ROWS = 4
NATTN_TQ = NATTN_ROWS * GRID_W
NATTN_KROWS = 3 * NATTN_ROWS


NATTN_HG = 2
NATTN_UNROLL = 2
N_RI = 2 * NA_KH - 1
N_CI = 2 * NA_KW - 1


def _na_key_row0(tile, rows):
    lo, hi = 0, rows - NATTN_KROWS
    r0 = (tile - 1) * NATTN_ROWS
    if isinstance(tile, int):
        return min(max(r0, lo), hi)
    return jnp.clip(r0, lo, hi)


def _nattn_kernel(q_ref, k_ref, v_ref, kc_ref, vc_ref, bias_ref, o_ref,
                  vs_ref, vcs_ref, s_ref, sc_ref, p_ref, pc_ref):
    seq = q_ref.shape[0]
    c = kc_ref.shape[0]
    n_tile = seq // NATTN_TQ
    nk = NATTN_KROWS * GRID_W

    def rows_of(tile):
        return pl.ds(pl.multiple_of(tile * NATTN_TQ, NATTN_TQ), NATTN_TQ)

    def key_rows_of(tile):
        return pl.ds(pl.multiple_of(_na_key_row0(tile, seq // GRID_W) * GRID_W, NATTN_TQ), nk)

    def cols_of(g):
        return slice(g * HEAD_DIM, (g + 1) * HEAD_DIM)

    def ones_col(n):
        return jnp.where(lax.broadcasted_iota(jnp.int32, (n, HEAD_DIM), 1) == 0, 1.0, 0.0).astype(BF16)

    for g in range(NATTN_HG):
        vs_ref[g] = jnp.concatenate([v_ref[:, cols_of(g)], ones_col(seq)], axis=1)
        vcs_ref[g] = jnp.concatenate([vc_ref[:, cols_of(g)], ones_col(c)], axis=1)

    def scores(tile, _, slot):
        var = jnp.where(tile == 0, 0, jnp.where(tile == n_tile - 1, 2, 1))
        for g in range(NATTN_HG):
            q = q_ref[rows_of(tile), cols_of(g)]
            s_ref[slot, g] = _qk(q, k_ref[key_rows_of(tile), cols_of(g)]) + bias_ref[g, var]
            sc_ref[slot, g] = _qk(q, kc_ref[:, cols_of(g)])

    def probs(_, slot):
        for g in range(NATTN_HG):
            s, sc = s_ref[slot, g], sc_ref[slot, g]
            folded = jnp.concatenate([jnp.maximum(s[:, :c], sc), s[:, c:]], axis=1)
            m = jnp.max(folded, axis=-1, keepdims=True)
            p_ref[slot, g] = jnp.exp2((s - m) * EXP2_SCALE).astype(BF16)
            pc_ref[slot, g] = jnp.exp2((sc - m) * EXP2_SCALE).astype(BF16)

    def weighted_values(tile, _, slot):
        for g in range(NATTN_HG):
            ol = (jnp.dot(p_ref[slot, g], vs_ref[g, key_rows_of(tile), :], preferred_element_type=F32)
                  + jnp.dot(pc_ref[slot, g], vcs_ref[g], preferred_element_type=F32))
            o = ol[:, :HEAD_DIM] / ol[:, HEAD_DIM:HEAD_DIM + 1]
            o_ref[rows_of(tile), cols_of(g)] = o.astype(o_ref.dtype)

    p_ref[1] = jnp.ones(p_ref.shape[1:], BF16)
    pc_ref[1] = jnp.ones(pc_ref.shape[1:], BF16)
    _run_chains(n_tile, 1, scores, probs, weighted_values, unroll=NATTN_UNROLL)


def _na_bias_kernel(rpb_ref, o_ref, u_ref, *, rows):
    base = pl.program_id(0) * (N_RI * N_CI)
    kh = min(NA_KH, rows)

    def fill(sg, carry):
        q0 = pl.multiple_of(sg * SUBLANES, SUBLANES)
        qc = q0 + lax.broadcasted_iota(jnp.int32, (SUBLANES, LANES), 0)
        kc = lax.broadcasted_iota(jnp.int32, (SUBLANES, LANES), 1) & (GRID_W - 1)
        diag = kc - qc + (NA_KW - 1)
        us = [jnp.zeros((SUBLANES, LANES), F32) for _ in range(N_RI)]
        for ci in range(N_CI):
            hit = diag == ci
            for a in range(N_RI):
                us[a] = jnp.where(hit, rpb_ref[base + a * N_CI + ci], us[a])
        for a in range(N_RI):
            u_ref[a, pl.ds(q0, SUBLANES), :] = us[a] * INV_ATTN_SCALE
        return carry

    lax.fori_loop(0, GRID_W // SUBLANES, fill, 0)

    qc = lax.broadcasted_iota(jnp.int32, (GRID_W, LANES), 0)
    lane = lax.broadcasted_iota(jnp.int32, (GRID_W, LANES), 1)
    kc = lane & (GRID_W - 1)
    cs = jnp.clip(qc - NA_KW // 2, 0, GRID_W - NA_KW)
    col_valid = (kc >= cs) & (kc < cs + NA_KW)
    low = lane < GRID_W
    masks = {(True, True): col_valid, (True, False): col_valid & low,
             (False, True): col_valid & jnp.logical_not(low)}
    neg_blk = jnp.full((GRID_W, LANES), NEG, F32)
    n_tile = rows // NATTN_ROWS
    for var, tile in enumerate((0, 1, n_tile - 1)):
        r0 = tile * NATTN_ROWS
        k0 = _na_key_row0(tile, rows)
        for qr in range(NATTN_ROWS):
            rs = min(max(r0 + qr - kh // 2, 0), rows - kh)
            for jj in range(NATTN_KROWS // 2):
                kr = k0 + 2 * jj
                ok = (rs <= kr < rs + kh, rs <= kr + 1 < rs + kh)
                if ok == (False, False):
                    blk = neg_blk
                else:
                    a = kr - (r0 + qr) + NA_KH - 1
                    a_lo = min(max(a, 0), N_RI - 1)
                    a_hi = min(max(a + 1, 0), N_RI - 1)
                    pair = jnp.where(low, u_ref[a_lo], u_ref[a_hi])
                    blk = jnp.where(masks[ok], pair, NEG)
                o_ref[var, qr * GRID_W:(qr + 1) * GRID_W, jj * LANES:(jj + 1) * LANES] = blk


def _na_bias_tables(rpb, rows):
    h = rpb.shape[0]
    assert 2 * GRID_W == LANES and rows >= 3 * NATTN_ROWS and NATTN_ROWS == NA_KH // 2
    return pl.pallas_call(
        functools.partial(_na_bias_kernel, rows=rows),
        out_shape=jax.ShapeDtypeStruct((h, 3, NATTN_TQ, NATTN_KROWS * GRID_W), F32),
        grid=(h,),
        in_specs=[pl.BlockSpec(memory_space=pltpu.SMEM)],
        out_specs=pl.BlockSpec((None, 3, NATTN_TQ, NATTN_KROWS * GRID_W), lambda hh: (hh, 0, 0, 0)),
        scratch_shapes=[pltpu.VMEM((N_RI, GRID_W, LANES), F32)],
        compiler_params=_params("arbitrary"),
        name="na_bias",
    )(rpb.reshape(-1))


def _nattn(qkv, kv_c, bias):
    b, s, _ = qkv.shape
    c = kv_c.shape[1]
    h = C_HEADS
    ng = h // NATTN_HG
    gw = NATTN_HG * HEAD_DIM
    nk = NATTN_KROWS * GRID_W
    assert NATTN_HG % 2 == 0 and s % NATTN_TQ == 0

    def col_map(col):
        return lambda hg, bi: (bi, 0, col + hg)

    return pl.pallas_call(
        _nattn_kernel,
        out_shape=jax.ShapeDtypeStruct((b, s, h * HEAD_DIM), BF16),
        grid=(ng, b),
        in_specs=[
            pl.BlockSpec((None, s, gw), col_map(0)),
            pl.BlockSpec((None, s, gw), col_map(ng)),
            pl.BlockSpec((None, s, gw), col_map(2 * ng)),
            pl.BlockSpec((None, c, gw), col_map(0)),
            pl.BlockSpec((None, c, gw), col_map(ng)),
            pl.BlockSpec((NATTN_HG, 3, NATTN_TQ, nk), lambda hg, bi: (hg, 0, 0, 0),
                         pipeline_mode=pl.Buffered(1)),
        ],
        out_specs=pl.BlockSpec((None, s, gw), col_map(0)),
        scratch_shapes=[
            pltpu.VMEM((NATTN_HG, s, 2 * HEAD_DIM), BF16),
            pltpu.VMEM((NATTN_HG, c, 2 * HEAD_DIM), BF16),
            pltpu.VMEM((2, NATTN_HG, NATTN_TQ, nk), F32),
            pltpu.VMEM((2, NATTN_HG, NATTN_TQ, c), F32),
            pltpu.VMEM((2, NATTN_HG, NATTN_TQ, nk), BF16),
            pltpu.VMEM((2, NATTN_HG, NATTN_TQ, c), BF16),
        ],
        compiler_params=_params("arbitrary", "arbitrary"),
        name="nattn",
    )(qkv, qkv, qkv, kv_c, kv_c, bias)


OUTPROJ_ROWS = 256


def _resid_mlp_kernel(*refs, widths, final_norm):
    x_ref, wo_ref, g1_ref, nw_ref, sh_ref, sc_ref, g2_ref, w1_ref, w2_ref = refs[:9]
    rest = refs[9:]
    o_refs = rest[:len(widths)]
    rest = rest[len(widths):]
    if final_norm:
        fw_ref = rest[0]
        rest = rest[1:]
    y_ref, x1_ref, h_ref = rest
    f = pl.program_id(2)
    tm = x_ref.shape[0]

    def mlp_partial():
        a = jnp.dot(h_ref[...], w1_ref[...], preferred_element_type=F32)
        a = jnp.square(jnp.maximum(a, 0.0)).astype(BF16)
        return jnp.dot(a, w2_ref[...], preferred_element_type=F32)

    @pl.when(f == 0)
    def _():
        for r0 in range(0, tm, OUTPROJ_ROWS):
            rows = slice(r0, min(r0 + OUTPROJ_ROWS, tm))
            y = None
            k0 = 0
            for o_ref, kw in zip(o_refs, widths):
                part = jnp.dot(o_ref[rows, :], wo_ref[k0:k0 + kw, :], preferred_element_type=F32)
                y = part if y is None else y + part
                k0 += kw
            x1 = x_ref[rows, :] + g1_ref[...] * y
            x1_ref[rows, :] = x1
            h_ref[rows, :] = _norm_modulate(x1, nw_ref[...], sh_ref[...], sc_ref[...]).astype(BF16)
        y_ref[...] = mlp_partial()

    @pl.when(f > 0)
    def _():
        y_ref[...] += mlp_partial()

    @pl.when(f == pl.num_programs(2) - 1)
    def _():
        y = x1_ref[...] + g2_ref[...] * y_ref[...]
        if final_norm:
            y = y * lax.rsqrt(jnp.mean(y * y, axis=-1, keepdims=True) + NORM_EPS) * fw_ref[...]
        y_ref[...] = y


def _resid_mlp(x, o_parts, w_out, g1, nw, shift, scale, g2, w1, w2, *, tm, tf, final_w=None):
    b, s, d = x.shape
    dff = w1.shape[1]
    widths = tuple(o.shape[2] for o in o_parts)
    assert sum(widths) == w_out.shape[0]
    vec = pl.BlockSpec((None, 1, d), lambda bi, i, f: (bi, 0, 0))
    in_specs = [
        pl.BlockSpec((None, tm, d), lambda bi, i, f: (bi, i, 0)),
        pl.BlockSpec(w_out.shape, lambda bi, i, f: (0, 0), pipeline_mode=pl.Buffered(1)),
        vec,
        pl.BlockSpec((1, d), lambda bi, i, f: (0, 0)),
        vec, vec, vec,
        pl.BlockSpec((d, tf), lambda bi, i, f: (0, f)),
        pl.BlockSpec((tf, d), lambda bi, i, f: (f, 0)),
    ] + [pl.BlockSpec((None, tm, kw), lambda bi, i, f: (bi, i, 0)) for kw in widths]
    args = [x, w_out, g1, nw.reshape(1, d), shift, scale, g2, w1, w2, *o_parts]
    if final_w is not None:
        in_specs.append(pl.BlockSpec((1, d), lambda bi, i, f: (0, 0)))
        args.append(final_w.reshape(1, d))
    return pl.pallas_call(
        functools.partial(_resid_mlp_kernel, widths=widths, final_norm=final_w is not None),
        out_shape=jax.ShapeDtypeStruct(x.shape, x.dtype),
        grid=(b, s // tm, dff // tf),
        in_specs=in_specs,
        out_specs=pl.BlockSpec((None, tm, d), lambda bi, i, f: (bi, i, 0)),
        scratch_shapes=[pltpu.VMEM((tm, d), F32), pltpu.VMEM((tm, d), BF16)],
        compiler_params=_params("arbitrary", "arbitrary", "arbitrary"),
        name="resid_mlp",
    )(*args)


EVEN_SECTIONS = (
    (0, A_Q_HEADS, "q_norm"),
    (A_Q_HEADS * HEAD_DIM, A_KV_HEADS, "k_norm"),
    ((A_Q_HEADS + A_KV_HEADS) * HEAD_DIM, A_KV_HEADS, "v"),
    ((A_Q_HEADS + 2 * A_KV_HEADS) * HEAD_DIM, B_Q_HEADS, "q"),
    ((A_Q_HEADS + 2 * A_KV_HEADS + B_Q_HEADS) * HEAD_DIM, B_KV_HEADS, "k"),
    ((A_Q_HEADS + 2 * A_KV_HEADS + B_Q_HEADS + B_KV_HEADS) * HEAD_DIM, B_KV_HEADS, "v"),
)
LAT_TM = 512
MLP_TF = 1024


def kernel(x, c, ctx, c_ctx, ada_w, ada_b, norm_w, mlp_w1, mlp_w2, ev_w_in, ev_w_out, ev_q_norm,
           ev_k_norm, ev_sink, od_w_in, od_w_out, od_rpb, final_norm_w):
    b, s, d = x.shape
    n_ctx = ctx.shape[1]
    depth = ada_w.shape[0]
    rows = s // GRID_W

    mod = _ada_mod(jnp.concatenate([c, c_ctx[None, :]], axis=0), ada_w, ada_b)
    rope_tabs = _rope_tables(s)

    def mod_vecs(layer):
        lat = [mod[layer, :b, k * d:(k + 1) * d][:, None, :] for k in range(6)]
        cx = [jnp.broadcast_to(mod[layer, b:b + 1, k * d:(k + 1) * d][:, None, :], (b, 1, d))
              for k in range(6)]
        return lat, cx

    for layer in range(depth):
        need_ctx = layer < depth - 1
        (sh1, sc1, g1, sh2, sc2, g2), (sh1c, sc1c, g1c, sh2c, sc2c, g2c) = mod_vecs(layer)
        j = layer // 2
        w1 = _to_bf16(mlp_w1, layer)
        w2 = _to_bf16(mlp_w2, layer)
        if layer % 2 == 0:
            w_in = _to_bf16(ev_w_in, j)
            w_out = _to_bf16(ev_w_out, j)
            n_in = w_in.shape[1]
            norms = (ev_q_norm[j], ev_k_norm[j])
            qkv = _proj(x, norm_w[layer, 0], sh1, sc1, w_in, tm=LAT_TM, tn=n_in,
                        sections=EVEN_SECTIONS, head_norms=norms, rope_tabs=rope_tabs, name="proj_even")
            qkv_c = _proj(ctx, norm_w[layer, 0], sh1c, sc1c, w_in, tm=n_ctx, tn=n_in,
                          sections=EVEN_SECTIONS, head_norms=norms, name="proj_even_ctx")
            oa = _gattn(qkv, qkv_c, q_blk=0, k_blk=A_Q_HEADS, v_blk=A_Q_HEADS + A_KV_HEADS)
            sink_kg = ev_sink[j].reshape(B_KV_HEADS, GQA_GROUP).astype(F32)
            qb_blk = (A_Q_HEADS + 2 * A_KV_HEADS) // GQA_GROUP
            kb_blk = A_Q_HEADS + 2 * A_KV_HEADS + B_Q_HEADS
            ob = _wattn(sink_kg, qkv, qkv_c, q_blk=qb_blk, k_blk=kb_blk, v_blk=kb_blk + B_KV_HEADS)
            o_parts = [oa, ob]
            if need_ctx:
                sink_all = jnp.concatenate([jnp.full((A_KV_HEADS, GQA_GROUP), NEG, F32), sink_kg], axis=0)
                oc_parts = [_cattn(sink_all, qkv_c)]
        else:
            w_in = _to_bf16(od_w_in, j)
            w_out = _to_bf16(od_w_out, j)
            hd = C_HEADS * HEAD_DIM
            qkv = _proj(x, norm_w[layer, 0], sh1, sc1, w_in, tm=LAT_TM, tn=hd, name="proj_odd")
            kv_c = _proj(ctx, norm_w[layer, 0], sh1c, sc1c, w_in, tm=n_ctx, tn=hd, n_off=1,
                         n_out=2 * hd, name="proj_odd_ctx")
            o_parts = [_nattn(qkv, kv_c, _na_bias_tables(od_rpb[j], rows))]
            assert not need_ctx, "the odd layer is the last one at this depth"
        x = _resid_mlp(x, o_parts, w_out, g1, norm_w[layer, 1], sh2, sc2, g2, w1, w2, tm=LAT_TM,
                       tf=MLP_TF, final_w=None if need_ctx else final_norm_w)
        if need_ctx:
            ctx = _resid_mlp(ctx, oc_parts, w_out, g1c, norm_w[layer, 1], sh2c, sc2c, g2c, w1, w2,
                             tm=n_ctx, tf=MLP_TF)
    return x
```

```python
import functools

import jax
import jax.numpy as jnp
import numpy as np
from jax import lax
from jax.experimental import pallas as pl
from jax.experimental.pallas import tpu as pltpu

D_MODEL = 2048
GRID_W = 64
HEAD_DIM = 128
N_HEADS = D_MODEL // HEAD_DIM
A_Q_HEADS = N_HEADS // 2
A_KV_HEADS = max(1, A_Q_HEADS // 4)
B_Q_HEADS = N_HEADS - A_Q_HEADS
B_KV_HEADS = max(1, B_Q_HEADS // 4)
GQA_GROUP = A_Q_HEADS // A_KV_HEADS
C_HEADS = N_HEADS
WINDOW = 128
NA_KH = 8
NA_KW = 16
D_FF = 4 * D_MODEL
ROPE_THETA = 10000.0
ROPE_PAIRS = HEAD_DIM // 4
NORM_EPS = 1e-6
NEG = -1e30
ATTN_SCALE = HEAD_DIM ** -0.5

V7X_VMEM_BYTES = 64 * 1024 * 1024
VMEM_LIMIT = V7X_VMEM_BYTES - 6 * 1024 * 1024
LANES = 128
SUBLANES = 8

F32 = jnp.float32
BF16 = jnp.bfloat16


def _params(*sem):
    return pltpu.CompilerParams(dimension_semantics=sem, vmem_limit_bytes=VMEM_LIMIT)


ADA_ROWS = 3
ADA_TN = 1024


def _ada_kernel(cb_ref, w_ref, b_ref, o_ref, s_ref):
    @pl.when((pl.program_id(0) == 0) & (pl.program_id(1) == 0))
    def _():
        cvals = cb_ref[...]
        s_ref[...] = cvals * (1.0 / (1.0 + jnp.exp(-cvals)))

    nchunk = ADA_TN // LANES

    def body(kk, accs):
        k0 = pl.multiple_of(kk * SUBLANES, SUBLANES)
        w = w_ref[pl.ds(k0, SUBLANES), :]
        out = []
        for r in range(ADA_ROWS):
            s = s_ref[r, pl.ds(k0, SUBLANES), :]
            for cch in range(nchunk):
                out.append(accs[r * nchunk + cch] + w[:, cch * LANES:(cch + 1) * LANES] * s)
        return tuple(out)

    zeros = tuple(jnp.zeros((SUBLANES, LANES), F32) for _ in range(ADA_ROWS * nchunk))
    accs = lax.fori_loop(0, D_MODEL // SUBLANES, body, zeros, unroll=4)
    o_ref[...] = jnp.broadcast_to(b_ref[...], o_ref.shape)
    for r in range(ADA_ROWS):
        row = jnp.concatenate(
            [jnp.sum(accs[r * nchunk + cch], axis=0, keepdims=True) for cch in range(nchunk)], axis=1)
        o_ref[r:r + 1, :] = row + b_ref[...]


def _ada_mod(cc, ada_w, ada_b):
    depth, d, n = ada_w.shape
    cb = jnp.broadcast_to(cc[:, :, None], (ADA_ROWS, d, LANES))
    return pl.pallas_call(
        _ada_kernel,
        out_shape=jax.ShapeDtypeStruct((depth, SUBLANES, n), F32),
        grid=(depth, n // ADA_TN),
        in_specs=[
            pl.BlockSpec((ADA_ROWS, d, LANES), lambda l, j: (0, 0, 0)),
            pl.BlockSpec((None, d, ADA_TN), lambda l, j: (l, 0, j)),
            pl.BlockSpec((None, 1, ADA_TN), lambda l, j: (l, 0, j)),
        ],
        out_specs=pl.BlockSpec((None, SUBLANES, ADA_TN), lambda l, j: (l, 0, j)),
        scratch_shapes=[pltpu.VMEM((ADA_ROWS, d, LANES), F32)],
        compiler_params=_params("arbitrary", "arbitrary"),
        name="ada_mod",
    )(cb, ada_w, ada_b.reshape(depth, 1, n))


CAST_BLOCK_BYTES = 8 * 1024 * 1024


def _cast_kernel(w_ref, o_ref):
    o_ref[...] = w_ref[...].astype(o_ref.dtype)


def _to_bf16(w_stack, idx):
    _, r, c = w_stack.shape
    tr = r
    while tr * c * w_stack.dtype.itemsize > CAST_BLOCK_BYTES and tr % 32 == 0:
        tr //= 2
    return pl.pallas_call(
        _cast_kernel,
        out_shape=jax.ShapeDtypeStruct((r, c), BF16),
        grid=(r // tr,),
        in_specs=[pl.BlockSpec((None, tr, c), lambda i: (idx, i, 0))],
        out_specs=pl.BlockSpec((tr, c), lambda i: (i, 0)),
        compiler_params=_params("arbitrary"),
        name="to_bf16",
    )(w_stack)


def _norm_modulate(x, nw, shift, scale):
    ms = jnp.mean(x * x, axis=-1, keepdims=True)
    y = x * lax.rsqrt(ms + NORM_EPS) * nw
    return y * (1.0 + scale) + shift


def _rope(y, cos, sin_lo, sin_hi):
    return (y * cos + pltpu.roll(y, HEAD_DIM - ROPE_PAIRS, 1) * sin_lo
            + pltpu.roll(y, ROPE_PAIRS, 1) * sin_hi)


def _rope_tables(seq):
    t = jnp.arange(seq)
    row = (t // GRID_W).astype(F32)
    col = (t % GRID_W).astype(F32)
    inv = ROPE_THETA ** (-jnp.arange(ROPE_PAIRS, dtype=F32) / ROPE_PAIRS)
    ang_r = row[:, None] * inv
    ang_c = col[:, None] * inv
    ang = jnp.concatenate([ang_r, ang_r, ang_c, ang_c], axis=-1)
    cos, sin = jnp.cos(ang), jnp.sin(ang)
    first = (jnp.arange(HEAD_DIM) % (2 * ROPE_PAIRS)) < ROPE_PAIRS
    sin_lo = jnp.where(first, -sin, 0.0)
    sin_hi = jnp.where(first, 0.0, sin)
    return cos, sin_lo, sin_hi


PROJ_ROWS = 256


def _proj_kernel(*refs, sections, rope):
    x_ref, nw_ref, sh_ref, sc_ref, w_ref = refs[:5]
    rest = refs[5:]
    if sections is not None:
        qn_ref, kn_ref = rest[:2]
        rest = rest[2:]
    if rope:
        cos_ref, slo_ref, shi_ref = rest[:3]
        rest = rest[3:]
    o_ref = rest[0]
    h_ref = rest[1] if sections is None else None
    tm = x_ref.shape[0]

    def project(h, rows):
        if sections is None:
            o_ref[rows, :] = jnp.dot(h, w_ref[...], preferred_element_type=F32).astype(o_ref.dtype)
            return
        for c0, nh, kind in sections:
            y = jnp.dot(h, w_ref[:, c0:c0 + nh * HEAD_DIM], preferred_element_type=F32)
            for hh in range(nh):
                yh = y[:, hh * HEAD_DIM:(hh + 1) * HEAD_DIM]
                if kind in ("q_norm", "k_norm"):
                    hw = (qn_ref if kind == "q_norm" else kn_ref)[...]
                    yh = yh * lax.rsqrt(jnp.mean(yh * yh, axis=-1, keepdims=True) + NORM_EPS) * hw
                if rope and kind != "v":
                    yh = _rope(yh, cos_ref[rows, :], slo_ref[rows, :], shi_ref[rows, :])
                lo = c0 + hh * HEAD_DIM
                o_ref[rows, lo:lo + HEAD_DIM] = yh.astype(o_ref.dtype)

    @pl.when(pl.program_id(2) == 0)
    def _():
        for r0 in range(0, tm, PROJ_ROWS):
            rows = slice(r0, min(r0 + PROJ_ROWS, tm))
            h = _norm_modulate(x_ref[rows, :], nw_ref[...], sh_ref[...], sc_ref[...]).astype(BF16)
            if h_ref is not None:
                h_ref[rows, :] = h
            project(h, rows)

    if h_ref is not None:
        @pl.when(pl.program_id(2) > 0)
        def _():
            project(h_ref[...], slice(0, tm))


def _proj(x, nw, shift, scale, w, *, tm, tn, n_off=0, n_out=None, sections=None,
          head_norms=None, rope_tabs=None, name="proj"):
    b, s, d = x.shape
    n_out = w.shape[1] if n_out is None else n_out
    nj = n_out // tn
    in_specs = [
        pl.BlockSpec((None, tm, d), lambda bi, i, j: (bi, i, 0)),
        pl.BlockSpec((1, d), lambda bi, i, j: (0, 0)),
        pl.BlockSpec((None, 1, d), lambda bi, i, j: (bi, 0, 0)),
        pl.BlockSpec((None, 1, d), lambda bi, i, j: (bi, 0, 0)),
        pl.BlockSpec((d, tn), lambda bi, i, j: (0, j + n_off)),
    ]
    args = [x, nw.reshape(1, d), shift, scale, w]
    if sections is not None:
        assert nj == 1
        in_specs += [pl.BlockSpec((1, HEAD_DIM), lambda bi, i, j: (0, 0))] * 2
        args += [head_norms[0].reshape(1, HEAD_DIM), head_norms[1].reshape(1, HEAD_DIM)]
    if rope_tabs is not None:
        in_specs += [pl.BlockSpec((tm, HEAD_DIM), lambda bi, i, j: (i, 0))] * 3
        args += list(rope_tabs)
    return pl.pallas_call(
        functools.partial(_proj_kernel, sections=sections, rope=rope_tabs is not None),
        out_shape=jax.ShapeDtypeStruct((b, s, n_out), BF16),
        grid=(b, s // tm, nj),
        in_specs=in_specs,
        out_specs=pl.BlockSpec((None, tm, tn), lambda bi, i, j: (bi, i, j)),
        scratch_shapes=[pltpu.VMEM((tm, d), BF16)] if sections is None else [],
        compiler_params=_params("arbitrary", "arbitrary", "arbitrary"),
        name=name,
    )(*args)


def _qk(q, k):
    return lax.dot_general(q, k, (((1,), (1,)), ((), ())), preferred_element_type=F32)


EXP2_SCALE = ATTN_SCALE * float(np.log2(np.e))
INV_ATTN_SCALE = 1.0 / ATTN_SCALE


def _softmax_pv(parts, sink=None):
    m = functools.reduce(jnp.maximum, [jnp.max(s, axis=-1, keepdims=True) for s, _ in parts])
    if sink is not None:
        m = jnp.maximum(m, sink)
    l = None
    o = None
    for s, v in parts:
        p = jnp.exp2((s - m) * EXP2_SCALE)
        ls = jnp.sum(p, axis=-1, keepdims=True)
        os_ = jnp.dot(p.astype(BF16), v, preferred_element_type=F32)
        l = ls if l is None else l + ls
        o = os_ if o is None else o + os_
    if sink is not None:
        l = l + jnp.exp2((sink - m) * EXP2_SCALE)
    return o / l


def _run_chains(n_blk, n_inner, scores, probs, weighted_values, unroll=1):
    per_body = unroll * n_inner
    assert per_body % 2 == 0 and n_blk % unroll == 0

    def chain(it, n):
        return it * unroll + n // n_inner, n % n_inner

    scores(0, 0, 0)

    def body(it, carry):
        for n in range(per_body):
            blk, g = chain(it, n + 1)
            scores(jnp.minimum(blk, n_blk - 1) if n + 1 == per_body else blk, g, (n + 1) % 2)
            probs(n % n_inner, n % 2)
            blk, g = chain(it, n - 1)
            weighted_values(jnp.maximum(blk, 0) if n == 0 else blk, g, (n - 1) % 2)
        return carry

    lax.fori_loop(0, n_blk // unroll, body, 0)
    weighted_values(n_blk - 1, n_inner - 1, 1)


GATTN_ROWS = 512


def _gattn_kernel(q_ref, k_ref, v_ref, kc_ref, vc_ref, o_ref, kall_ref, vall_ref, s_ref, p_ref):
    c = kc_ref.shape[0]

    n_blk = q_ref.shape[0] // GATTN_ROWS

    kall_ref[:c, :] = kc_ref[...]
    kall_ref[c:, :] = k_ref[...]
    vall_ref[:c, :HEAD_DIM] = vc_ref[...]
    vall_ref[c:, :HEAD_DIM] = v_ref[...]
    lane = lax.broadcasted_iota(jnp.int32, (vall_ref.shape[0], HEAD_DIM), 1)
    vall_ref[:, HEAD_DIM:] = jnp.where(lane == 0, 1.0, 0.0).astype(BF16)

    def rows_of(blk):
        return pl.ds(pl.multiple_of(blk * GATTN_ROWS, GATTN_ROWS), GATTN_ROWS)

    def cols_of(g):
        return slice(g * HEAD_DIM, (g + 1) * HEAD_DIM)

    def scores(blk, g, slot):
        s_ref[slot] = _qk(q_ref[rows_of(blk), cols_of(g)], kall_ref[...])

    def probs(g, slot):
        m = jnp.max(s_ref[slot], axis=-1, keepdims=True)
        p_ref[slot] = jnp.exp2((s_ref[slot] - m) * EXP2_SCALE).astype(BF16)

    def weighted_values(blk, g, slot):
        ol = jnp.dot(p_ref[slot], vall_ref[...], preferred_element_type=F32)
        o = ol[:, :HEAD_DIM] / ol[:, HEAD_DIM:HEAD_DIM + 1]
        o_ref[rows_of(blk), cols_of(g)] = o.astype(o_ref.dtype)

    p_ref[1] = jnp.ones(p_ref.shape[1:], BF16)
    _run_chains(n_blk, GQA_GROUP, scores, probs, weighted_values)


def _gattn(qkv, qkv_c, *, q_blk, k_blk, v_blk):
    b, s, _ = qkv.shape
    c = qkv_c.shape[1]
    gw = GQA_GROUP * HEAD_DIM
    assert GQA_GROUP % 2 == 0 and s % GATTN_ROWS == 0
    single = pl.Buffered(1)
    return pl.pallas_call(
        _gattn_kernel,
        out_shape=jax.ShapeDtypeStruct((b, s, A_Q_HEADS * HEAD_DIM), BF16),
        grid=(b, A_KV_HEADS),
        in_specs=[
            pl.BlockSpec((None, s, gw), lambda bi, kv: (bi, 0, q_blk + kv), pipeline_mode=single),
            pl.BlockSpec((None, s, HEAD_DIM), lambda bi, kv: (bi, 0, k_blk + kv), pipeline_mode=single),
            pl.BlockSpec((None, s, HEAD_DIM), lambda bi, kv: (bi, 0, v_blk + kv), pipeline_mode=single),
            pl.BlockSpec((None, c, HEAD_DIM), lambda bi, kv: (bi, 0, k_blk + kv)),
            pl.BlockSpec((None, c, HEAD_DIM), lambda bi, kv: (bi, 0, v_blk + kv)),
        ],
        out_specs=pl.BlockSpec((None, s, gw), lambda bi, kv: (bi, 0, kv)),
        scratch_shapes=[
            pltpu.VMEM((c + s, HEAD_DIM), BF16),
            pltpu.VMEM((c + s, 2 * HEAD_DIM), BF16),
            pltpu.VMEM((2, GATTN_ROWS, c + s), F32),
            pltpu.VMEM((2, GATTN_ROWS, c + s), BF16),
        ],
        compiler_params=_params("arbitrary", "arbitrary"),
        name="gattn",
    )(qkv, qkv, qkv, qkv_c, qkv_c)


WATTN_TQ = 4 * WINDOW
WATTN_NK = WATTN_TQ + 2 * WINDOW


def _wattn_kernel(sink_ref, q_ref, k_ref, v_ref, kc_ref, vc_ref, o_ref,
                  mask_ref, s_ref, sc_ref, p_ref, pc_ref, l_ref):
    kv = pl.program_id(1)
    seq = q_ref.shape[0]
    n_tile = seq // WATTN_TQ

    def rows_of(tile):
        return pl.ds(pl.multiple_of(tile * WATTN_TQ, WATTN_TQ), WATTN_TQ)

    def key0_of(tile):
        lo, hi = 0, seq - WATTN_NK
        k0 = tile * WATTN_TQ - WINDOW
        return min(max(k0, lo), hi) if isinstance(tile, int) else jnp.clip(k0, lo, hi)

    def key_rows_of(tile):
        return pl.ds(pl.multiple_of(key0_of(tile), WINDOW), WATTN_NK)

    def cols_of(g):
        return slice(g * HEAD_DIM, (g + 1) * HEAD_DIM)

    for var, tile in enumerate((0, 1, n_tile - 1)):
        shift = key0_of(tile) - tile * WATTN_TQ
        dist = (shift + lax.broadcasted_iota(jnp.int32, (WATTN_TQ, WATTN_NK), 1)
                - lax.broadcasted_iota(jnp.int32, (WATTN_TQ, WATTN_NK), 0))
        mask_ref[var] = jnp.where(jnp.abs(dist) <= WINDOW, 0.0, NEG)

    def scores(tile, g, slot):
        q = q_ref[rows_of(tile), cols_of(g)]
        var = jnp.where(tile == 0, 0, jnp.where(tile == n_tile - 1, 2, 1))
        s_ref[slot] = _qk(q, k_ref[key_rows_of(tile), :]) + mask_ref[var]
        sc_ref[slot] = _qk(q, kc_ref[...])

    def probs(g, slot):
        s, sc = s_ref[slot], sc_ref[slot]
        sink = sink_ref[kv, g] * INV_ATTN_SCALE
        m = jnp.maximum(jnp.max(s, axis=-1, keepdims=True), jnp.max(sc, axis=-1, keepdims=True))
        m = jnp.maximum(m, sink)
        p = jnp.exp2((s - m) * EXP2_SCALE)
        pc = jnp.exp2((sc - m) * EXP2_SCALE)
        l_ref[slot] = (jnp.sum(p, axis=-1, keepdims=True) + jnp.sum(pc, axis=-1, keepdims=True)
                       + jnp.exp2((sink - m) * EXP2_SCALE))
        p_ref[slot] = p.astype(BF16)
        pc_ref[slot] = pc.astype(BF16)

    def weighted_values(tile, g, slot):
        o = (jnp.dot(p_ref[slot], v_ref[key_rows_of(tile), :], preferred_element_type=F32)
             + jnp.dot(pc_ref[slot], vc_ref[...], preferred_element_type=F32))
        o_ref[rows_of(tile), cols_of(g)] = (o / l_ref[slot]).astype(o_ref.dtype)

    p_ref[1] = jnp.ones(p_ref.shape[1:], BF16)
    pc_ref[1] = jnp.ones(pc_ref.shape[1:], BF16)
    l_ref[1] = jnp.ones(l_ref.shape[1:], F32)
    _run_chains(seq // WATTN_TQ, GQA_GROUP, scores, probs, weighted_values)


def _wattn(sink_kg, qkv, qkv_c, *, q_blk, k_blk, v_blk):
    b, s, _ = qkv.shape
    c = qkv_c.shape[1]
    gw = GQA_GROUP * HEAD_DIM
    assert s % WATTN_TQ == 0 and s >= 3 * WATTN_TQ

    def col_map(col):
        return lambda bi, kv: (bi, 0, col + kv)

    return pl.pallas_call(
        _wattn_kernel,
        out_shape=jax.ShapeDtypeStruct((b, s, B_Q_HEADS * HEAD_DIM), BF16),
        grid=(b, B_KV_HEADS),
        in_specs=[
            pl.BlockSpec(memory_space=pltpu.SMEM),
            pl.BlockSpec((None, s, gw), col_map(q_blk)),
            pl.BlockSpec((None, s, HEAD_DIM), col_map(k_blk)),
            pl.BlockSpec((None, s, HEAD_DIM), col_map(v_blk)),
            pl.BlockSpec((None, c, HEAD_DIM), col_map(k_blk)),
            pl.BlockSpec((None, c, HEAD_DIM), col_map(v_blk)),
        ],
        out_specs=pl.BlockSpec((None, s, gw), col_map(0)),
        scratch_shapes=[
            pltpu.VMEM((3, WATTN_TQ, WATTN_NK), F32),
            pltpu.VMEM((2, WATTN_TQ, WATTN_NK), F32),
            pltpu.VMEM((2, WATTN_TQ, c), F32),
            pltpu.VMEM((2, WATTN_TQ, WATTN_NK), BF16),
            pltpu.VMEM((2, WATTN_TQ, c), BF16),
            pltpu.VMEM((2, WATTN_TQ, 1), F32),
        ],
        compiler_params=_params("arbitrary", "arbitrary"),
        name="wattn",
    )(sink_kg, qkv, qkv, qkv, qkv_c, qkv_c)


def _cattn_kernel(sink_ref, q_ref, k_ref, v_ref, o_ref):
    grp = pl.program_id(1)
    k, v = k_ref[...], v_ref[...]
    for g in range(GQA_GROUP):
        q = q_ref[:, g * HEAD_DIM:(g + 1) * HEAD_DIM]
        o = _softmax_pv([(_qk(q, k), v)], sink=sink_ref[grp, g] * INV_ATTN_SCALE)
        o_ref[:, g * HEAD_DIM:(g + 1) * HEAD_DIM] = o.astype(o_ref.dtype)


def _cattn(sink_all, qkv_c):
    b, c, _ = qkv_c.shape
    gw = GQA_GROUP * HEAD_DIM
    n_grp = A_KV_HEADS + B_KV_HEADS
    q_map = lambda bi, g: (bi, 0, g + g // A_KV_HEADS)
    k_map = lambda bi, g: (bi, 0, 8 + g + 10 * (g // A_KV_HEADS))
    v_map = lambda bi, g: (bi, 0, 10 + g + 10 * (g // A_KV_HEADS))
    return pl.pallas_call(
        _cattn_kernel,
        out_shape=jax.ShapeDtypeStruct((b, c, N_HEADS * HEAD_DIM), BF16),
        grid=(b, n_grp),
        in_specs=[
            pl.BlockSpec(memory_space=pltpu.SMEM),
            pl.BlockSpec((None, c, gw), q_map),
            pl.BlockSpec((None, c, HEAD_DIM), k_map),
            pl.BlockSpec((None, c, HEAD_DIM), v_map),
        ],
        out_specs=pl.BlockSpec((None, c, gw), lambda bi, g: (bi, 0, g)),
        compiler_params=_params("arbitrary", "arbitrary"),
        name="cattn",
    )(sink_all, qkv_c, qkv_c, qkv_c)


NATTN_ROWS = 4
NATTN_TQ = NATTN_ROWS * GRID_W
NATTN_KROWS = 3 * NATTN_ROWS


NATTN_HG = 2
NATTN_UNROLL = 2
N_RI = 2 * NA_KH - 1
N_CI = 2 * NA_KW - 1


def _na_key_row0(tile, rows):
    lo, hi = 0, rows - NATTN_KROWS
    r0 = (tile - 1) * NATTN_ROWS
    if isinstance(tile, int):
        return min(max(r0, lo), hi)
    return jnp.clip(r0, lo, hi)


def _nattn_kernel(q_ref, k_ref, v_ref, kc_ref, vc_ref, bias_ref, o_ref,
                  vs_ref, vcs_ref, s_ref, sc_ref, p_ref, pc_ref):
    seq = q_ref.shape[0]
    c = kc_ref.shape[0]
    n_tile = seq // NATTN_TQ
    nk = NATTN_KROWS * GRID_W

    def rows_of(tile):
        return pl.ds(pl.multiple_of(tile * NATTN_TQ, NATTN_TQ), NATTN_TQ)

    def key_rows_of(tile):
        return pl.ds(pl.multiple_of(_na_key_row0(tile, seq // GRID_W) * GRID_W, NATTN_TQ), nk)

    def cols_of(g):
        return slice(g * HEAD_DIM, (g + 1) * HEAD_DIM)

    def ones_col(n):
        return jnp.where(lax.broadcasted_iota(jnp.int32, (n, HEAD_DIM), 1) == 0, 1.0, 0.0).astype(BF16)

    for g in range(NATTN_HG):
        vs_ref[g] = jnp.concatenate([v_ref[:, cols_of(g)], ones_col(seq)], axis=1)
        vcs_ref[g] = jnp.concatenate([vc_ref[:, cols_of(g)], ones_col(c)], axis=1)

    def scores(tile, _, slot):
        var = jnp.where(tile == 0, 0, jnp.where(tile == n_tile - 1, 2, 1))
        for g in range(NATTN_HG):
            q = q_ref[rows_of(tile), cols_of(g)]
            s_ref[slot, g] = _qk(q, k_ref[key_rows_of(tile), cols_of(g)]) + bias_ref[g, var]
            sc_ref[slot, g] = _qk(q, kc_ref[:, cols_of(g)])

    def probs(_, slot):
        for g in range(NATTN_HG):
            s, sc = s_ref[slot, g], sc_ref[slot, g]
            folded = jnp.concatenate([jnp.maximum(s[:, :c], sc), s[:, c:]], axis=1)
            m = jnp.max(folded, axis=-1, keepdims=True)
            p_ref[slot, g] = jnp.exp2((s - m) * EXP2_SCALE).astype(BF16)
            pc_ref[slot, g] = jnp.exp2((sc - m) * EXP2_SCALE).astype(BF16)

    def weighted_values(tile, _, slot):
        for g in range(NATTN_HG):
            ol = (jnp.dot(p_ref[slot, g], vs_ref[g, key_rows_of(tile), :], preferred_element_type=F32)
                  + jnp.dot(pc_ref[slot, g], vcs_ref[g], preferred_element_type=F32))
            o = ol[:, :HEAD_DIM] / ol[:, HEAD_DIM:HEAD_DIM + 1]
            o_ref[rows_of(tile), cols_of(g)] = o.astype(o_ref.dtype)

    p_ref[1] = jnp.ones(p_ref.shape[1:], BF16)
    pc_ref[1] = jnp.ones(pc_ref.shape[1:], BF16)
    _run_chains(n_tile, 1, scores, probs, weighted_values, unroll=NATTN_UNROLL)


def _na_bias_kernel(rpb_ref, o_ref, u_ref, *, rows):
    base = pl.program_id(0) * (N_RI * N_CI)
    kh = min(NA_KH, rows)

    def fill(sg, carry):
        q0 = pl.multiple_of(sg * SUBLANES, SUBLANES)
        qc = q0 + lax.broadcasted_iota(jnp.int32, (SUBLANES, LANES), 0)
        kc = lax.broadcasted_iota(jnp.int32, (SUBLANES, LANES), 1) & (GRID_W - 1)
        diag = kc - qc + (NA_KW - 1)
        us = [jnp.zeros((SUBLANES, LANES), F32) for _ in range(N_RI)]
        for ci in range(N_CI):
            hit = diag == ci
            for a in range(N_RI):
                us[a] = jnp.where(hit, rpb_ref[base + a * N_CI + ci], us[a])
        for a in range(N_RI):
            u_ref[a, pl.ds(q0, SUBLANES), :] = us[a] * INV_ATTN_SCALE
        return carry

    lax.fori_loop(0, GRID_W // SUBLANES, fill, 0)

    qc = lax.broadcasted_iota(jnp.int32, (GRID_W, LANES), 0)
    lane = lax.broadcasted_iota(jnp.int32, (GRID_W, LANES), 1)
    kc = lane & (GRID_W - 1)
    cs = jnp.clip(qc - NA_KW // 2, 0, GRID_W - NA_KW)
    col_valid = (kc >= cs) & (kc < cs + NA_KW)
    low = lane < GRID_W
    masks = {(True, True): col_valid, (True, False): col_valid & low,
             (False, True): col_valid & jnp.logical_not(low)}
    neg_blk = jnp.full((GRID_W, LANES), NEG, F32)
    n_tile = rows // NATTN_ROWS
    for var, tile in enumerate((0, 1, n_tile - 1)):
        r0 = tile * NATTN_ROWS
        k0 = _na_key_row0(tile, rows)
        for qr in range(NATTN_ROWS):
            rs = min(max(r0 + qr - kh // 2, 0), rows - kh)
            for jj in range(NATTN_KROWS // 2):
                kr = k0 + 2 * jj
                ok = (rs <= kr < rs + kh, rs <= kr + 1 < rs + kh)
                if ok == (False, False):
                    blk = neg_blk
                else:
                    a = kr - (r0 + qr) + NA_KH - 1
                    a_lo = min(max(a, 0), N_RI - 1)
                    a_hi = min(max(a + 1, 0), N_RI - 1)
                    pair = jnp.where(low, u_ref[a_lo], u_ref[a_hi])
                    blk = jnp.where(masks[ok], pair, NEG)
                o_ref[var, qr * GRID_W:(qr + 1) * GRID_W, jj * LANES:(jj + 1) * LANES] = blk


def _na_bias_tables(rpb, rows):
    h = rpb.shape[0]
    assert 2 * GRID_W == LANES and rows >= 3 * NATTN_ROWS and NATTN_ROWS == NA_KH // 2
    return pl.pallas_call(
        functools.partial(_na_bias_kernel, rows=rows),
        out_shape=jax.ShapeDtypeStruct((h, 3, NATTN_TQ, NATTN_KROWS * GRID_W), F32),
        grid=(h,),
        in_specs=[pl.BlockSpec(memory_space=pltpu.SMEM)],
        out_specs=pl.BlockSpec((None, 3, NATTN_TQ, NATTN_KROWS * GRID_W), lambda hh: (hh, 0, 0, 0)),
        scratch_shapes=[pltpu.VMEM((N_RI, GRID_W, LANES), F32)],
        compiler_params=_params("arbitrary"),
        name="na_bias",
    )(rpb.reshape(-1))


def _nattn(qkv, kv_c, bias):
    b, s, _ = qkv.shape
    c = kv_c.shape[1]
    h = C_HEADS
    ng = h // NATTN_HG
    gw = NATTN_HG * HEAD_DIM
    nk = NATTN_KROWS * GRID_W
    assert NATTN_HG % 2 == 0 and s % NATTN_TQ == 0

    def col_map(col):
        return lambda hg, bi: (bi, 0, col + hg)

    return pl.pallas_call(
        _nattn_kernel,
        out_shape=jax.ShapeDtypeStruct((b, s, h * HEAD_DIM), BF16),
        grid=(ng, b),
        in_specs=[
            pl.BlockSpec((None, s, gw), col_map(0)),
            pl.BlockSpec((None, s, gw), col_map(ng)),
            pl.BlockSpec((None, s, gw), col_map(2 * ng)),
            pl.BlockSpec((None, c, gw), col_map(0)),
            pl.BlockSpec((None, c, gw), col_map(ng)),
            pl.BlockSpec((NATTN_HG, 3, NATTN_TQ, nk), lambda hg, bi: (hg, 0, 0, 0),
                         pipeline_mode=pl.Buffered(1)),
        ],
        out_specs=pl.BlockSpec((None, s, gw), col_map(0)),
        scratch_shapes=[
            pltpu.VMEM((NATTN_HG, s, 2 * HEAD_DIM), BF16),
            pltpu.VMEM((NATTN_HG, c, 2 * HEAD_DIM), BF16),
            pltpu.VMEM((2, NATTN_HG, NATTN_TQ, nk), F32),
            pltpu.VMEM((2, NATTN_HG, NATTN_TQ, c), F32),
            pltpu.VMEM((2, NATTN_HG, NATTN_TQ, nk), BF16),
            pltpu.VMEM((2, NATTN_HG, NATTN_TQ, c), BF16),
        ],
        compiler_params=_params("arbitrary", "arbitrary"),
        name="nattn",
    )(qkv, qkv, qkv, kv_c, kv_c, bias)


OUTPROJ_ROWS = 256


def _resid_mlp_kernel(*refs, widths, final_norm):
    x_ref, wo_ref, g1_ref, nw_ref, sh_ref, sc_ref, g2_ref, w1_ref, w2_ref = refs[:9]
    rest = refs[9:]
    o_refs = rest[:len(widths)]
    rest = rest[len(widths):]
    if final_norm:
        fw_ref = rest[0]
        rest = rest[1:]
    y_ref, x1_ref, h_ref = rest
    f = pl.program_id(2)
    tm = x_ref.shape[0]

    last = pl.num_programs(2) - 1

    def mlp_partial(rows=slice(None)):
        a = jnp.dot(h_ref[rows, :], w1_ref[...], preferred_element_type=F32)
        a = jnp.square(jnp.maximum(a, 0.0)).astype(BF16)
        return jnp.dot(a, w2_ref[...], preferred_element_type=F32)

    @pl.when(f == 0)
    def _():
        for r0 in range(0, tm, OUTPROJ_ROWS):
            rows = slice(r0, min(r0 + OUTPROJ_ROWS, tm))
            y = None
            k0 = 0
            for o_ref, kw in zip(o_refs, widths):
                part = jnp.dot(o_ref[rows, :], wo_ref[k0:k0 + kw, :], preferred_element_type=F32)
                y = part if y is None else y + part
                k0 += kw
            x1 = x_ref[rows, :] + g1_ref[...] * y
            x1_ref[rows, :] = x1
            h_ref[rows, :] = _norm_modulate(x1, nw_ref[...], sh_ref[...], sc_ref[...]).astype(BF16)
        y_ref[...] = mlp_partial()

    @pl.when((f > 0) & (f < last))
    def _():
        y_ref[...] += mlp_partial()

    @pl.when(f == last)
    def _():
        for r0 in range(0, tm, OUTPROJ_ROWS):
            rows = slice(r0, min(r0 + OUTPROJ_ROWS, tm))
            y = x1_ref[rows, :] + g2_ref[...] * (y_ref[rows, :] + mlp_partial(rows))
            if final_norm:
                y = y * lax.rsqrt(jnp.mean(y * y, axis=-1, keepdims=True) + NORM_EPS) * fw_ref[...]
            y_ref[rows, :] = y


def _resid_mlp(x, o_parts, w_out, g1, nw, shift, scale, g2, w1, w2, *, tm, tf, final_w=None):
    b, s, d = x.shape
    dff = w1.shape[1]
    widths = tuple(o.shape[2] for o in o_parts)
    assert sum(widths) == w_out.shape[0] and dff // tf >= 2
    vec = pl.BlockSpec((None, 1, d), lambda bi, i, f: (bi, 0, 0))
    in_specs = [
        pl.BlockSpec((None, tm, d), lambda bi, i, f: (bi, i, 0)),
        pl.BlockSpec(w_out.shape, lambda bi, i, f: (0, 0), pipeline_mode=pl.Buffered(1)),
        vec,
        pl.BlockSpec((1, d), lambda bi, i, f: (0, 0)),
        vec, vec, vec,
        pl.BlockSpec((d, tf), lambda bi, i, f: (0, f)),
        pl.BlockSpec((tf, d), lambda bi, i, f: (f, 0)),
    ] + [pl.BlockSpec((None, tm, kw), lambda bi, i, f: (bi, i, 0)) for kw in widths]
    args = [x, w_out, g1, nw.reshape(1, d), shift, scale, g2, w1, w2, *o_parts]
    if final_w is not None:
        in_specs.append(pl.BlockSpec((1, d), lambda bi, i, f: (0, 0)))
        args.append(final_w.reshape(1, d))
    return pl.pallas_call(
        functools.partial(_resid_mlp_kernel, widths=widths, final_norm=final_w is not None),
        out_shape=jax.ShapeDtypeStruct(x.shape, x.dtype),
        grid=(b, s // tm, dff // tf),
        in_specs=in_specs,
        out_specs=pl.BlockSpec((None, tm, d), lambda bi, i, f: (bi, i, 0)),
        scratch_shapes=[pltpu.VMEM((tm, d), F32), pltpu.VMEM((tm, d), BF16)],
        compiler_params=_params("arbitrary", "arbitrary", "arbitrary"),
        name="resid_mlp",
    )(*args)


EVEN_SECTIONS = (
    (0, A_Q_HEADS, "q_norm"),
    (A_Q_HEADS * HEAD_DIM, A_KV_HEADS, "k_norm"),
    ((A_Q_HEADS + A_KV_HEADS) * HEAD_DIM, A_KV_HEADS, "v"),
    ((A_Q_HEADS + 2 * A_KV_HEADS) * HEAD_DIM, B_Q_HEADS, "q"),
    ((A_Q_HEADS + 2 * A_KV_HEADS + B_Q_HEADS) * HEAD_DIM, B_KV_HEADS, "k"),
    ((A_Q_HEADS + 2 * A_KV_HEADS + B_Q_HEADS + B_KV_HEADS) * HEAD_DIM, B_KV_HEADS, "v"),
)
LAT_TM = 512
MLP_TF = 1024


def kernel(x, c, ctx, c_ctx, ada_w, ada_b, norm_w, mlp_w1, mlp_w2, ev_w_in, ev_w_out, ev_q_norm,
           ev_k_norm, ev_sink, od_w_in, od_w_out, od_rpb, final_norm_w):
    b, s, d = x.shape
    n_ctx = ctx.shape[1]
    depth = ada_w.shape[0]
    rows = s // GRID_W

    mod = _ada_mod(jnp.concatenate([c, c_ctx[None, :]], axis=0), ada_w, ada_b)
    rope_tabs = _rope_tables(s)

    def mod_vecs(layer):
        lat = [mod[layer, :b, k * d:(k + 1) * d][:, None, :] for k in range(6)]
        cx = [mod[layer, b:b + 1, k * d:(k + 1) * d][:, None, :] for k in range(6)]
        return lat, cx

    ctx = ctx.reshape(1, b * n_ctx, d)

    for layer in range(depth):
        need_ctx = layer < depth - 1
        (sh1, sc1, g1, sh2, sc2, g2), (sh1c, sc1c, g1c, sh2c, sc2c, g2c) = mod_vecs(layer)
        j = layer // 2
        w1 = _to_bf16(mlp_w1, layer)
        w2 = _to_bf16(mlp_w2, layer)
        if layer % 2 == 0:
            w_in = _to_bf16(ev_w_in, j)
            w_out = _to_bf16(ev_w_out, j)
            n_in = w_in.shape[1]
            norms = (ev_q_norm[j], ev_k_norm[j])
            qkv = _proj(x, norm_w[layer, 0], sh1, sc1, w_in, tm=LAT_TM, tn=n_in,
                        sections=EVEN_SECTIONS, head_norms=norms, rope_tabs=rope_tabs, name="proj_even")
            qkv_c = _proj(ctx, norm_w[layer, 0], sh1c, sc1c, w_in, tm=b * n_ctx, tn=n_in,
                          sections=EVEN_SECTIONS, head_norms=norms, name="proj_even_ctx")
            qkv_c = qkv_c.reshape(b, n_ctx, n_in)
            oa = _gattn(qkv, qkv_c, q_blk=0, k_blk=A_Q_HEADS, v_blk=A_Q_HEADS + A_KV_HEADS)
            sink_kg = ev_sink[j].reshape(B_KV_HEADS, GQA_GROUP).astype(F32)
            qb_blk = (A_Q_HEADS + 2 * A_KV_HEADS) // GQA_GROUP
            kb_blk = A_Q_HEADS + 2 * A_KV_HEADS + B_Q_HEADS
            ob = _wattn(sink_kg, qkv, qkv_c, q_blk=qb_blk, k_blk=kb_blk, v_blk=kb_blk + B_KV_HEADS)
            o_parts = [oa, ob]
            if need_ctx:
                sink_all = jnp.concatenate([jnp.full((A_KV_HEADS, GQA_GROUP), NEG, F32), sink_kg], axis=0)
                oc_parts = [_cattn(sink_all, qkv_c).reshape(1, b * n_ctx, N_HEADS * HEAD_DIM)]
        else:
            w_in = _to_bf16(od_w_in, j)
            w_out = _to_bf16(od_w_out, j)
            hd = C_HEADS * HEAD_DIM
            qkv = _proj(x, norm_w[layer, 0], sh1, sc1, w_in, tm=LAT_TM, tn=hd, name="proj_odd")
            kv_c = _proj(ctx, norm_w[layer, 0], sh1c, sc1c, w_in, tm=b * n_ctx, tn=hd, n_off=1,
                         n_out=2 * hd, name="proj_odd_ctx").reshape(b, n_ctx, 2 * hd)
            o_parts = [_nattn(qkv, kv_c, _na_bias_tables(od_rpb[j], rows))]
            assert not need_ctx, "the odd layer is the last one at this depth"
        x = _resid_mlp(x, o_parts, w_out, g1, norm_w[layer, 1], sh2, sc2, g2, w1, w2, tm=LAT_TM,
                       tf=MLP_TF, final_w=None if need_ctx else final_norm_w)
        if need_ctx:
            ctx = _resid_mlp(ctx, oc_parts, w_out, g1c, norm_w[layer, 1], sh2c, sc2c, g2c, w1, w2,
                             tm=b * n_ctx, tf=MLP_TF)
    return x
```

```python
import functools

import jax
import jax.numpy as jnp
import numpy as np
from jax import lax
from jax.experimental import pallas as pl
from jax.experimental.pallas import tpu as pltpu

D_MODEL = 2048
GRID_W = 64
HEAD_DIM = 128
N_HEADS = D_MODEL // HEAD_DIM
A_Q_HEADS = N_HEADS // 2
A_KV_HEADS = max(1, A_Q_HEADS // 4)
B_Q_HEADS = N_HEADS - A_Q_HEADS
B_KV_HEADS = max(1, B_Q_HEADS // 4)
GQA_GROUP = A_Q_HEADS // A_KV_HEADS
C_HEADS = N_HEADS
WINDOW = 128
NA_KH = 8
NA_KW = 16
D_FF = 4 * D_MODEL
ROPE_THETA = 10000.0
ROPE_PAIRS = HEAD_DIM // 4
NORM_EPS = 1e-6
NEG = -1e30
ATTN_SCALE = HEAD_DIM ** -0.5

V7X_VMEM_BYTES = 64 * 1024 * 1024
VMEM_LIMIT = V7X_VMEM_BYTES - 6 * 1024 * 1024
LANES = 128
SUBLANES = 8

F32 = jnp.float32
BF16 = jnp.bfloat16


def _params(*sem):
    return pltpu.CompilerParams(dimension_semantics=sem, vmem_limit_bytes=VMEM_LIMIT)


ADA_ROWS = 3
ADA_TN = 1024


def _ada_kernel(cb_ref, w_ref, b_ref, o_ref, s_ref):
    @pl.when((pl.program_id(0) == 0) & (pl.program_id(1) == 0))
    def _():
        cvals = cb_ref[...]
        s_ref[...] = cvals * (1.0 / (1.0 + jnp.exp(-cvals)))

    nchunk = ADA_TN // LANES

    def body(kk, accs):
        k0 = pl.multiple_of(kk * SUBLANES, SUBLANES)
        w = w_ref[pl.ds(k0, SUBLANES), :]
        out = []
        for r in range(ADA_ROWS):
            s = s_ref[r, pl.ds(k0, SUBLANES), :]
            for cch in range(nchunk):
                out.append(accs[r * nchunk + cch] + w[:, cch * LANES:(cch + 1) * LANES] * s)
        return tuple(out)

    zeros = tuple(jnp.zeros((SUBLANES, LANES), F32) for _ in range(ADA_ROWS * nchunk))
    accs = lax.fori_loop(0, D_MODEL // SUBLANES, body, zeros, unroll=4)
    o_ref[...] = jnp.broadcast_to(b_ref[...], o_ref.shape)
    for r in range(ADA_ROWS):
        row = jnp.concatenate(
            [jnp.sum(accs[r * nchunk + cch], axis=0, keepdims=True) for cch in range(nchunk)], axis=1)
        o_ref[r:r + 1, :] = row + b_ref[...]


def _ada_mod(cc, ada_w, ada_b):
    depth, d, n = ada_w.shape
    cb = jnp.broadcast_to(cc[:, :, None], (ADA_ROWS, d, LANES))
    return pl.pallas_call(
        _ada_kernel,
        out_shape=jax.ShapeDtypeStruct((depth, SUBLANES, n), F32),
        grid=(depth, n // ADA_TN),
        in_specs=[
            pl.BlockSpec((ADA_ROWS, d, LANES), lambda l, j: (0, 0, 0)),
            pl.BlockSpec((None, d, ADA_TN), lambda l, j: (l, 0, j)),
            pl.BlockSpec((None, 1, ADA_TN), lambda l, j: (l, 0, j)),
        ],
        out_specs=pl.BlockSpec((None, SUBLANES, ADA_TN), lambda l, j: (l, 0, j)),
        scratch_shapes=[pltpu.VMEM((ADA_ROWS, d, LANES), F32)],
        compiler_params=_params("arbitrary", "arbitrary"),
        name="ada_mod",
    )(cb, ada_w, ada_b.reshape(depth, 1, n))


CAST_BLOCK_BYTES = 8 * 1024 * 1024


def _cast_kernel(w_ref, o_ref):
    o_ref[...] = w_ref[...].astype(o_ref.dtype)


def _to_bf16(w_stack, idx):
    _, r, c = w_stack.shape
    tr = r
    while tr * c * w_stack.dtype.itemsize > CAST_BLOCK_BYTES and tr % 32 == 0:
        tr //= 2
    return pl.pallas_call(
        _cast_kernel,
        out_shape=jax.ShapeDtypeStruct((r, c), BF16),
        grid=(r // tr,),
        in_specs=[pl.BlockSpec((None, tr, c), lambda i: (idx, i, 0))],
        out_specs=pl.BlockSpec((tr, c), lambda i: (i, 0)),
        compiler_params=_params("arbitrary"),
        name="to_bf16",
    )(w_stack)


def _norm_modulate(x, nw, shift, scale):
    ms = jnp.mean(x * x, axis=-1, keepdims=True)
    y = x * lax.rsqrt(ms + NORM_EPS) * nw
    return y * (1.0 + scale) + shift


def _rope(y, cos, sin_lo, sin_hi):
    return (y * cos + pltpu.roll(y, HEAD_DIM - ROPE_PAIRS, 1) * sin_lo
            + pltpu.roll(y, ROPE_PAIRS, 1) * sin_hi)


def _rope_tables(seq):
    t = jnp.arange(seq)
    row = (t // GRID_W).astype(F32)
    col = (t % GRID_W).astype(F32)
    inv = ROPE_THETA ** (-jnp.arange(ROPE_PAIRS, dtype=F32) / ROPE_PAIRS)
    ang_r = row[:, None] * inv
    ang_c = col[:, None] * inv
    ang = jnp.concatenate([ang_r, ang_r, ang_c, ang_c], axis=-1)
    cos, sin = jnp.cos(ang), jnp.sin(ang)
    first = (jnp.arange(HEAD_DIM) % (2 * ROPE_PAIRS)) < ROPE_PAIRS
    sin_lo = jnp.where(first, -sin, 0.0)
    sin_hi = jnp.where(first, 0.0, sin)
    return cos, sin_lo, sin_hi


PROJ_ROWS = 256


def _proj_kernel(*refs, sections, rope, head_major):
    x_ref, nw_ref, sh_ref, sc_ref, w_ref = refs[:5]
    rest = refs[5:]
    if sections is not None:
        qn_ref, kn_ref = rest[:2]
        rest = rest[2:]
    if rope:
        cos_ref, slo_ref, shi_ref = rest[:3]
        rest = rest[3:]
    o_ref = rest[0]
    h_ref = rest[1] if sections is None else None
    tm = x_ref.shape[0]

    def project(h, rows):
        if sections is None:
            y = jnp.dot(h, w_ref[...], preferred_element_type=F32).astype(o_ref.dtype)
            if head_major:
                for hh in range(y.shape[1] // HEAD_DIM):
                    o_ref[hh, rows, :] = y[:, hh * HEAD_DIM:(hh + 1) * HEAD_DIM]
            else:
                o_ref[rows, :] = y
            return
        for c0, nh, kind in sections:
            y = jnp.dot(h, w_ref[:, c0:c0 + nh * HEAD_DIM], preferred_element_type=F32)
            for hh in range(nh):
                yh = y[:, hh * HEAD_DIM:(hh + 1) * HEAD_DIM]
                if kind in ("q_norm", "k_norm"):
                    hw = (qn_ref if kind == "q_norm" else kn_ref)[...]
                    yh = yh * lax.rsqrt(jnp.mean(yh * yh, axis=-1, keepdims=True) + NORM_EPS) * hw
                if rope and kind != "v":
                    yh = _rope(yh, cos_ref[rows, :], slo_ref[rows, :], shi_ref[rows, :])
                lo = c0 + hh * HEAD_DIM
                o_ref[rows, lo:lo + HEAD_DIM] = yh.astype(o_ref.dtype)

    @pl.when(pl.program_id(2) == 0)
    def _():
        for r0 in range(0, tm, PROJ_ROWS):
            rows = slice(r0, min(r0 + PROJ_ROWS, tm))
            h = _norm_modulate(x_ref[rows, :], nw_ref[...], sh_ref[...], sc_ref[...]).astype(BF16)
            if h_ref is not None:
                h_ref[rows, :] = h
            project(h, rows)

    if h_ref is not None:
        @pl.when(pl.program_id(2) > 0)
        def _():
            project(h_ref[...], slice(0, tm))


def _proj(x, nw, shift, scale, w, *, tm, tn, n_off=0, n_out=None, sections=None,
          head_norms=None, rope_tabs=None, head_major=False, name="proj"):
    b, s, d = x.shape
    n_out = w.shape[1] if n_out is None else n_out
    nj = n_out // tn
    assert not (head_major and sections is not None)
    if head_major:
        hpt = tn // HEAD_DIM
        out_shape = jax.ShapeDtypeStruct((b, n_out // HEAD_DIM, s, HEAD_DIM), BF16)
        out_spec = pl.BlockSpec((None, hpt, tm, HEAD_DIM), lambda bi, i, j: (bi, j, i, 0))
    else:
        out_shape = jax.ShapeDtypeStruct((b, s, n_out), BF16)
        out_spec = pl.BlockSpec((None, tm, tn), lambda bi, i, j: (bi, i, j))
    in_specs = [
        pl.BlockSpec((None, tm, d), lambda bi, i, j: (bi, i, 0)),
        pl.BlockSpec((1, d), lambda bi, i, j: (0, 0)),
        pl.BlockSpec((None, 1, d), lambda bi, i, j: (bi, 0, 0)),
        pl.BlockSpec((None, 1, d), lambda bi, i, j: (bi, 0, 0)),
        pl.BlockSpec((d, tn), lambda bi, i, j: (0, j + n_off)),
    ]
    args = [x, nw.reshape(1, d), shift, scale, w]
    if sections is not None:
        assert nj == 1
        in_specs += [pl.BlockSpec((1, HEAD_DIM), lambda bi, i, j: (0, 0))] * 2
        args += [head_norms[0].reshape(1, HEAD_DIM), head_norms[1].reshape(1, HEAD_DIM)]
    if rope_tabs is not None:
        in_specs += [pl.BlockSpec((tm, HEAD_DIM), lambda bi, i, j: (i, 0))] * 3
        args += list(rope_tabs)
    return pl.pallas_call(
        functools.partial(_proj_kernel, sections=sections, rope=rope_tabs is not None,
                          head_major=head_major),
        out_shape=out_shape,
        grid=(b, s // tm, nj),
        in_specs=in_specs,
        out_specs=out_spec,
        scratch_shapes=[pltpu.VMEM((tm, d), BF16)] if sections is None else [],
        compiler_params=_params("arbitrary", "arbitrary", "arbitrary"),
        name=name,
    )(*args)


def _qk(q, k):
    return lax.dot_general(q, k, (((1,), (1,)), ((), ())), preferred_element_type=F32)


EXP2_SCALE = ATTN_SCALE * float(np.log2(np.e))
INV_ATTN_SCALE = 1.0 / ATTN_SCALE


def _softmax_pv(parts, sink=None):
    m = functools.reduce(jnp.maximum, [jnp.max(s, axis=-1, keepdims=True) for s, _ in parts])
    if sink is not None:
        m = jnp.maximum(m, sink)
    l = None
    o = None
    for s, v in parts:
        p = jnp.exp2((s - m) * EXP2_SCALE)
        ls = jnp.sum(p, axis=-1, keepdims=True)
        os_ = jnp.dot(p.astype(BF16), v, preferred_element_type=F32)
        l = ls if l is None else l + ls
        o = os_ if o is None else o + os_
    if sink is not None:
        l = l + jnp.exp2((sink - m) * EXP2_SCALE)
    return o / l


def _run_chains(n_blk, n_inner, scores, probs, weighted_values, unroll=1):
    per_body = unroll * n_inner
    assert per_body % 2 == 0 and n_blk % unroll == 0

    def chain(it, n):
        return it * unroll + n // n_inner, n % n_inner

    scores(0, 0, 0)

    def body(it, carry):
        for n in range(per_body):
            blk, g = chain(it, n + 1)
            scores(jnp.minimum(blk, n_blk - 1) if n + 1 == per_body else blk, g, (n + 1) % 2)
            probs(n % n_inner, n % 2)
            blk, g = chain(it, n - 1)
            weighted_values(jnp.maximum(blk, 0) if n == 0 else blk, g, (n - 1) % 2)
        return carry

    lax.fori_loop(0, n_blk // unroll, body, 0)
    weighted_values(n_blk - 1, n_inner - 1, 1)


GATTN_ROWS = 512


def _gattn_kernel(q_ref, k_ref, v_ref, kc_ref, vc_ref, o_ref, kall_ref, vall_ref, s_ref, p_ref):
    c = kc_ref.shape[0]

    n_blk = q_ref.shape[0] // GATTN_ROWS

    kall_ref[:c, :] = kc_ref[...]
    kall_ref[c:, :] = k_ref[...]
    vall_ref[:c, :HEAD_DIM] = vc_ref[...]
    vall_ref[c:, :HEAD_DIM] = v_ref[...]
    lane = lax.broadcasted_iota(jnp.int32, (vall_ref.shape[0], HEAD_DIM), 1)
    vall_ref[:, HEAD_DIM:] = jnp.where(lane == 0, 1.0, 0.0).astype(BF16)

    def rows_of(blk):
        return pl.ds(pl.multiple_of(blk * GATTN_ROWS, GATTN_ROWS), GATTN_ROWS)

    def cols_of(g):
        return slice(g * HEAD_DIM, (g + 1) * HEAD_DIM)

    def scores(blk, g, slot):
        s_ref[slot] = _qk(q_ref[rows_of(blk), cols_of(g)], kall_ref[...])

    def probs(g, slot):
        m = jnp.max(s_ref[slot], axis=-1, keepdims=True)
        p_ref[slot] = jnp.exp2((s_ref[slot] - m) * EXP2_SCALE).astype(BF16)

    def weighted_values(blk, g, slot):
        ol = jnp.dot(p_ref[slot], vall_ref[...], preferred_element_type=F32)
        o = ol[:, :HEAD_DIM] / ol[:, HEAD_DIM:HEAD_DIM + 1]
        o_ref[rows_of(blk), cols_of(g)] = o.astype(o_ref.dtype)

    p_ref[1] = jnp.ones(p_ref.shape[1:], BF16)
    _run_chains(n_blk, GQA_GROUP, scores, probs, weighted_values)


def _gattn(qkv, qkv_c, *, q_blk, k_blk, v_blk):
    b, s, _ = qkv.shape
    c = qkv_c.shape[1]
    gw = GQA_GROUP * HEAD_DIM
    assert GQA_GROUP % 2 == 0 and s % GATTN_ROWS == 0
    single = pl.Buffered(1)
    return pl.pallas_call(
        _gattn_kernel,
        out_shape=jax.ShapeDtypeStruct((b, s, A_Q_HEADS * HEAD_DIM), BF16),
        grid=(b, A_KV_HEADS),
        in_specs=[
            pl.BlockSpec((None, s, gw), lambda bi, kv: (bi, 0, q_blk + kv), pipeline_mode=single),
            pl.BlockSpec((None, s, HEAD_DIM), lambda bi, kv: (bi, 0, k_blk + kv), pipeline_mode=single),
            pl.BlockSpec((None, s, HEAD_DIM), lambda bi, kv: (bi, 0, v_blk + kv), pipeline_mode=single),
            pl.BlockSpec((None, c, HEAD_DIM), lambda bi, kv: (bi, 0, k_blk + kv)),
            pl.BlockSpec((None, c, HEAD_DIM), lambda bi, kv: (bi, 0, v_blk + kv)),
        ],
        out_specs=pl.BlockSpec((None, s, gw), lambda bi, kv: (bi, 0, kv)),
        scratch_shapes=[
            pltpu.VMEM((c + s, HEAD_DIM), BF16),
            pltpu.VMEM((c + s, 2 * HEAD_DIM), BF16),
            pltpu.VMEM((2, GATTN_ROWS, c + s), F32),
            pltpu.VMEM((2, GATTN_ROWS, c + s), BF16),
        ],
        compiler_params=_params("arbitrary", "arbitrary"),
        name="gattn",
    )(qkv, qkv, qkv, qkv_c, qkv_c)


WATTN_TQ = 4 * WINDOW
WATTN_NK = WATTN_TQ + 2 * WINDOW


def _wattn_kernel(sink_ref, q_ref, k_ref, v_ref, kc_ref, vc_ref, o_ref,
                  mask_ref, s_ref, sc_ref, p_ref, pc_ref, l_ref):
    kv = pl.program_id(1)
    seq = q_ref.shape[0]
    n_tile = seq // WATTN_TQ

    def rows_of(tile):
        return pl.ds(pl.multiple_of(tile * WATTN_TQ, WATTN_TQ), WATTN_TQ)

    def key0_of(tile):
        lo, hi = 0, seq - WATTN_NK
        k0 = tile * WATTN_TQ - WINDOW
        return min(max(k0, lo), hi) if isinstance(tile, int) else jnp.clip(k0, lo, hi)

    def key_rows_of(tile):
        return pl.ds(pl.multiple_of(key0_of(tile), WINDOW), WATTN_NK)

    def cols_of(g):
        return slice(g * HEAD_DIM, (g + 1) * HEAD_DIM)

    for var, tile in enumerate((0, 1, n_tile - 1)):
        shift = key0_of(tile) - tile * WATTN_TQ
        dist = (shift + lax.broadcasted_iota(jnp.int32, (WATTN_TQ, WATTN_NK), 1)
                - lax.broadcasted_iota(jnp.int32, (WATTN_TQ, WATTN_NK), 0))
        mask_ref[var] = jnp.where(jnp.abs(dist) <= WINDOW, 0.0, NEG)

    def scores(tile, g, slot):
        q = q_ref[rows_of(tile), cols_of(g)]
        var = jnp.where(tile == 0, 0, jnp.where(tile == n_tile - 1, 2, 1))
        s_ref[slot] = _qk(q, k_ref[key_rows_of(tile), :]) + mask_ref[var]
        sc_ref[slot] = _qk(q, kc_ref[...])

    def probs(g, slot):
        s, sc = s_ref[slot], sc_ref[slot]
        sink = sink_ref[kv, g] * INV_ATTN_SCALE
        m = jnp.maximum(jnp.max(s, axis=-1, keepdims=True), jnp.max(sc, axis=-1, keepdims=True))
        m = jnp.maximum(m, sink)
        p = jnp.exp2((s - m) * EXP2_SCALE)
        pc = jnp.exp2((sc - m) * EXP2_SCALE)
        l_ref[slot] = (jnp.sum(p, axis=-1, keepdims=True) + jnp.sum(pc, axis=-1, keepdims=True)
                       + jnp.exp2((sink - m) * EXP2_SCALE))
        p_ref[slot] = p.astype(BF16)
        pc_ref[slot] = pc.astype(BF16)

    def weighted_values(tile, g, slot):
        o = (jnp.dot(p_ref[slot], v_ref[key_rows_of(tile), :], preferred_element_type=F32)
             + jnp.dot(pc_ref[slot], vc_ref[...], preferred_element_type=F32))
        o_ref[rows_of(tile), cols_of(g)] = (o / l_ref[slot]).astype(o_ref.dtype)

    p_ref[1] = jnp.ones(p_ref.shape[1:], BF16)
    pc_ref[1] = jnp.ones(pc_ref.shape[1:], BF16)
    l_ref[1] = jnp.ones(l_ref.shape[1:], F32)
    _run_chains(seq // WATTN_TQ, GQA_GROUP, scores, probs, weighted_values)


def _wattn(sink_kg, qkv, qkv_c, *, q_blk, k_blk, v_blk):
    b, s, _ = qkv.shape
    c = qkv_c.shape[1]
    gw = GQA_GROUP * HEAD_DIM
    assert s % WATTN_TQ == 0 and s >= 3 * WATTN_TQ

    def col_map(col):
        return lambda bi, kv: (bi, 0, col + kv)

    return pl.pallas_call(
        _wattn_kernel,
        out_shape=jax.ShapeDtypeStruct((b, s, B_Q_HEADS * HEAD_DIM), BF16),
        grid=(b, B_KV_HEADS),
        in_specs=[
            pl.BlockSpec(memory_space=pltpu.SMEM),
            pl.BlockSpec((None, s, gw), col_map(q_blk)),
            pl.BlockSpec((None, s, HEAD_DIM), col_map(k_blk)),
            pl.BlockSpec((None, s, HEAD_DIM), col_map(v_blk)),
            pl.BlockSpec((None, c, HEAD_DIM), col_map(k_blk)),
            pl.BlockSpec((None, c, HEAD_DIM), col_map(v_blk)),
        ],
        out_specs=pl.BlockSpec((None, s, gw), col_map(0)),
        scratch_shapes=[
            pltpu.VMEM((3, WATTN_TQ, WATTN_NK), F32),
            pltpu.VMEM((2, WATTN_TQ, WATTN_NK), F32),
            pltpu.VMEM((2, WATTN_TQ, c), F32),
            pltpu.VMEM((2, WATTN_TQ, WATTN_NK), BF16),
            pltpu.VMEM((2, WATTN_TQ, c), BF16),
            pltpu.VMEM((2, WATTN_TQ, 1), F32),
        ],
        compiler_params=_params("arbitrary", "arbitrary"),
        name="wattn",
    )(sink_kg, qkv, qkv, qkv, qkv_c, qkv_c)


def _cattn_kernel(sink_ref, q_ref, k_ref, v_ref, o_ref):
    grp = pl.program_id(1)
    k, v = k_ref[...], v_ref[...]
    for g in range(GQA_GROUP):
        q = q_ref[:, g * HEAD_DIM:(g + 1) * HEAD_DIM]
        o = _softmax_pv([(_qk(q, k), v)], sink=sink_ref[grp, g] * INV_ATTN_SCALE)
        o_ref[:, g * HEAD_DIM:(g + 1) * HEAD_DIM] = o.astype(o_ref.dtype)


def _cattn(sink_all, qkv_c):
    b, c, _ = qkv_c.shape
    gw = GQA_GROUP * HEAD_DIM
    n_grp = A_KV_HEADS + B_KV_HEADS
    q_map = lambda bi, g: (bi, 0, g + g // A_KV_HEADS)
    k_map = lambda bi, g: (bi, 0, 8 + g + 10 * (g // A_KV_HEADS))
    v_map = lambda bi, g: (bi, 0, 10 + g + 10 * (g // A_KV_HEADS))
    return pl.pallas_call(
        _cattn_kernel,
        out_shape=jax.ShapeDtypeStruct((b, c, N_HEADS * HEAD_DIM), BF16),
        grid=(b, n_grp),
        in_specs=[
            pl.BlockSpec(memory_space=pltpu.SMEM),
            pl.BlockSpec((None, c, gw), q_map),
            pl.BlockSpec((None, c, HEAD_DIM), k_map),
            pl.BlockSpec((None, c, HEAD_DIM), v_map),
        ],
        out_specs=pl.BlockSpec((None, c, gw), lambda bi, g: (bi, 0, g)),
        compiler_params=_params("arbitrary", "arbitrary"),
        name="cattn",
    )(sink_all, qkv_c, qkv_c, qkv_c)


NATTN_ROWS = 4
NATTN_TQ = NATTN_ROWS * GRID_W
NATTN_KROWS = 3 * NATTN_ROWS


NATTN_HG = 2
NATTN_UNROLL = 2
N_RI = 2 * NA_KH - 1
N_CI = 2 * NA_KW - 1


def _na_key_row0(tile, rows):
    lo, hi = 0, rows - NATTN_KROWS
    r0 = (tile - 1) * NATTN_ROWS
    if isinstance(tile, int):
        return min(max(r0, lo), hi)
    return jnp.clip(r0, lo, hi)


def _nattn_kernel(q_ref, k_ref, v_ref, kc_ref, vc_ref, bias_ref, o_ref,
                  vs_ref, vcs_ref, s_ref, sc_ref, p_ref, pc_ref):
    seq = q_ref.shape[1]
    c = kc_ref.shape[1]
    n_tile = seq // NATTN_TQ
    nk = NATTN_KROWS * GRID_W

    def rows_of(tile):
        return pl.ds(pl.multiple_of(tile * NATTN_TQ, NATTN_TQ), NATTN_TQ)

    def key_rows_of(tile):
        return pl.ds(pl.multiple_of(_na_key_row0(tile, seq // GRID_W) * GRID_W, NATTN_TQ), nk)

    def cols_of(g):
        return slice(g * HEAD_DIM, (g + 1) * HEAD_DIM)

    def ones_col(n):
        return jnp.where(lax.broadcasted_iota(jnp.int32, (n, HEAD_DIM), 1) == 0, 1.0, 0.0).astype(BF16)

    for g in range(NATTN_HG):
        vs_ref[g] = jnp.concatenate([v_ref[g], ones_col(seq)], axis=1)
        vcs_ref[g] = jnp.concatenate([vc_ref[g], ones_col(c)], axis=1)

    def scores(tile, _, slot):
        var = jnp.where(tile == 0, 0, jnp.where(tile == n_tile - 1, 2, 1))
        for g in range(NATTN_HG):
            q = q_ref[g, rows_of(tile), :]
            s_ref[slot, g] = _qk(q, k_ref[g, key_rows_of(tile), :]) + bias_ref[g, var]
            sc_ref[slot, g] = _qk(q, kc_ref[g])

    def probs(_, slot):
        for g in range(NATTN_HG):
            s, sc = s_ref[slot, g], sc_ref[slot, g]
            folded = jnp.concatenate([jnp.maximum(s[:, :c], sc), s[:, c:]], axis=1)
            m = jnp.max(folded, axis=-1, keepdims=True)
            p_ref[slot, g] = jnp.exp2((s - m) * EXP2_SCALE).astype(BF16)
            pc_ref[slot, g] = jnp.exp2((sc - m) * EXP2_SCALE).astype(BF16)

    def weighted_values(tile, _, slot):
        for g in range(NATTN_HG):
            ol = (jnp.dot(p_ref[slot, g], vs_ref[g, key_rows_of(tile), :], preferred_element_type=F32)
                  + jnp.dot(pc_ref[slot, g], vcs_ref[g], preferred_element_type=F32))
            o = ol[:, :HEAD_DIM] / ol[:, HEAD_DIM:HEAD_DIM + 1]
            o_ref[rows_of(tile), cols_of(g)] = o.astype(o_ref.dtype)

    p_ref[1] = jnp.ones(p_ref.shape[1:], BF16)
    pc_ref[1] = jnp.ones(pc_ref.shape[1:], BF16)
    _run_chains(n_tile, 1, scores, probs, weighted_values, unroll=NATTN_UNROLL)


def _na_bias_kernel(rpb_ref, o_ref, u_ref, *, rows):
    base = pl.program_id(0) * (N_RI * N_CI)
    kh = min(NA_KH, rows)

    def fill(sg, carry):
        q0 = pl.multiple_of(sg * SUBLANES, SUBLANES)
        qc = q0 + lax.broadcasted_iota(jnp.int32, (SUBLANES, LANES), 0)
        kc = lax.broadcasted_iota(jnp.int32, (SUBLANES, LANES), 1) & (GRID_W - 1)
        diag = kc - qc + (NA_KW - 1)
        us = [jnp.zeros((SUBLANES, LANES), F32) for _ in range(N_RI)]
        for ci in range(N_CI):
            hit = diag == ci
            for a in range(N_RI):
                us[a] = jnp.where(hit, rpb_ref[base + a * N_CI + ci], us[a])
        for a in range(N_RI):
            u_ref[a, pl.ds(q0, SUBLANES), :] = us[a] * INV_ATTN_SCALE
        return carry

    lax.fori_loop(0, GRID_W // SUBLANES, fill, 0)

    qc = lax.broadcasted_iota(jnp.int32, (GRID_W, LANES), 0)
    lane = lax.broadcasted_iota(jnp.int32, (GRID_W, LANES), 1)
    kc = lane & (GRID_W - 1)
    cs = jnp.clip(qc - NA_KW // 2, 0, GRID_W - NA_KW)
    col_valid = (kc >= cs) & (kc < cs + NA_KW)
    low = lane < GRID_W
    masks = {(True, True): col_valid, (True, False): col_valid & low,
             (False, True): col_valid & jnp.logical_not(low)}
    neg_blk = jnp.full((GRID_W, LANES), NEG, F32)
    n_tile = rows // NATTN_ROWS
    for var, tile in enumerate((0, 1, n_tile - 1)):
        r0 = tile * NATTN_ROWS
        k0 = _na_key_row0(tile, rows)
        for qr in range(NATTN_ROWS):
            rs = min(max(r0 + qr - kh // 2, 0), rows - kh)
            for jj in range(NATTN_KROWS // 2):
                kr = k0 + 2 * jj
                ok = (rs <= kr < rs + kh, rs <= kr + 1 < rs + kh)
                if ok == (False, False):
                    blk = neg_blk
                else:
                    a = kr - (r0 + qr) + NA_KH - 1
                    a_lo = min(max(a, 0), N_RI - 1)
                    a_hi = min(max(a + 1, 0), N_RI - 1)
                    pair = jnp.where(low, u_ref[a_lo], u_ref[a_hi])
                    blk = jnp.where(masks[ok], pair, NEG)
                o_ref[var, qr * GRID_W:(qr + 1) * GRID_W, jj * LANES:(jj + 1) * LANES] = blk


def _na_bias_tables(rpb, rows):
    h = rpb.shape[0]
    assert 2 * GRID_W == LANES and rows >= 3 * NATTN_ROWS and NATTN_ROWS == NA_KH // 2
    return pl.pallas_call(
        functools.partial(_na_bias_kernel, rows=rows),
        out_shape=jax.ShapeDtypeStruct((h, 3, NATTN_TQ, NATTN_KROWS * GRID_W), F32),
        grid=(h,),
        in_specs=[pl.BlockSpec(memory_space=pltpu.SMEM)],
        out_specs=pl.BlockSpec((None, 3, NATTN_TQ, NATTN_KROWS * GRID_W), lambda hh: (hh, 0, 0, 0)),
        scratch_shapes=[pltpu.VMEM((N_RI, GRID_W, LANES), F32)],
        compiler_params=_params("arbitrary"),
        name="na_bias",
    )(rpb.reshape(-1))


def _nattn(qkv, kv_c, bias):
    b, _, s, _ = qkv.shape
    c = kv_c.shape[2]
    h = C_HEADS
    ng = h // NATTN_HG
    gw = NATTN_HG * HEAD_DIM
    nk = NATTN_KROWS * GRID_W
    assert s % NATTN_TQ == 0

    def head_map(first_group):
        return lambda hg, bi: (bi, first_group + hg, 0, 0)

    return pl.pallas_call(
        _nattn_kernel,
        out_shape=jax.ShapeDtypeStruct((b, s, h * HEAD_DIM), BF16),
        grid=(ng, b),
        in_specs=[
            pl.BlockSpec((None, NATTN_HG, s, HEAD_DIM), head_map(0)),
            pl.BlockSpec((None, NATTN_HG, s, HEAD_DIM), head_map(ng)),
            pl.BlockSpec((None, NATTN_HG, s, HEAD_DIM), head_map(2 * ng)),
            pl.BlockSpec((None, NATTN_HG, c, HEAD_DIM), head_map(0)),
            pl.BlockSpec((None, NATTN_HG, c, HEAD_DIM), head_map(ng)),
            pl.BlockSpec((NATTN_HG, 3, NATTN_TQ, nk), lambda hg, bi: (hg, 0, 0, 0),
                         pipeline_mode=pl.Buffered(1)),
        ],
        out_specs=pl.BlockSpec((None, s, gw), lambda hg, bi: (bi, 0, hg)),
        scratch_shapes=[
            pltpu.VMEM((NATTN_HG, s, 2 * HEAD_DIM), BF16),
            pltpu.VMEM((NATTN_HG, c, 2 * HEAD_DIM), BF16),
            pltpu.VMEM((2, NATTN_HG, NATTN_TQ, nk), F32),
            pltpu.VMEM((2, NATTN_HG, NATTN_TQ, c), F32),
            pltpu.VMEM((2, NATTN_HG, NATTN_TQ, nk), BF16),
            pltpu.VMEM((2, NATTN_HG, NATTN_TQ, c), BF16),
        ],
        compiler_params=_params("arbitrary", "arbitrary"),
        name="nattn",
    )(qkv, qkv, qkv, kv_c, kv_c, bias)


OUTPROJ_ROWS = 256


def _resid_mlp_kernel(*refs, widths, final_norm):
    x_ref, wo_ref, g1_ref, nw_ref, sh_ref, sc_ref, g2_ref, w1_ref, w2_ref = refs[:9]
    rest = refs[9:]
    o_refs = rest[:len(widths)]
    rest = rest[len(widths):]
    if final_norm:
        fw_ref = rest[0]
        rest = rest[1:]
    y_ref, x1_ref, h_ref = rest
    f = pl.program_id(2)
    tm = x_ref.shape[0]

    last = pl.num_programs(2) - 1

    def mlp_partial(rows=slice(None)):
        a = jnp.dot(h_ref[rows, :], w1_ref[...], preferred_element_type=F32)
        a = jnp.square(jnp.maximum(a, 0.0)).astype(BF16)
        return jnp.dot(a, w2_ref[...], preferred_element_type=F32)

    @pl.when(f == 0)
    def _():
        for r0 in range(0, tm, OUTPROJ_ROWS):
            rows = slice(r0, min(r0 + OUTPROJ_ROWS, tm))
            y = None
            k0 = 0
            for o_ref, kw in zip(o_refs, widths):
                part = jnp.dot(o_ref[rows, :], wo_ref[k0:k0 + kw, :], preferred_element_type=F32)
                y = part if y is None else y + part
                k0 += kw
            x1 = x_ref[rows, :] + g1_ref[...] * y
            x1_ref[rows, :] = x1
            h_ref[rows, :] = _norm_modulate(x1, nw_ref[...], sh_ref[...], sc_ref[...]).astype(BF16)
        y_ref[...] = mlp_partial()

    @pl.when((f > 0) & (f < last))
    def _():
        y_ref[...] += mlp_partial()

    @pl.when(f == last)
    def _():
        for r0 in range(0, tm, OUTPROJ_ROWS):
            rows = slice(r0, min(r0 + OUTPROJ_ROWS, tm))
            y = x1_ref[rows, :] + g2_ref[...] * (y_ref[rows, :] + mlp_partial(rows))
            if final_norm:
                y = y * lax.rsqrt(jnp.mean(y * y, axis=-1, keepdims=True) + NORM_EPS) * fw_ref[...]
            y_ref[rows, :] = y


def _resid_mlp(x, o_parts, w_out, g1, nw, shift, scale, g2, w1, w2, *, tm, tf, final_w=None):
    b, s, d = x.shape
    dff = w1.shape[1]
    widths = tuple(o.shape[2] for o in o_parts)
    assert sum(widths) == w_out.shape[0] and dff // tf >= 2
    vec = pl.BlockSpec((None, 1, d), lambda bi, i, f: (bi, 0, 0))
    in_specs = [
        pl.BlockSpec((None, tm, d), lambda bi, i, f: (bi, i, 0)),
        pl.BlockSpec(w_out.shape, lambda bi, i, f: (0, 0), pipeline_mode=pl.Buffered(1)),
        vec,
        pl.BlockSpec((1, d), lambda bi, i, f: (0, 0)),
        vec, vec, vec,
        pl.BlockSpec((d, tf), lambda bi, i, f: (0, f)),
        pl.BlockSpec((tf, d), lambda bi, i, f: (f, 0)),
    ] + [pl.BlockSpec((None, tm, kw), lambda bi, i, f: (bi, i, 0)) for kw in widths]
    args = [x, w_out, g1, nw.reshape(1, d), shift, scale, g2, w1, w2, *o_parts]
    if final_w is not None:
        in_specs.append(pl.BlockSpec((1, d), lambda bi, i, f: (0, 0)))
        args.append(final_w.reshape(1, d))
    return pl.pallas_call(
        functools.partial(_resid_mlp_kernel, widths=widths, final_norm=final_w is not None),
        out_shape=jax.ShapeDtypeStruct(x.shape, x.dtype),
        grid=(b, s // tm, dff // tf),
        in_specs=in_specs,
        out_specs=pl.BlockSpec((None, tm, d), lambda bi, i, f: (bi, i, 0)),
        scratch_shapes=[pltpu.VMEM((tm, d), F32), pltpu.VMEM((tm, d), BF16)],
        compiler_params=_params("arbitrary", "arbitrary", "arbitrary"),
        name="resid_mlp",
    )(*args)


EVEN_SECTIONS = (
    (0, A_Q_HEADS, "q_norm"),
    (A_Q_HEADS * HEAD_DIM, A_KV_HEADS, "k_norm"),
    ((A_Q_HEADS + A_KV_HEADS) * HEAD_DIM, A_KV_HEADS, "v"),
    ((A_Q_HEADS + 2 * A_KV_HEADS) * HEAD_DIM, B_Q_HEADS, "q"),
    ((A_Q_HEADS + 2 * A_KV_HEADS + B_Q_HEADS) * HEAD_DIM, B_KV_HEADS, "k"),
    ((A_Q_HEADS + 2 * A_KV_HEADS + B_Q_HEADS + B_KV_HEADS) * HEAD_DIM, B_KV_HEADS, "v"),
)
LAT_TM = 512
MLP_TF = 1024


def kernel(x, c, ctx, c_ctx, ada_w, ada_b, norm_w, mlp_w1, mlp_w2, ev_w_in, ev_w_out, ev_q_norm,
           ev_k_norm, ev_sink, od_w_in, od_w_out, od_rpb, final_norm_w):
    b, s, d = x.shape
    n_ctx = ctx.shape[1]
    depth = ada_w.shape[0]
    rows = s // GRID_W

    mod = _ada_mod(jnp.concatenate([c, c_ctx[None, :]], axis=0), ada_w, ada_b)
    rope_tabs = _rope_tables(s)

    def mod_vecs(layer):
        lat = [mod[layer, :b, k * d:(k + 1) * d][:, None, :] for k in range(6)]
        cx = [mod[layer, b:b + 1, k * d:(k + 1) * d][:, None, :] for k in range(6)]
        return lat, cx

    ctx = ctx.reshape(1, b * n_ctx, d)

    for layer in range(depth):
        need_ctx = layer < depth - 1
        (sh1, sc1, g1, sh2, sc2, g2), (sh1c, sc1c, g1c, sh2c, sc2c, g2c) = mod_vecs(layer)
        j = layer // 2
        w1 = _to_bf16(mlp_w1, layer)
        w2 = _to_bf16(mlp_w2, layer)
        if layer % 2 == 0:
            w_in = _to_bf16(ev_w_in, j)
            w_out = _to_bf16(ev_w_out, j)
            n_in = w_in.shape[1]
            norms = (ev_q_norm[j], ev_k_norm[j])
            qkv = _proj(x, norm_w[layer, 0], sh1, sc1, w_in, tm=LAT_TM, tn=n_in,
                        sections=EVEN_SECTIONS, head_norms=norms, rope_tabs=rope_tabs, name="proj_even")
            qkv_c = _proj(ctx, norm_w[layer, 0], sh1c, sc1c, w_in, tm=b * n_ctx, tn=n_in,
                          sections=EVEN_SECTIONS, head_norms=norms, name="proj_even_ctx")
            qkv_c = qkv_c.reshape(b, n_ctx, n_in)
            oa = _gattn(qkv, qkv_c, q_blk=0, k_blk=A_Q_HEADS, v_blk=A_Q_HEADS + A_KV_HEADS)
            sink_kg = ev_sink[j].reshape(B_KV_HEADS, GQA_GROUP).astype(F32)
            qb_blk = (A_Q_HEADS + 2 * A_KV_HEADS) // GQA_GROUP
            kb_blk = A_Q_HEADS + 2 * A_KV_HEADS + B_Q_HEADS
            ob = _wattn(sink_kg, qkv, qkv_c, q_blk=qb_blk, k_blk=kb_blk, v_blk=kb_blk + B_KV_HEADS)
            o_parts = [oa, ob]
            if need_ctx:
                sink_all = jnp.concatenate([jnp.full((A_KV_HEADS, GQA_GROUP), NEG, F32), sink_kg], axis=0)
                oc_parts = [_cattn(sink_all, qkv_c).reshape(1, b * n_ctx, N_HEADS * HEAD_DIM)]
        else:
            w_in = _to_bf16(od_w_in, j)
            w_out = _to_bf16(od_w_out, j)
            hd = C_HEADS * HEAD_DIM
            qkv = _proj(x, norm_w[layer, 0], sh1, sc1, w_in, tm=LAT_TM, tn=hd, head_major=True,
                        name="proj_odd")
            per_batch = lambda v: jnp.broadcast_to(v, (b, 1, d))
            kv_c = _proj(ctx.reshape(b, n_ctx, d), norm_w[layer, 0], per_batch(sh1c), per_batch(sc1c),
                         w_in, tm=n_ctx, tn=hd, n_off=1, n_out=2 * hd, head_major=True,
                         name="proj_odd_ctx")
            o_parts = [_nattn(qkv, kv_c, _na_bias_tables(od_rpb[j], rows))]
            assert not need_ctx, "the odd layer is the last one at this depth"
        x = _resid_mlp(x, o_parts, w_out, g1, norm_w[layer, 1], sh2, sc2, g2, w1, w2, tm=LAT_TM,
                       tf=MLP_TF, final_w=None if need_ctx else final_norm_w)
        if need_ctx:
            ctx = _resid_mlp(ctx, oc_parts, w_out, g1c, norm_w[layer, 1], sh2c, sc2c, g2c, w1, w2,
                             tm=b * n_ctx, tf=MLP_TF)
    return x
```

```python
import functools

import jax
import jax.numpy as jnp
import numpy as np
from jax import lax
from jax.experimental import pallas as pl
from jax.experimental.pallas import tpu as pltpu

D_MODEL = 2048
GRID_W = 64
HEAD_DIM = 128
N_HEADS = D_MODEL // HEAD_DIM
A_Q_HEADS = N_HEADS // 2
A_KV_HEADS = max(1, A_Q_HEADS // 4)
B_Q_HEADS = N_HEADS - A_Q_HEADS
B_KV_HEADS = max(1, B_Q_HEADS // 4)
GQA_GROUP = A_Q_HEADS // A_KV_HEADS
C_HEADS = N_HEADS
WINDOW = 128
NA_KH = 8
NA_KW = 16
D_FF = 4 * D_MODEL
ROPE_THETA = 10000.0
ROPE_PAIRS = HEAD_DIM // 4
NORM_EPS = 1e-6
NEG = -1e30
ATTN_SCALE = HEAD_DIM ** -0.5

V7X_VMEM_BYTES = 64 * 1024 * 1024
VMEM_LIMIT = V7X_VMEM_BYTES - 6 * 1024 * 1024
LANES = 128
SUBLANES = 8

F32 = jnp.float32
BF16 = jnp.bfloat16


def _params(*sem):
    return pltpu.CompilerParams(dimension_semantics=sem, vmem_limit_bytes=VMEM_LIMIT)


ADA_ROWS = 3
ADA_TN = 1024


def _ada_kernel(cb_ref, w_ref, b_ref, o_ref, s_ref):
    @pl.when((pl.program_id(0) == 0) & (pl.program_id(1) == 0))
    def _():
        cvals = cb_ref[...]
        s_ref[...] = cvals * (1.0 / (1.0 + jnp.exp(-cvals)))

    nchunk = ADA_TN // LANES

    def body(kk, accs):
        k0 = pl.multiple_of(kk * SUBLANES, SUBLANES)
        w = w_ref[pl.ds(k0, SUBLANES), :]
        out = []
        for r in range(ADA_ROWS):
            s = s_ref[r, pl.ds(k0, SUBLANES), :]
            for cch in range(nchunk):
                out.append(accs[r * nchunk + cch] + w[:, cch * LANES:(cch + 1) * LANES] * s)
        return tuple(out)

    zeros = tuple(jnp.zeros((SUBLANES, LANES), F32) for _ in range(ADA_ROWS * nchunk))
    accs = lax.fori_loop(0, D_MODEL // SUBLANES, body, zeros, unroll=4)
    o_ref[...] = jnp.broadcast_to(b_ref[...], o_ref.shape)
    for r in range(ADA_ROWS):
        row = jnp.concatenate(
            [jnp.sum(accs[r * nchunk + cch], axis=0, keepdims=True) for cch in range(nchunk)], axis=1)
        o_ref[r:r + 1, :] = row + b_ref[...]


def _ada_mod(cc, ada_w, ada_b):
    depth, d, n = ada_w.shape
    cb = jnp.broadcast_to(cc[:, :, None], (ADA_ROWS, d, LANES))
    return pl.pallas_call(
        _ada_kernel,
        out_shape=jax.ShapeDtypeStruct((depth, SUBLANES, n), F32),
        grid=(depth, n // ADA_TN),
        in_specs=[
            pl.BlockSpec((ADA_ROWS, d, LANES), lambda l, j: (0, 0, 0)),
            pl.BlockSpec((None, d, ADA_TN), lambda l, j: (l, 0, j)),
            pl.BlockSpec((None, 1, ADA_TN), lambda l, j: (l, 0, j)),
        ],
        out_specs=pl.BlockSpec((None, SUBLANES, ADA_TN), lambda l, j: (l, 0, j)),
        scratch_shapes=[pltpu.VMEM((ADA_ROWS, d, LANES), F32)],
        compiler_params=_params("arbitrary", "arbitrary"),
        name="ada_mod",
    )(cb, ada_w, ada_b.reshape(depth, 1, n))


CAST_BLOCK_BYTES = 8 * 1024 * 1024


def _cast_kernel(w_ref, o_ref):
    o_ref[...] = w_ref[...].astype(o_ref.dtype)


def _to_bf16(w_stack, idx):
    _, r, c = w_stack.shape
    tr = r
    while tr * c * w_stack.dtype.itemsize > CAST_BLOCK_BYTES and tr % 32 == 0:
        tr //= 2
    return pl.pallas_call(
        _cast_kernel,
        out_shape=jax.ShapeDtypeStruct((r, c), BF16),
        grid=(r // tr,),
        in_specs=[pl.BlockSpec((None, tr, c), lambda i: (idx, i, 0))],
        out_specs=pl.BlockSpec((tr, c), lambda i: (i, 0)),
        compiler_params=_params("arbitrary"),
        name="to_bf16",
    )(w_stack)


def _norm_modulate(x, nw, shift, scale):
    ms = jnp.mean(x * x, axis=-1, keepdims=True)
    y = x * lax.rsqrt(ms + NORM_EPS) * nw
    return y * (1.0 + scale) + shift


def _rope(y, cos, sin_lo, sin_hi):
    return (y * cos + pltpu.roll(y, HEAD_DIM - ROPE_PAIRS, 1) * sin_lo
            + pltpu.roll(y, ROPE_PAIRS, 1) * sin_hi)


def _rope_tables(seq):
    t = jnp.arange(seq)
    row = (t // GRID_W).astype(F32)
    col = (t % GRID_W).astype(F32)
    inv = ROPE_THETA ** (-jnp.arange(ROPE_PAIRS, dtype=F32) / ROPE_PAIRS)
    ang_r = row[:, None] * inv
    ang_c = col[:, None] * inv
    ang = jnp.concatenate([ang_r, ang_r, ang_c, ang_c], axis=-1)
    cos, sin = jnp.cos(ang), jnp.sin(ang)
    first = (jnp.arange(HEAD_DIM) % (2 * ROPE_PAIRS)) < ROPE_PAIRS
    sin_lo = jnp.where(first, -sin, 0.0)
    sin_hi = jnp.where(first, 0.0, sin)
    return cos, sin_lo, sin_hi


PROJ_ROWS = 256


def _proj_kernel(*refs, sections, rope, head_major):
    x_ref, nw_ref, sh_ref, sc_ref, w_ref = refs[:5]
    rest = refs[5:]
    if sections is not None:
        qn_ref, kn_ref = rest[:2]
        rest = rest[2:]
    if rope:
        cos_ref, slo_ref, shi_ref = rest[:3]
        rest = rest[3:]
    o_ref = rest[0]
    h_ref = rest[1] if sections is None else None
    tm = x_ref.shape[0]

    def project(h, rows):
        if sections is None:
            y = jnp.dot(h, w_ref[...], preferred_element_type=F32).astype(o_ref.dtype)
            if head_major:
                for hh in range(y.shape[1] // HEAD_DIM):
                    o_ref[hh, rows, :] = y[:, hh * HEAD_DIM:(hh + 1) * HEAD_DIM]
            else:
                o_ref[rows, :] = y
            return
        for c0, nh, kind in sections:
            y = jnp.dot(h, w_ref[:, c0:c0 + nh * HEAD_DIM], preferred_element_type=F32)
            for hh in range(nh):
                yh = y[:, hh * HEAD_DIM:(hh + 1) * HEAD_DIM]
                if kind in ("q_norm", "k_norm"):
                    hw = (qn_ref if kind == "q_norm" else kn_ref)[...]
                    yh = yh * lax.rsqrt(jnp.mean(yh * yh, axis=-1, keepdims=True) + NORM_EPS) * hw
                if rope and kind != "v":
                    yh = _rope(yh, cos_ref[rows, :], slo_ref[rows, :], shi_ref[rows, :])
                lo = c0 + hh * HEAD_DIM
                o_ref[rows, lo:lo + HEAD_DIM] = yh.astype(o_ref.dtype)

    @pl.when(pl.program_id(2) == 0)
    def _():
        for r0 in range(0, tm, PROJ_ROWS):
            rows = slice(r0, min(r0 + PROJ_ROWS, tm))
            h = _norm_modulate(x_ref[rows, :], nw_ref[...], sh_ref[...], sc_ref[...]).astype(BF16)
            if h_ref is not None:
                h_ref[rows, :] = h
            project(h, rows)

    if h_ref is not None:
        @pl.when(pl.program_id(2) > 0)
        def _():
            project(h_ref[...], slice(0, tm))


def _proj(x, nw, shift, scale, w, *, tm, tn, n_off=0, n_out=None, sections=None,
          head_norms=None, rope_tabs=None, head_major=False, name="proj"):
    b, s, d = x.shape
    n_out = w.shape[1] if n_out is None else n_out
    nj = n_out // tn
    assert not (head_major and sections is not None)
    if head_major:
        hpt = tn // HEAD_DIM
        out_shape = jax.ShapeDtypeStruct((b, n_out // HEAD_DIM, s, HEAD_DIM), BF16)
        out_spec = pl.BlockSpec((None, hpt, tm, HEAD_DIM), lambda bi, i, j: (bi, j, i, 0))
    else:
        out_shape = jax.ShapeDtypeStruct((b, s, n_out), BF16)
        out_spec = pl.BlockSpec((None, tm, tn), lambda bi, i, j: (bi, i, j))
    in_specs = [
        pl.BlockSpec((None, tm, d), lambda bi, i, j: (bi, i, 0)),
        pl.BlockSpec((1, d), lambda bi, i, j: (0, 0)),
        pl.BlockSpec((None, 1, d), lambda bi, i, j: (bi, 0, 0)),
        pl.BlockSpec((None, 1, d), lambda bi, i, j: (bi, 0, 0)),
        pl.BlockSpec((d, tn), lambda bi, i, j: (0, j + n_off)),
    ]
    args = [x, nw.reshape(1, d), shift, scale, w]
    if sections is not None:
        assert nj == 1
        in_specs += [pl.BlockSpec((1, HEAD_DIM), lambda bi, i, j: (0, 0))] * 2
        args += [head_norms[0].reshape(1, HEAD_DIM), head_norms[1].reshape(1, HEAD_DIM)]
    if rope_tabs is not None:
        in_specs += [pl.BlockSpec((tm, HEAD_DIM), lambda bi, i, j: (i, 0))] * 3
        args += list(rope_tabs)
    return pl.pallas_call(
        functools.partial(_proj_kernel, sections=sections, rope=rope_tabs is not None,
                          head_major=head_major),
        out_shape=out_shape,
        grid=(b, s // tm, nj),
        in_specs=in_specs,
        out_specs=out_spec,
        scratch_shapes=[pltpu.VMEM((tm, d), BF16)] if sections is None else [],
        compiler_params=_params("arbitrary", "arbitrary", "arbitrary"),
        name=name,
    )(*args)


def _qk(q, k):
    return lax.dot_general(q, k, (((1,), (1,)), ((), ())), preferred_element_type=F32)


EXP2_SCALE = ATTN_SCALE * float(np.log2(np.e))
INV_ATTN_SCALE = 1.0 / ATTN_SCALE


def _softmax_pv(parts, sink=None):
    m = functools.reduce(jnp.maximum, [jnp.max(s, axis=-1, keepdims=True) for s, _ in parts])
    if sink is not None:
        m = jnp.maximum(m, sink)
    l = None
    o = None
    for s, v in parts:
        p = jnp.exp2((s - m) * EXP2_SCALE)
        ls = jnp.sum(p, axis=-1, keepdims=True)
        os_ = jnp.dot(p.astype(BF16), v, preferred_element_type=F32)
        l = ls if l is None else l + ls
        o = os_ if o is None else o + os_
    if sink is not None:
        l = l + jnp.exp2((sink - m) * EXP2_SCALE)
    return o / l


def _run_chains(n_blk, n_inner, scores, probs, weighted_values, unroll=1):
    per_body = unroll * n_inner
    assert per_body % 2 == 0 and n_blk % unroll == 0

    def chain(it, n):
        return it * unroll + n // n_inner, n % n_inner

    scores(0, 0, 0)

    def body(it, carry):
        for n in range(per_body):
            blk, g = chain(it, n + 1)
            scores(jnp.minimum(blk, n_blk - 1) if n + 1 == per_body else blk, g, (n + 1) % 2)
            probs(n % n_inner, n % 2)
            blk, g = chain(it, n - 1)
            weighted_values(jnp.maximum(blk, 0) if n == 0 else blk, g, (n - 1) % 2)
        return carry

    lax.fori_loop(0, n_blk // unroll, body, 0)
    weighted_values(n_blk - 1, n_inner - 1, 1)


GATTN_ROWS = 512


def _gattn_kernel(q_ref, k_ref, v_ref, kc_ref, vc_ref, o_ref, kall_ref, vall_ref, s_ref, p_ref):
    c = kc_ref.shape[0]

    n_blk = q_ref.shape[0] // GATTN_ROWS

    kall_ref[:c, :] = kc_ref[...]
    kall_ref[c:, :] = k_ref[...]
    vall_ref[:c, :HEAD_DIM] = vc_ref[...]
    vall_ref[c:, :HEAD_DIM] = v_ref[...]
    lane = lax.broadcasted_iota(jnp.int32, (vall_ref.shape[0], HEAD_DIM), 1)
    vall_ref[:, HEAD_DIM:] = jnp.where(lane == 0, 1.0, 0.0).astype(BF16)

    def rows_of(blk):
        return pl.ds(pl.multiple_of(blk * GATTN_ROWS, GATTN_ROWS), GATTN_ROWS)

    def cols_of(g):
        return slice(g * HEAD_DIM, (g + 1) * HEAD_DIM)

    def scores(blk, g, slot):
        s_ref[slot] = _qk(q_ref[rows_of(blk), cols_of(g)], kall_ref[...])

    def probs(g, slot):
        m = jnp.max(s_ref[slot], axis=-1, keepdims=True)
        p_ref[slot] = jnp.exp2((s_ref[slot] - m) * EXP2_SCALE).astype(BF16)

    def weighted_values(blk, g, slot):
        ol = jnp.dot(p_ref[slot], vall_ref[...], preferred_element_type=F32)
        o = ol[:, :HEAD_DIM] / ol[:, HEAD_DIM:HEAD_DIM + 1]
        o_ref[rows_of(blk), cols_of(g)] = o.astype(o_ref.dtype)

    p_ref[1] = jnp.ones(p_ref.shape[1:], BF16)
    _run_chains(n_blk, GQA_GROUP, scores, probs, weighted_values)


def _gattn(qkv, qkv_c, *, q_blk, k_blk, v_blk):
    b, s, _ = qkv.shape
    c = qkv_c.shape[1]
    gw = GQA_GROUP * HEAD_DIM
    assert GQA_GROUP % 2 == 0 and s % GATTN_ROWS == 0
    single = pl.Buffered(1)
    return pl.pallas_call(
        _gattn_kernel,
        out_shape=jax.ShapeDtypeStruct((b, s, A_Q_HEADS * HEAD_DIM), BF16),
        grid=(b, A_KV_HEADS),
        in_specs=[
            pl.BlockSpec((None, s, gw), lambda bi, kv: (bi, 0, q_blk + kv), pipeline_mode=single),
            pl.BlockSpec((None, s, HEAD_DIM), lambda bi, kv: (bi, 0, k_blk + kv), pipeline_mode=single),
            pl.BlockSpec((None, s, HEAD_DIM), lambda bi, kv: (bi, 0, v_blk + kv), pipeline_mode=single),
            pl.BlockSpec((None, c, HEAD_DIM), lambda bi, kv: (bi, 0, k_blk + kv)),
            pl.BlockSpec((None, c, HEAD_DIM), lambda bi, kv: (bi, 0, v_blk + kv)),
        ],
        out_specs=pl.BlockSpec((None, s, gw), lambda bi, kv: (bi, 0, kv)),
        scratch_shapes=[
            pltpu.VMEM((c + s, HEAD_DIM), BF16),
            pltpu.VMEM((c + s, 2 * HEAD_DIM), BF16),
            pltpu.VMEM((2, GATTN_ROWS, c + s), F32),
            pltpu.VMEM((2, GATTN_ROWS, c + s), BF16),
        ],
        compiler_params=_params("arbitrary", "arbitrary"),
        name="gattn",
    )(qkv, qkv, qkv, qkv_c, qkv_c)


WATTN_TQ = 4 * WINDOW
WATTN_NK = WATTN_TQ + 2 * WINDOW


def _wattn_kernel(sink_ref, q_ref, k_ref, v_ref, kc_ref, vc_ref, o_ref,
                  mask_ref, s_ref, sc_ref, p_ref, pc_ref, l_ref):
    kv = pl.program_id(1)
    seq = q_ref.shape[0]
    n_tile = seq // WATTN_TQ

    def rows_of(tile):
        return pl.ds(pl.multiple_of(tile * WATTN_TQ, WATTN_TQ), WATTN_TQ)

    def key0_of(tile):
        lo, hi = 0, seq - WATTN_NK
        k0 = tile * WATTN_TQ - WINDOW
        return min(max(k0, lo), hi) if isinstance(tile, int) else jnp.clip(k0, lo, hi)

    def key_rows_of(tile):
        return pl.ds(pl.multiple_of(key0_of(tile), WINDOW), WATTN_NK)

    def cols_of(g):
        return slice(g * HEAD_DIM, (g + 1) * HEAD_DIM)

    for var, tile in enumerate((0, 1, n_tile - 1)):
        shift = key0_of(tile) - tile * WATTN_TQ
        dist = (shift + lax.broadcasted_iota(jnp.int32, (WATTN_TQ, WATTN_NK), 1)
                - lax.broadcasted_iota(jnp.int32, (WATTN_TQ, WATTN_NK), 0))
        mask_ref[var] = jnp.where(jnp.abs(dist) <= WINDOW, 0.0, NEG)

    def scores(tile, g, slot):
        q = q_ref[rows_of(tile), cols_of(g)]
        var = jnp.where(tile == 0, 0, jnp.where(tile == n_tile - 1, 2, 1))
        s_ref[slot] = _qk(q, k_ref[key_rows_of(tile), :]) + mask_ref[var]
        sc_ref[slot] = _qk(q, kc_ref[...])

    def probs(g, slot):
        s, sc = s_ref[slot], sc_ref[slot]
        sink = sink_ref[kv, g] * INV_ATTN_SCALE
        m = jnp.maximum(jnp.max(s, axis=-1, keepdims=True), jnp.max(sc, axis=-1, keepdims=True))
        m = jnp.maximum(m, sink)
        p = jnp.exp2((s - m) * EXP2_SCALE)
        pc = jnp.exp2((sc - m) * EXP2_SCALE)
        l_ref[slot] = (jnp.sum(p, axis=-1, keepdims=True) + jnp.sum(pc, axis=-1, keepdims=True)
                       + jnp.exp2((sink - m) * EXP2_SCALE))
        p_ref[slot] = p.astype(BF16)
        pc_ref[slot] = pc.astype(BF16)

    def weighted_values(tile, g, slot):
        o = (jnp.dot(p_ref[slot], v_ref[key_rows_of(tile), :], preferred_element_type=F32)
             + jnp.dot(pc_ref[slot], vc_ref[...], preferred_element_type=F32))
        o_ref[rows_of(tile), cols_of(g)] = (o / l_ref[slot]).astype(o_ref.dtype)

    p_ref[1] = jnp.ones(p_ref.shape[1:], BF16)
    pc_ref[1] = jnp.ones(pc_ref.shape[1:], BF16)
    l_ref[1] = jnp.ones(l_ref.shape[1:], F32)
    _run_chains(seq // WATTN_TQ, GQA_GROUP, scores, probs, weighted_values)


def _wattn(sink_kg, qkv, qkv_c, *, q_blk, k_blk, v_blk):
    b, s, _ = qkv.shape
    c = qkv_c.shape[1]
    gw = GQA_GROUP * HEAD_DIM
    assert s % WATTN_TQ == 0 and s >= 3 * WATTN_TQ

    def col_map(col):
        return lambda bi, kv: (bi, 0, col + kv)

    return pl.pallas_call(
        _wattn_kernel,
        out_shape=jax.ShapeDtypeStruct((b, s, B_Q_HEADS * HEAD_DIM), BF16),
        grid=(b, B_KV_HEADS),
        in_specs=[
            pl.BlockSpec(memory_space=pltpu.SMEM),
            pl.BlockSpec((None, s, gw), col_map(q_blk)),
            pl.BlockSpec((None, s, HEAD_DIM), col_map(k_blk)),
            pl.BlockSpec((None, s, HEAD_DIM), col_map(v_blk)),
            pl.BlockSpec((None, c, HEAD_DIM), col_map(k_blk)),
            pl.BlockSpec((None, c, HEAD_DIM), col_map(v_blk)),
        ],
        out_specs=pl.BlockSpec((None, s, gw), col_map(0)),
        scratch_shapes=[
            pltpu.VMEM((3, WATTN_TQ, WATTN_NK), F32),
            pltpu.VMEM((2, WATTN_TQ, WATTN_NK), F32),
            pltpu.VMEM((2, WATTN_TQ, c), F32),
            pltpu.VMEM((2, WATTN_TQ, WATTN_NK), BF16),
            pltpu.VMEM((2, WATTN_TQ, c), BF16),
            pltpu.VMEM((2, WATTN_TQ, 1), F32),
        ],
        compiler_params=_params("arbitrary", "arbitrary"),
        name="wattn",
    )(sink_kg, qkv, qkv, qkv, qkv_c, qkv_c)


def _cattn_kernel(sink_ref, q_ref, k_ref, v_ref, o_ref):
    grp = pl.program_id(1)
    k, v = k_ref[...], v_ref[...]
    for g in range(GQA_GROUP):
        q = q_ref[:, g * HEAD_DIM:(g + 1) * HEAD_DIM]
        o = _softmax_pv([(_qk(q, k), v)], sink=sink_ref[grp, g] * INV_ATTN_SCALE)
        o_ref[:, g * HEAD_DIM:(g + 1) * HEAD_DIM] = o.astype(o_ref.dtype)


def _cattn(sink_all, qkv_c):
    b, c, _ = qkv_c.shape
    gw = GQA_GROUP * HEAD_DIM
    n_grp = A_KV_HEADS + B_KV_HEADS
    q_map = lambda bi, g: (bi, 0, g + g // A_KV_HEADS)
    k_map = lambda bi, g: (bi, 0, 8 + g + 10 * (g // A_KV_HEADS))
    v_map = lambda bi, g: (bi, 0, 10 + g + 10 * (g // A_KV_HEADS))
    return pl.pallas_call(
        _cattn_kernel,
        out_shape=jax.ShapeDtypeStruct((b, c, N_HEADS * HEAD_DIM), BF16),
        grid=(b, n_grp),
        in_specs=[
            pl.BlockSpec(memory_space=pltpu.SMEM),
            pl.BlockSpec((None, c, gw), q_map),
            pl.BlockSpec((None, c, HEAD_DIM), k_map),
            pl.BlockSpec((None, c, HEAD_DIM), v_map),
        ],
        out_specs=pl.BlockSpec((None, c, gw), lambda bi, g: (bi, 0, g)),
        compiler_params=_params("arbitrary", "arbitrary"),
        name="cattn",
    )(sink_all, qkv_c, qkv_c, qkv_c)


NATTN_ROWS = 4
NATTN_TQ = NATTN_ROWS * GRID_W
NATTN_KROWS = 3 * NATTN_ROWS


NATTN_HG = 2
NATTN_UNROLL = 2
N_RI = 2 * NA_KH - 1
N_CI = 2 * NA_KW - 1


def _na_key_row0(tile, rows):
    lo, hi = 0, rows - NATTN_KROWS
    r0 = (tile - 1) * NATTN_ROWS
    if isinstance(tile, int):
        return min(max(r0, lo), hi)
    return jnp.clip(r0, lo, hi)


def _nattn_kernel(q_ref, k_ref, v_ref, kc_ref, vc_ref, bias_ref, o_ref,
                  vs_ref, vcs_ref, s_ref, sc_ref, p_ref, pc_ref):
    seq = q_ref.shape[1]
    c = kc_ref.shape[1]
    n_tile = seq // NATTN_TQ
    nk = NATTN_KROWS * GRID_W

    def rows_of(tile):
        return pl.ds(pl.multiple_of(tile * NATTN_TQ, NATTN_TQ), NATTN_TQ)

    def key_rows_of(tile):
        return pl.ds(pl.multiple_of(_na_key_row0(tile, seq // GRID_W) * GRID_W, NATTN_TQ), nk)

    def cols_of(g):
        return slice(g * HEAD_DIM, (g + 1) * HEAD_DIM)

    def ones_col(n):
        return jnp.where(lax.broadcasted_iota(jnp.int32, (n, HEAD_DIM), 1) == 0, 1.0, 0.0).astype(BF16)

    for g in range(NATTN_HG):
        vs_ref[g] = jnp.concatenate([v_ref[g], ones_col(seq)], axis=1)
        vcs_ref[g] = jnp.concatenate([vc_ref[g], ones_col(c)], axis=1)

    def scores(tile, _, slot):
        var = jnp.where(tile == 0, 0, jnp.where(tile == n_tile - 1, 2, 1))
        for g in range(NATTN_HG):
            q = q_ref[g, rows_of(tile), :]
            s_ref[slot, g] = _qk(q, k_ref[g, key_rows_of(tile), :]) + bias_ref[g, var]
            sc_ref[slot, g] = _qk(q, kc_ref[g])

    def probs(_, slot):
        for g in range(NATTN_HG):
            s, sc = s_ref[slot, g], sc_ref[slot, g]
            folded = jnp.concatenate([jnp.maximum(s[:, :c], sc), s[:, c:]], axis=1)
            m = jnp.max(folded, axis=-1, keepdims=True)
            p_ref[slot, g] = jnp.exp2((s - m) * EXP2_SCALE).astype(BF16)
            pc_ref[slot, g] = jnp.exp2((sc - m) * EXP2_SCALE).astype(BF16)

    def weighted_values(tile, _, slot):
        for g in range(NATTN_HG):
            ol = (jnp.dot(p_ref[slot, g], vs_ref[g, key_rows_of(tile), :], preferred_element_type=F32)
                  + jnp.dot(pc_ref[slot, g], vcs_ref[g], preferred_element_type=F32))
            o = ol[:, :HEAD_DIM] / ol[:, HEAD_DIM:HEAD_DIM + 1]
            o_ref[rows_of(tile), cols_of(g)] = o.astype(o_ref.dtype)

    p_ref[1] = jnp.ones(p_ref.shape[1:], BF16)
    pc_ref[1] = jnp.ones(pc_ref.shape[1:], BF16)
    _run_chains(n_tile, 1, scores, probs, weighted_values, unroll=NATTN_UNROLL)


def _na_bias_kernel(rpb_ref, o_ref, u_ref, *, rows):
    base = pl.program_id(0) * (N_RI * N_CI)
    kh = min(NA_KH, rows)

    def fill(sg, carry):
        q0 = pl.multiple_of(sg * SUBLANES, SUBLANES)
        qc = q0 + lax.broadcasted_iota(jnp.int32, (SUBLANES, LANES), 0)
        kc = lax.broadcasted_iota(jnp.int32, (SUBLANES, LANES), 1) & (GRID_W - 1)
        diag = kc - qc + (NA_KW - 1)
        us = [jnp.zeros((SUBLANES, LANES), F32) for _ in range(N_RI)]
        for ci in range(N_CI):
            hit = diag == ci
            for a in range(N_RI):
                us[a] = jnp.where(hit, rpb_ref[base + a * N_CI + ci], us[a])
        for a in range(N_RI):
            u_ref[a, pl.ds(q0, SUBLANES), :] = us[a] * INV_ATTN_SCALE
        return carry

    lax.fori_loop(0, GRID_W // SUBLANES, fill, 0)

    qc = lax.broadcasted_iota(jnp.int32, (GRID_W, LANES), 0)
    lane = lax.broadcasted_iota(jnp.int32, (GRID_W, LANES), 1)
    kc = lane & (GRID_W - 1)
    cs = jnp.clip(qc - NA_KW // 2, 0, GRID_W - NA_KW)
    col_valid = (kc >= cs) & (kc < cs + NA_KW)
    low = lane < GRID_W
    masks = {(True, True): col_valid, (True, False): col_valid & low,
             (False, True): col_valid & jnp.logical_not(low)}
    neg_blk = jnp.full((GRID_W, LANES), NEG, F32)
    n_tile = rows // NATTN_ROWS
    for var, tile in enumerate((0, 1, n_tile - 1)):
        r0 = tile * NATTN_ROWS
        k0 = _na_key_row0(tile, rows)
        for qr in range(NATTN_ROWS):
            rs = min(max(r0 + qr - kh // 2, 0), rows - kh)
            for jj in range(NATTN_KROWS // 2):
                kr = k0 + 2 * jj
                ok = (rs <= kr < rs + kh, rs <= kr + 1 < rs + kh)
                if ok == (False, False):
                    blk = neg_blk
                else:
                    a = kr - (r0 + qr) + NA_KH - 1
                    a_lo = min(max(a, 0), N_RI - 1)
                    a_hi = min(max(a + 1, 0), N_RI - 1)
                    pair = jnp.where(low, u_ref[a_lo], u_ref[a_hi])
                    blk = jnp.where(masks[ok], pair, NEG)
                o_ref[var, qr * GRID_W:(qr + 1) * GRID_W, jj * LANES:(jj + 1) * LANES] = blk


def _na_bias_tables(rpb, rows):
    h = rpb.shape[0]
    assert 2 * GRID_W == LANES and rows >= 3 * NATTN_ROWS and NATTN_ROWS == NA_KH // 2
    return pl.pallas_call(
        functools.partial(_na_bias_kernel, rows=rows),
        out_shape=jax.ShapeDtypeStruct((h, 3, NATTN_TQ, NATTN_KROWS * GRID_W), F32),
        grid=(h,),
        in_specs=[pl.BlockSpec(memory_space=pltpu.SMEM)],
        out_specs=pl.BlockSpec((None, 3, NATTN_TQ, NATTN_KROWS * GRID_W), lambda hh: (hh, 0, 0, 0)),
        scratch_shapes=[pltpu.VMEM((N_RI, GRID_W, LANES), F32)],
        compiler_params=_params("arbitrary"),
        name="na_bias",
    )(rpb.reshape(-1))


def _nattn(qkv, kv_c, bias):
    b, _, s, _ = qkv.shape
    c = kv_c.shape[2]
    h = C_HEADS
    ng = h // NATTN_HG
    gw = NATTN_HG * HEAD_DIM
    nk = NATTN_KROWS * GRID_W
    assert s % NATTN_TQ == 0

    def head_map(first_group):
        return lambda hg, bi: (bi, first_group + hg, 0, 0)

    return pl.pallas_call(
        _nattn_kernel,
        out_shape=jax.ShapeDtypeStruct((b, s, h * HEAD_DIM), BF16),
        grid=(ng, b),
        in_specs=[
            pl.BlockSpec((None, NATTN_HG, s, HEAD_DIM), head_map(0)),
            pl.BlockSpec((None, NATTN_HG, s, HEAD_DIM), head_map(ng)),
            pl.BlockSpec((None, NATTN_HG, s, HEAD_DIM), head_map(2 * ng)),
            pl.BlockSpec((None, NATTN_HG, c, HEAD_DIM), head_map(0)),
            pl.BlockSpec((None, NATTN_HG, c, HEAD_DIM), head_map(ng)),
            pl.BlockSpec((NATTN_HG, 3, NATTN_TQ, nk), lambda hg, bi: (hg, 0, 0, 0),
                         pipeline_mode=pl.Buffered(1)),
        ],
        out_specs=pl.BlockSpec((None, s, gw), lambda hg, bi: (bi, 0, hg)),
        scratch_shapes=[
            pltpu.VMEM((NATTN_HG, s, 2 * HEAD_DIM), BF16),
            pltpu.VMEM((NATTN_HG, c, 2 * HEAD_DIM), BF16),
            pltpu.VMEM((2, NATTN_HG, NATTN_TQ, nk), F32),
            pltpu.VMEM((2, NATTN_HG, NATTN_TQ, c), F32),
            pltpu.VMEM((2, NATTN_HG, NATTN_TQ, nk), BF16),
            pltpu.VMEM((2, NATTN_HG, NATTN_TQ, c), BF16),
        ],
        compiler_params=_params("arbitrary", "arbitrary"),
        name="nattn",
    )(qkv, qkv, qkv, kv_c, kv_c, bias)


OUTPROJ_ROWS = 256


def _resid_mlp_kernel(*refs, widths, final_norm, n_cast):
    x_ref, wo_ref, g1_ref, nw_ref, sh_ref, sc_ref, g2_ref, w1_ref, w2_ref = refs[:9]
    rest = refs[9:]
    o_refs = rest[:len(widths)]
    rest = rest[len(widths):]
    if final_norm:
        fw_ref = rest[0]
        rest = rest[1:]
    cast_in = rest[:n_cast]
    y_ref = rest[n_cast]
    cast_out = rest[n_cast + 1:2 * n_cast + 1]
    x1_ref, h_ref = rest[2 * n_cast + 1:]
    f = pl.program_id(2)
    tm = x_ref.shape[0]

    for src_ref, dst_ref in zip(cast_in, cast_out):
        dst_ref[...] = src_ref[...].astype(dst_ref.dtype)

    last = pl.num_programs(2) - 1

    def mlp_partial(rows=slice(None)):
        a = jnp.dot(h_ref[rows, :], w1_ref[...], preferred_element_type=F32)
        a = jnp.square(jnp.maximum(a, 0.0)).astype(BF16)
        return jnp.dot(a, w2_ref[...], preferred_element_type=F32)

    @pl.when(f == 0)
    def _():
        for r0 in range(0, tm, OUTPROJ_ROWS):
            rows = slice(r0, min(r0 + OUTPROJ_ROWS, tm))
            y = None
            k0 = 0
            for o_ref, kw in zip(o_refs, widths):
                part = jnp.dot(o_ref[rows, :], wo_ref[k0:k0 + kw, :], preferred_element_type=F32)
                y = part if y is None else y + part
                k0 += kw
            x1 = x_ref[rows, :] + g1_ref[...] * y
            x1_ref[rows, :] = x1
            h_ref[rows, :] = _norm_modulate(x1, nw_ref[...], sh_ref[...], sc_ref[...]).astype(BF16)
        y_ref[...] = mlp_partial()

    @pl.when((f > 0) & (f < last))
    def _():
        y_ref[...] += mlp_partial()

    @pl.when(f == last)
    def _():
        for r0 in range(0, tm, OUTPROJ_ROWS):
            rows = slice(r0, min(r0 + OUTPROJ_ROWS, tm))
            y = x1_ref[rows, :] + g2_ref[...] * (y_ref[rows, :] + mlp_partial(rows))
            if final_norm:
                y = y * lax.rsqrt(jnp.mean(y * y, axis=-1, keepdims=True) + NORM_EPS) * fw_ref[...]
            y_ref[rows, :] = y


def _resid_mlp(x, o_parts, w_out, g1, nw, shift, scale, g2, w1, w2, *, tm, tf, final_w=None,
               cast_jobs=()):
    b, s, d = x.shape
    dff = w1.shape[1]
    widths = tuple(o.shape[2] for o in o_parts)
    assert sum(widths) == w_out.shape[0] and dff // tf >= 2
    n_i, n_f = s // tm, dff // tf
    steps = b * n_i * n_f
    vec = pl.BlockSpec((None, 1, d), lambda bi, i, f: (bi, 0, 0))
    in_specs = [
        pl.BlockSpec((None, tm, d), lambda bi, i, f: (bi, i, 0)),
        pl.BlockSpec(w_out.shape, lambda bi, i, f: (0, 0), pipeline_mode=pl.Buffered(1)),
        vec,
        pl.BlockSpec((1, d), lambda bi, i, f: (0, 0)),
        vec, vec, vec,
        pl.BlockSpec((d, tf), lambda bi, i, f: (0, f)),
        pl.BlockSpec((tf, d), lambda bi, i, f: (f, 0)),
    ] + [pl.BlockSpec((None, tm, kw), lambda bi, i, f: (bi, i, 0)) for kw in widths]
    args = [x, w_out, g1, nw.reshape(1, d), shift, scale, g2, w1, w2, *o_parts]
    if final_w is not None:
        in_specs.append(pl.BlockSpec((1, d), lambda bi, i, f: (0, 0)))
        args.append(final_w.reshape(1, d))
    out_shapes = [jax.ShapeDtypeStruct(x.shape, x.dtype)]
    out_specs = [pl.BlockSpec((None, tm, d), lambda bi, i, f: (bi, i, 0))]
    for w_stack, idx in cast_jobs:
        _, r, c = w_stack.shape
        tr = r // steps
        assert tr * steps == r and tr % 16 == 0
        in_specs.append(pl.BlockSpec((None, tr, c),
                                     lambda bi, i, f, idx=idx: (idx, (bi * n_i + i) * n_f + f, 0)))
        args.append(w_stack)
        out_shapes.append(jax.ShapeDtypeStruct((r, c), BF16))
        out_specs.append(pl.BlockSpec((tr, c), lambda bi, i, f: ((bi * n_i + i) * n_f + f, 0)))
    outs = pl.pallas_call(
        functools.partial(_resid_mlp_kernel, widths=widths, final_norm=final_w is not None,
                          n_cast=len(cast_jobs)),
        out_shape=out_shapes,
        grid=(b, n_i, n_f),
        in_specs=in_specs,
        out_specs=out_specs,
        scratch_shapes=[pltpu.VMEM((tm, d), F32), pltpu.VMEM((tm, d), BF16)],
        compiler_params=_params("arbitrary", "arbitrary", "arbitrary"),
        name="resid_mlp",
    )(*args)
    return outs if cast_jobs else outs[0]


EVEN_SECTIONS = (
    (0, A_Q_HEADS, "q_norm"),
    (A_Q_HEADS * HEAD_DIM, A_KV_HEADS, "k_norm"),
    ((A_Q_HEADS + A_KV_HEADS) * HEAD_DIM, A_KV_HEADS, "v"),
    ((A_Q_HEADS + 2 * A_KV_HEADS) * HEAD_DIM, B_Q_HEADS, "q"),
    ((A_Q_HEADS + 2 * A_KV_HEADS + B_Q_HEADS) * HEAD_DIM, B_KV_HEADS, "k"),
    ((A_Q_HEADS + 2 * A_KV_HEADS + B_Q_HEADS + B_KV_HEADS) * HEAD_DIM, B_KV_HEADS, "v"),
)
LAT_TM = 512
MLP_TF = 1024


def kernel(x, c, ctx, c_ctx, ada_w, ada_b, norm_w, mlp_w1, mlp_w2, ev_w_in, ev_w_out, ev_q_norm,
           ev_k_norm, ev_sink, od_w_in, od_w_out, od_rpb, final_norm_w):
    b, s, d = x.shape
    n_ctx = ctx.shape[1]
    depth = ada_w.shape[0]
    rows = s // GRID_W

    mod = _ada_mod(jnp.concatenate([c, c_ctx[None, :]], axis=0), ada_w, ada_b)
    rope_tabs = _rope_tables(s)

    def mod_vecs(layer):
        lat = [mod[layer, :b, k * d:(k + 1) * d][:, None, :] for k in range(6)]
        cx = [mod[layer, b:b + 1, k * d:(k + 1) * d][:, None, :] for k in range(6)]
        return lat, cx

    ctx = ctx.reshape(1, b * n_ctx, d)

    def weight_sources(layer):
        w_in, w_out = (ev_w_in, ev_w_out) if layer % 2 == 0 else (od_w_in, od_w_out)
        return ((mlp_w1, layer), (mlp_w2, layer), (w_in, layer // 2), (w_out, layer // 2))

    converted = {}

    def layer_weights(layer):
        if layer in converted:
            return converted[layer]
        return tuple(_to_bf16(w, idx) for w, idx in weight_sources(layer))

    for layer in range(depth):
        need_ctx = layer < depth - 1
        (sh1, sc1, g1, sh2, sc2, g2), (sh1c, sc1c, g1c, sh2c, sc2c, g2c) = mod_vecs(layer)
        j = layer // 2
        w1, w2, w_in, w_out = layer_weights(layer)
        if layer % 2 == 0:
            n_in = w_in.shape[1]
            norms = (ev_q_norm[j], ev_k_norm[j])
            qkv = _proj(x, norm_w[layer, 0], sh1, sc1, w_in, tm=LAT_TM, tn=n_in,
                        sections=EVEN_SECTIONS, head_norms=norms, rope_tabs=rope_tabs, name="proj_even")
            qkv_c = _proj(ctx, norm_w[layer, 0], sh1c, sc1c, w_in, tm=b * n_ctx, tn=n_in,
                          sections=EVEN_SECTIONS, head_norms=norms, name="proj_even_ctx")
            qkv_c = qkv_c.reshape(b, n_ctx, n_in)
            oa = _gattn(qkv, qkv_c, q_blk=0, k_blk=A_Q_HEADS, v_blk=A_Q_HEADS + A_KV_HEADS)
            sink_kg = ev_sink[j].reshape(B_KV_HEADS, GQA_GROUP).astype(F32)
            qb_blk = (A_Q_HEADS + 2 * A_KV_HEADS) // GQA_GROUP
            kb_blk = A_Q_HEADS + 2 * A_KV_HEADS + B_Q_HEADS
            ob = _wattn(sink_kg, qkv, qkv_c, q_blk=qb_blk, k_blk=kb_blk, v_blk=kb_blk + B_KV_HEADS)
            o_parts = [oa, ob]
            if need_ctx:
                sink_all = jnp.concatenate([jnp.full((A_KV_HEADS, GQA_GROUP), NEG, F32), sink_kg], axis=0)
                oc_parts = [_cattn(sink_all, qkv_c).reshape(1, b * n_ctx, N_HEADS * HEAD_DIM)]
        else:
            hd = C_HEADS * HEAD_DIM
            qkv = _proj(x, norm_w[layer, 0], sh1, sc1, w_in, tm=LAT_TM, tn=hd, head_major=True,
                        name="proj_odd")
            per_batch = lambda v: jnp.broadcast_to(v, (b, 1, d))
            kv_c = _proj(ctx.reshape(b, n_ctx, d), norm_w[layer, 0], per_batch(sh1c), per_batch(sc1c),
                         w_in, tm=n_ctx, tn=hd, n_off=1, n_out=2 * hd, head_major=True,
                         name="proj_odd_ctx")
            o_parts = [_nattn(qkv, kv_c, _na_bias_tables(od_rpb[j], rows))]
            assert not need_ctx, "the odd layer is the last one at this depth"
        jobs = weight_sources(layer + 1) if layer + 1 < depth else ()
        outs = _resid_mlp(x, o_parts, w_out, g1, norm_w[layer, 1], sh2, sc2, g2, w1, w2, tm=LAT_TM,
                          tf=MLP_TF, final_w=None if need_ctx else final_norm_w, cast_jobs=jobs)
        if jobs:
            x, converted[layer + 1] = outs[0], tuple(outs[1:])
        else:
            x = outs
        if need_ctx:
            ctx = _resid_mlp(ctx, oc_parts, w_out, g1c, norm_w[layer, 1], sh2c, sc2c, g2c, w1, w2,
                             tm=b * n_ctx, tf=MLP_TF)
    return x
```

```python
import functools

import jax
import jax.numpy as jnp
import numpy as np
from jax import lax
from jax.experimental import pallas as pl
from jax.experimental.pallas import tpu as pltpu

D_MODEL = 2048
GRID_W = 64
HEAD_DIM = 128
N_HEADS = D_MODEL // HEAD_DIM
A_Q_HEADS = N_HEADS // 2
A_KV_HEADS = max(1, A_Q_HEADS // 4)
B_Q_HEADS = N_HEADS - A_Q_HEADS
B_KV_HEADS = max(1, B_Q_HEADS // 4)
GQA_GROUP = A_Q_HEADS // A_KV_HEADS
C_HEADS = N_HEADS
WINDOW = 128
NA_KH = 8
NA_KW = 16
D_FF = 4 * D_MODEL
ROPE_THETA = 10000.0
ROPE_PAIRS = HEAD_DIM // 4
NORM_EPS = 1e-6
NEG = -1e30
ATTN_SCALE = HEAD_DIM ** -0.5

V7X_VMEM_BYTES = 64 * 1024 * 1024
VMEM_LIMIT = V7X_VMEM_BYTES - 6 * 1024 * 1024
LANES = 128
SUBLANES = 8

F32 = jnp.float32
BF16 = jnp.bfloat16


def _params(*sem):
    return pltpu.CompilerParams(dimension_semantics=sem, vmem_limit_bytes=VMEM_LIMIT)


ADA_ROWS = 3
ADA_TN = 1024


def _ada_kernel(cb_ref, w_ref, b_ref, o_ref, s_ref):
    @pl.when((pl.program_id(0) == 0) & (pl.program_id(1) == 0))
    def _():
        cvals = cb_ref[...]
        s_ref[...] = cvals * (1.0 / (1.0 + jnp.exp(-cvals)))

    nchunk = ADA_TN // LANES

    def body(kk, accs):
        k0 = pl.multiple_of(kk * SUBLANES, SUBLANES)
        w = w_ref[pl.ds(k0, SUBLANES), :]
        out = []
        for r in range(ADA_ROWS):
            s = s_ref[r, pl.ds(k0, SUBLANES), :]
            for cch in range(nchunk):
                out.append(accs[r * nchunk + cch] + w[:, cch * LANES:(cch + 1) * LANES] * s)
        return tuple(out)

    zeros = tuple(jnp.zeros((SUBLANES, LANES), F32) for _ in range(ADA_ROWS * nchunk))
    accs = lax.fori_loop(0, D_MODEL // SUBLANES, body, zeros, unroll=4)
    o_ref[...] = jnp.broadcast_to(b_ref[...], o_ref.shape)
    for r in range(ADA_ROWS):
        row = jnp.concatenate(
            [jnp.sum(accs[r * nchunk + cch], axis=0, keepdims=True) for cch in range(nchunk)], axis=1)
        o_ref[r:r + 1, :] = row + b_ref[...]


def _ada_mod(cc, ada_w, ada_b):
    depth, d, n = ada_w.shape
    cb = jnp.broadcast_to(cc[:, :, None], (ADA_ROWS, d, LANES))
    return pl.pallas_call(
        _ada_kernel,
        out_shape=jax.ShapeDtypeStruct((depth, SUBLANES, n), F32),
        grid=(depth, n // ADA_TN),
        in_specs=[
            pl.BlockSpec((ADA_ROWS, d, LANES), lambda l, j: (0, 0, 0)),
            pl.BlockSpec((None, d, ADA_TN), lambda l, j: (l, 0, j)),
            pl.BlockSpec((None, 1, ADA_TN), lambda l, j: (l, 0, j)),
        ],
        out_specs=pl.BlockSpec((None, SUBLANES, ADA_TN), lambda l, j: (l, 0, j)),
        scratch_shapes=[pltpu.VMEM((ADA_ROWS, d, LANES), F32)],
        compiler_params=_params("arbitrary", "arbitrary"),
        name="ada_mod",
    )(cb, ada_w, ada_b.reshape(depth, 1, n))


CAST_BLOCK_BYTES = 8 * 1024 * 1024


def _cast_kernel(w_ref, o_ref):
    o_ref[...] = w_ref[...].astype(o_ref.dtype)


def _to_bf16(w_stack, idx):
    _, r, c = w_stack.shape
    tr = r
    while tr * c * w_stack.dtype.itemsize > CAST_BLOCK_BYTES and tr % 32 == 0:
        tr //= 2
    return pl.pallas_call(
        _cast_kernel,
        out_shape=jax.ShapeDtypeStruct((r, c), BF16),
        grid=(r // tr,),
        in_specs=[pl.BlockSpec((None, tr, c), lambda i: (idx, i, 0))],
        out_specs=pl.BlockSpec((tr, c), lambda i: (i, 0)),
        compiler_params=_params("arbitrary"),
        name="to_bf16",
    )(w_stack)


def _norm_modulate(x, nw, shift, scale):
    ms = jnp.mean(x * x, axis=-1, keepdims=True)
    y = x * lax.rsqrt(ms + NORM_EPS) * nw
    return y * (1.0 + scale) + shift


def _rope(y, cos, sin_lo, sin_hi):
    return (y * cos + pltpu.roll(y, HEAD_DIM - ROPE_PAIRS, 1) * sin_lo
            + pltpu.roll(y, ROPE_PAIRS, 1) * sin_hi)


def _rope_tables(seq):
    t = jnp.arange(seq)
    row = (t // GRID_W).astype(F32)
    col = (t % GRID_W).astype(F32)
    inv = ROPE_THETA ** (-jnp.arange(ROPE_PAIRS, dtype=F32) / ROPE_PAIRS)
    ang_r = row[:, None] * inv
    ang_c = col[:, None] * inv
    ang = jnp.concatenate([ang_r, ang_r, ang_c, ang_c], axis=-1)
    cos, sin = jnp.cos(ang), jnp.sin(ang)
    first = (jnp.arange(HEAD_DIM) % (2 * ROPE_PAIRS)) < ROPE_PAIRS
    sin_lo = jnp.where(first, -sin, 0.0)
    sin_hi = jnp.where(first, 0.0, sin)
    return cos, sin_lo, sin_hi


PROJ_ROWS = 256


def _proj_kernel(*refs, sections, rope, head_major, n_cast):
    x_ref, nw_ref, sh_ref, sc_ref, w_ref = refs[:5]
    rest = refs[5:]
    if sections is not None:
        qn_ref, kn_ref = rest[:2]
        rest = rest[2:]
    if rope:
        cos_ref, slo_ref, shi_ref = rest[:3]
        rest = rest[3:]
    cast_in = rest[:n_cast]
    o_ref = rest[n_cast]
    cast_out = rest[n_cast + 1:2 * n_cast + 1]
    h_ref = rest[2 * n_cast + 1] if sections is None else None
    tm = x_ref.shape[0]

    for src_ref, dst_ref in zip(cast_in, cast_out):
        dst_ref[...] = src_ref[...].astype(dst_ref.dtype)

    def project(h, rows):
        if sections is None:
            y = jnp.dot(h, w_ref[...], preferred_element_type=F32).astype(o_ref.dtype)
            if head_major:
                for hh in range(y.shape[1] // HEAD_DIM):
                    o_ref[hh, rows, :] = y[:, hh * HEAD_DIM:(hh + 1) * HEAD_DIM]
            else:
                o_ref[rows, :] = y
            return
        for c0, nh, kind in sections:
            y = jnp.dot(h, w_ref[:, c0:c0 + nh * HEAD_DIM], preferred_element_type=F32)
            for hh in range(nh):
                yh = y[:, hh * HEAD_DIM:(hh + 1) * HEAD_DIM]
                if kind in ("q_norm", "k_norm"):
                    hw = (qn_ref if kind == "q_norm" else kn_ref)[...]
                    yh = yh * lax.rsqrt(jnp.mean(yh * yh, axis=-1, keepdims=True) + NORM_EPS) * hw
                if rope and kind != "v":
                    yh = _rope(yh, cos_ref[rows, :], slo_ref[rows, :], shi_ref[rows, :])
                lo = c0 + hh * HEAD_DIM
                o_ref[rows, lo:lo + HEAD_DIM] = yh.astype(o_ref.dtype)

    @pl.when(pl.program_id(2) == 0)
    def _():
        for r0 in range(0, tm, PROJ_ROWS):
            rows = slice(r0, min(r0 + PROJ_ROWS, tm))
            h = _norm_modulate(x_ref[rows, :], nw_ref[...], sh_ref[...], sc_ref[...]).astype(BF16)
            if h_ref is not None:
                h_ref[rows, :] = h
            project(h, rows)

    if h_ref is not None:
        @pl.when(pl.program_id(2) > 0)
        def _():
            project(h_ref[...], slice(0, tm))


def _proj(x, nw, shift, scale, w, *, tm, tn, n_off=0, n_out=None, sections=None,
          head_norms=None, rope_tabs=None, head_major=False, cast_jobs=(), name="proj"):
    b, s, d = x.shape
    n_out = w.shape[1] if n_out is None else n_out
    nj = n_out // tn
    n_i = s // tm
    steps = b * n_i * nj
    assert not (head_major and sections is not None)
    if head_major:
        hpt = tn // HEAD_DIM
        out_shape = jax.ShapeDtypeStruct((b, n_out // HEAD_DIM, s, HEAD_DIM), BF16)
        out_spec = pl.BlockSpec((None, hpt, tm, HEAD_DIM), lambda bi, i, j: (bi, j, i, 0))
    else:
        out_shape = jax.ShapeDtypeStruct((b, s, n_out), BF16)
        out_spec = pl.BlockSpec((None, tm, tn), lambda bi, i, j: (bi, i, j))
    in_specs = [
        pl.BlockSpec((None, tm, d), lambda bi, i, j: (bi, i, 0)),
        pl.BlockSpec((1, d), lambda bi, i, j: (0, 0)),
        pl.BlockSpec((None, 1, d), lambda bi, i, j: (bi, 0, 0)),
        pl.BlockSpec((None, 1, d), lambda bi, i, j: (bi, 0, 0)),
        pl.BlockSpec((d, tn), lambda bi, i, j: (0, j + n_off),
                     pipeline_mode=pl.Buffered(1) if nj == 1 else None),
    ]
    args = [x, nw.reshape(1, d), shift, scale, w]
    if sections is not None:
        assert nj == 1
        in_specs += [pl.BlockSpec((1, HEAD_DIM), lambda bi, i, j: (0, 0))] * 2
        args += [head_norms[0].reshape(1, HEAD_DIM), head_norms[1].reshape(1, HEAD_DIM)]
    if rope_tabs is not None:
        in_specs += [pl.BlockSpec((tm, HEAD_DIM), lambda bi, i, j: (i, 0))] * 3
        args += list(rope_tabs)
    out_shapes, out_specs = [out_shape], [out_spec]
    for w_stack, idx in cast_jobs:
        _, r, c = w_stack.shape
        tr = r // steps
        assert tr * steps == r and tr % 16 == 0
        in_specs.append(pl.BlockSpec((None, tr, c),
                                     lambda bi, i, j, idx=idx: (idx, (bi * n_i + i) * nj + j, 0)))
        args.append(w_stack)
        out_shapes.append(jax.ShapeDtypeStruct((r, c), BF16))
        out_specs.append(pl.BlockSpec((tr, c), lambda bi, i, j: ((bi * n_i + i) * nj + j, 0)))
    outs = pl.pallas_call(
        functools.partial(_proj_kernel, sections=sections, rope=rope_tabs is not None,
                          head_major=head_major, n_cast=len(cast_jobs)),
        out_shape=out_shapes,
        grid=(b, n_i, nj),
        in_specs=in_specs,
        out_specs=out_specs,
        scratch_shapes=[pltpu.VMEM((tm, d), BF16)] if sections is None else [],
        compiler_params=_params("arbitrary", "arbitrary", "arbitrary"),
        name=name,
    )(*args)
    return outs if cast_jobs else outs[0]


def _qk(q, k):
    return lax.dot_general(q, k, (((1,), (1,)), ((), ())), preferred_element_type=F32)


EXP2_SCALE = ATTN_SCALE * float(np.log2(np.e))
INV_ATTN_SCALE = 1.0 / ATTN_SCALE


def _softmax_pv(parts, sink=None):
    m = functools.reduce(jnp.maximum, [jnp.max(s, axis=-1, keepdims=True) for s, _ in parts])
    if sink is not None:
        m = jnp.maximum(m, sink)
    l = None
    o = None
    for s, v in parts:
        p = jnp.exp2((s - m) * EXP2_SCALE)
        ls = jnp.sum(p, axis=-1, keepdims=True)
        os_ = jnp.dot(p.astype(BF16), v, preferred_element_type=F32)
        l = ls if l is None else l + ls
        o = os_ if o is None else o + os_
    if sink is not None:
        l = l + jnp.exp2((sink - m) * EXP2_SCALE)
    return o / l


def _run_chains(n_blk, n_inner, scores, probs, weighted_values, unroll=1):
    per_body = unroll * n_inner
    assert per_body % 2 == 0 and n_blk % unroll == 0

    def chain(it, n):
        return it * unroll + n // n_inner, n % n_inner

    scores(0, 0, 0)

    def body(it, carry):
        for n in range(per_body):
            blk, g = chain(it, n + 1)
            scores(jnp.minimum(blk, n_blk - 1) if n + 1 == per_body else blk, g, (n + 1) % 2)
            probs(n % n_inner, n % 2)
            blk, g = chain(it, n - 1)
            weighted_values(jnp.maximum(blk, 0) if n == 0 else blk, g, (n - 1) % 2)
        return carry

    lax.fori_loop(0, n_blk // unroll, body, 0)
    weighted_values(n_blk - 1, n_inner - 1, 1)


GATTN_ROWS = 512


def _gattn_kernel(q_ref, k_ref, v_ref, kc_ref, vc_ref, o_ref, kall_ref, vall_ref, s_ref, p_ref):
    c = kc_ref.shape[0]

    n_blk = q_ref.shape[0] // GATTN_ROWS

    kall_ref[:c, :] = kc_ref[...]
    kall_ref[c:, :] = k_ref[...]
    vall_ref[:c, :HEAD_DIM] = vc_ref[...]
    vall_ref[c:, :HEAD_DIM] = v_ref[...]
    lane = lax.broadcasted_iota(jnp.int32, (vall_ref.shape[0], HEAD_DIM), 1)
    vall_ref[:, HEAD_DIM:] = jnp.where(lane == 0, 1.0, 0.0).astype(BF16)

    def rows_of(blk):
        return pl.ds(pl.multiple_of(blk * GATTN_ROWS, GATTN_ROWS), GATTN_ROWS)

    def cols_of(g):
        return slice(g * HEAD_DIM, (g + 1) * HEAD_DIM)

    def scores(blk, g, slot):
        s_ref[slot] = _qk(q_ref[rows_of(blk), cols_of(g)], kall_ref[...])

    def probs(g, slot):
        m = jnp.max(s_ref[slot], axis=-1, keepdims=True)
        p_ref[slot] = jnp.exp2((s_ref[slot] - m) * EXP2_SCALE).astype(BF16)

    def weighted_values(blk, g, slot):
        ol = jnp.dot(p_ref[slot], vall_ref[...], preferred_element_type=F32)
        o = ol[:, :HEAD_DIM] / ol[:, HEAD_DIM:HEAD_DIM + 1]
        o_ref[rows_of(blk), cols_of(g)] = o.astype(o_ref.dtype)

    p_ref[1] = jnp.ones(p_ref.shape[1:], BF16)
    _run_chains(n_blk, GQA_GROUP, scores, probs, weighted_values)


def _gattn(qkv, qkv_c, *, q_blk, k_blk, v_blk):
    b, s, _ = qkv.shape
    c = qkv_c.shape[1]
    gw = GQA_GROUP * HEAD_DIM
    assert GQA_GROUP % 2 == 0 and s % GATTN_ROWS == 0
    single = pl.Buffered(1)
    return pl.pallas_call(
        _gattn_kernel,
        out_shape=jax.ShapeDtypeStruct((b, s, A_Q_HEADS * HEAD_DIM), BF16),
        grid=(b, A_KV_HEADS),
        in_specs=[
            pl.BlockSpec((None, s, gw), lambda bi, kv: (bi, 0, q_blk + kv), pipeline_mode=single),
            pl.BlockSpec((None, s, HEAD_DIM), lambda bi, kv: (bi, 0, k_blk + kv), pipeline_mode=single),
            pl.BlockSpec((None, s, HEAD_DIM), lambda bi, kv: (bi, 0, v_blk + kv), pipeline_mode=single),
            pl.BlockSpec((None, c, HEAD_DIM), lambda bi, kv: (bi, 0, k_blk + kv)),
            pl.BlockSpec((None, c, HEAD_DIM), lambda bi, kv: (bi, 0, v_blk + kv)),
        ],
        out_specs=pl.BlockSpec((None, s, gw), lambda bi, kv: (bi, 0, kv)),
        scratch_shapes=[
            pltpu.VMEM((c + s, HEAD_DIM), BF16),
            pltpu.VMEM((c + s, 2 * HEAD_DIM), BF16),
            pltpu.VMEM((2, GATTN_ROWS, c + s), F32),
            pltpu.VMEM((2, GATTN_ROWS, c + s), BF16),
        ],
        compiler_params=_params("arbitrary", "arbitrary"),
        name="gattn",
    )(qkv, qkv, qkv, qkv_c, qkv_c)


WATTN_TQ = 4 * WINDOW
WATTN_NK = WATTN_TQ + 2 * WINDOW


def _wattn_kernel(sink_ref, q_ref, k_ref, v_ref, kc_ref, vc_ref, o_ref,
                  mask_ref, s_ref, sc_ref, p_ref, pc_ref, l_ref):
    kv = pl.program_id(1)
    seq = q_ref.shape[0]
    n_tile = seq // WATTN_TQ

    def rows_of(tile):
        return pl.ds(pl.multiple_of(tile * WATTN_TQ, WATTN_TQ), WATTN_TQ)

    def key0_of(tile):
        lo, hi = 0, seq - WATTN_NK
        k0 = tile * WATTN_TQ - WINDOW
        return min(max(k0, lo), hi) if isinstance(tile, int) else jnp.clip(k0, lo, hi)

    def key_rows_of(tile):
        return pl.ds(pl.multiple_of(key0_of(tile), WINDOW), WATTN_NK)

    def cols_of(g):
        return slice(g * HEAD_DIM, (g + 1) * HEAD_DIM)

    for var, tile in enumerate((0, 1, n_tile - 1)):
        shift = key0_of(tile) - tile * WATTN_TQ
        dist = (shift + lax.broadcasted_iota(jnp.int32, (WATTN_TQ, WATTN_NK), 1)
                - lax.broadcasted_iota(jnp.int32, (WATTN_TQ, WATTN_NK), 0))
        mask_ref[var] = jnp.where(jnp.abs(dist) <= WINDOW, 0.0, NEG)

    def scores(tile, g, slot):
        q = q_ref[rows_of(tile), cols_of(g)]
        var = jnp.where(tile == 0, 0, jnp.where(tile == n_tile - 1, 2, 1))
        s_ref[slot] = _qk(q, k_ref[key_rows_of(tile), :]) + mask_ref[var]
        sc_ref[slot] = _qk(q, kc_ref[...])

    def probs(g, slot):
        s, sc = s_ref[slot], sc_ref[slot]
        sink = sink_ref[kv, g] * INV_ATTN_SCALE
        m = jnp.maximum(jnp.max(s, axis=-1, keepdims=True), jnp.max(sc, axis=-1, keepdims=True))
        m = jnp.maximum(m, sink)
        p = jnp.exp2((s - m) * EXP2_SCALE)
        pc = jnp.exp2((sc - m) * EXP2_SCALE)
        l_ref[slot] = (jnp.sum(p, axis=-1, keepdims=True) + jnp.sum(pc, axis=-1, keepdims=True)
                       + jnp.exp2((sink - m) * EXP2_SCALE))
        p_ref[slot] = p.astype(BF16)
        pc_ref[slot] = pc.astype(BF16)

    def weighted_values(tile, g, slot):
        o = (jnp.dot(p_ref[slot], v_ref[key_rows_of(tile), :], preferred_element_type=F32)
             + jnp.dot(pc_ref[slot], vc_ref[...], preferred_element_type=F32))
        o_ref[rows_of(tile), cols_of(g)] = (o / l_ref[slot]).astype(o_ref.dtype)

    p_ref[1] = jnp.ones(p_ref.shape[1:], BF16)
    pc_ref[1] = jnp.ones(pc_ref.shape[1:], BF16)
    l_ref[1] = jnp.ones(l_ref.shape[1:], F32)
    _run_chains(seq // WATTN_TQ, GQA_GROUP, scores, probs, weighted_values)


def _wattn(sink_kg, qkv, qkv_c, *, q_blk, k_blk, v_blk):
    b, s, _ = qkv.shape
    c = qkv_c.shape[1]
    gw = GQA_GROUP * HEAD_DIM
    assert s % WATTN_TQ == 0 and s >= 3 * WATTN_TQ

    def col_map(col):
        return lambda bi, kv: (bi, 0, col + kv)

    return pl.pallas_call(
        _wattn_kernel,
        out_shape=jax.ShapeDtypeStruct((b, s, B_Q_HEADS * HEAD_DIM), BF16),
        grid=(b, B_KV_HEADS),
        in_specs=[
            pl.BlockSpec(memory_space=pltpu.SMEM),
            pl.BlockSpec((None, s, gw), col_map(q_blk)),
            pl.BlockSpec((None, s, HEAD_DIM), col_map(k_blk)),
            pl.BlockSpec((None, s, HEAD_DIM), col_map(v_blk)),
            pl.BlockSpec((None, c, HEAD_DIM), col_map(k_blk)),
            pl.BlockSpec((None, c, HEAD_DIM), col_map(v_blk)),
        ],
        out_specs=pl.BlockSpec((None, s, gw), col_map(0)),
        scratch_shapes=[
            pltpu.VMEM((3, WATTN_TQ, WATTN_NK), F32),
            pltpu.VMEM((2, WATTN_TQ, WATTN_NK), F32),
            pltpu.VMEM((2, WATTN_TQ, c), F32),
            pltpu.VMEM((2, WATTN_TQ, WATTN_NK), BF16),
            pltpu.VMEM((2, WATTN_TQ, c), BF16),
            pltpu.VMEM((2, WATTN_TQ, 1), F32),
        ],
        compiler_params=_params("arbitrary", "arbitrary"),
        name="wattn",
    )(sink_kg, qkv, qkv, qkv, qkv_c, qkv_c)


def _cattn_kernel(sink_ref, q_ref, k_ref, v_ref, o_ref):
    grp = pl.program_id(1)
    k, v = k_ref[...], v_ref[...]
    for g in range(GQA_GROUP):
        q = q_ref[:, g * HEAD_DIM:(g + 1) * HEAD_DIM]
        o = _softmax_pv([(_qk(q, k), v)], sink=sink_ref[grp, g] * INV_ATTN_SCALE)
        o_ref[:, g * HEAD_DIM:(g + 1) * HEAD_DIM] = o.astype(o_ref.dtype)


def _cattn(sink_all, qkv_c):
    b, c, _ = qkv_c.shape
    gw = GQA_GROUP * HEAD_DIM
    n_grp = A_KV_HEADS + B_KV_HEADS
    q_map = lambda bi, g: (bi, 0, g + g // A_KV_HEADS)
    k_map = lambda bi, g: (bi, 0, 8 + g + 10 * (g // A_KV_HEADS))
    v_map = lambda bi, g: (bi, 0, 10 + g + 10 * (g // A_KV_HEADS))
    return pl.pallas_call(
        _cattn_kernel,
        out_shape=jax.ShapeDtypeStruct((b, c, N_HEADS * HEAD_DIM), BF16),
        grid=(b, n_grp),
        in_specs=[
            pl.BlockSpec(memory_space=pltpu.SMEM),
            pl.BlockSpec((None, c, gw), q_map),
            pl.BlockSpec((None, c, HEAD_DIM), k_map),
            pl.BlockSpec((None, c, HEAD_DIM), v_map),
        ],
        out_specs=pl.BlockSpec((None, c, gw), lambda bi, g: (bi, 0, g)),
        compiler_params=_params("arbitrary", "arbitrary"),
        name="cattn",
    )(sink_all, qkv_c, qkv_c, qkv_c)


NATTN_ROWS = 4
NATTN_TQ = NATTN_ROWS * GRID_W
NATTN_KROWS = 3 * NATTN_ROWS


NATTN_HG = 2
NATTN_UNROLL = 2
N_RI = 2 * NA_KH - 1
N_CI = 2 * NA_KW - 1


def _na_key_row0(tile, rows):
    lo, hi = 0, rows - NATTN_KROWS
    r0 = (tile - 1) * NATTN_ROWS
    if isinstance(tile, int):
        return min(max(r0, lo), hi)
    return jnp.clip(r0, lo, hi)


def _nattn_kernel(q_ref, k_ref, v_ref, kc_ref, vc_ref, bias_ref, o_ref,
                  vs_ref, vcs_ref, s_ref, sc_ref, p_ref, pc_ref):
    seq = q_ref.shape[1]
    c = kc_ref.shape[1]
    n_tile = seq // NATTN_TQ
    nk = NATTN_KROWS * GRID_W

    def rows_of(tile):
        return pl.ds(pl.multiple_of(tile * NATTN_TQ, NATTN_TQ), NATTN_TQ)

    def key_rows_of(tile):
        return pl.ds(pl.multiple_of(_na_key_row0(tile, seq // GRID_W) * GRID_W, NATTN_TQ), nk)

    def cols_of(g):
        return slice(g * HEAD_DIM, (g + 1) * HEAD_DIM)

    def ones_col(n):
        return jnp.where(lax.broadcasted_iota(jnp.int32, (n, HEAD_DIM), 1) == 0, 1.0, 0.0).astype(BF16)

    for g in range(NATTN_HG):
        vs_ref[g] = jnp.concatenate([v_ref[g], ones_col(seq)], axis=1)
        vcs_ref[g] = jnp.concatenate([vc_ref[g], ones_col(c)], axis=1)

    def scores(tile, _, slot):
        var = jnp.where(tile == 0, 0, jnp.where(tile == n_tile - 1, 2, 1))
        for g in range(NATTN_HG):
            q = q_ref[g, rows_of(tile), :]
            s_ref[slot, g] = _qk(q, k_ref[g, key_rows_of(tile), :]) + bias_ref[g, var]
            sc_ref[slot, g] = _qk(q, kc_ref[g])

    def probs(_, slot):
        for g in range(NATTN_HG):
            s, sc = s_ref[slot, g], sc_ref[slot, g]
            folded = jnp.concatenate([jnp.maximum(s[:, :c], sc), s[:, c:]], axis=1)
            m = jnp.max(folded, axis=-1, keepdims=True)
            p_ref[slot, g] = jnp.exp2((s - m) * EXP2_SCALE).astype(BF16)
            pc_ref[slot, g] = jnp.exp2((sc - m) * EXP2_SCALE).astype(BF16)

    def weighted_values(tile, _, slot):
        for g in range(NATTN_HG):
            ol = (jnp.dot(p_ref[slot, g], vs_ref[g, key_rows_of(tile), :], preferred_element_type=F32)
                  + jnp.dot(pc_ref[slot, g], vcs_ref[g], preferred_element_type=F32))
            o = ol[:, :HEAD_DIM] / ol[:, HEAD_DIM:HEAD_DIM + 1]
            o_ref[rows_of(tile), cols_of(g)] = o.astype(o_ref.dtype)

    p_ref[1] = jnp.ones(p_ref.shape[1:], BF16)
    pc_ref[1] = jnp.ones(pc_ref.shape[1:], BF16)
    _run_chains(n_tile, 1, scores, probs, weighted_values, unroll=NATTN_UNROLL)


def _na_bias_kernel(rpb_ref, o_ref, u_ref, *, rows):
    base = pl.program_id(0) * (N_RI * N_CI)
    kh = min(NA_KH, rows)

    def fill(sg, carry):
        q0 = pl.multiple_of(sg * SUBLANES, SUBLANES)
        qc = q0 + lax.broadcasted_iota(jnp.int32, (SUBLANES, LANES), 0)
        kc = lax.broadcasted_iota(jnp.int32, (SUBLANES, LANES), 1) & (GRID_W - 1)
        diag = kc - qc + (NA_KW - 1)
        us = [jnp.zeros((SUBLANES, LANES), F32) for _ in range(N_RI)]
        for ci in range(N_CI):
            hit = diag == ci
            for a in range(N_RI):
                us[a] = jnp.where(hit, rpb_ref[base + a * N_CI + ci], us[a])
        for a in range(N_RI):
            u_ref[a, pl.ds(q0, SUBLANES), :] = us[a] * INV_ATTN_SCALE
        return carry

    lax.fori_loop(0, GRID_W // SUBLANES, fill, 0)

    qc = lax.broadcasted_iota(jnp.int32, (GRID_W, LANES), 0)
    lane = lax.broadcasted_iota(jnp.int32, (GRID_W, LANES), 1)
    kc = lane & (GRID_W - 1)
    cs = jnp.clip(qc - NA_KW // 2, 0, GRID_W - NA_KW)
    col_valid = (kc >= cs) & (kc < cs + NA_KW)
    low = lane < GRID_W
    masks = {(True, True): col_valid, (True, False): col_valid & low,
             (False, True): col_valid & jnp.logical_not(low)}
    neg_blk = jnp.full((GRID_W, LANES), NEG, F32)
    n_tile = rows // NATTN_ROWS
    for var, tile in enumerate((0, 1, n_tile - 1)):
        r0 = tile * NATTN_ROWS
        k0 = _na_key_row0(tile, rows)
        for qr in range(NATTN_ROWS):
            rs = min(max(r0 + qr - kh // 2, 0), rows - kh)
            for jj in range(NATTN_KROWS // 2):
                kr = k0 + 2 * jj
                ok = (rs <= kr < rs + kh, rs <= kr + 1 < rs + kh)
                if ok == (False, False):
                    blk = neg_blk
                else:
                    a = kr - (r0 + qr) + NA_KH - 1
                    a_lo = min(max(a, 0), N_RI - 1)
                    a_hi = min(max(a + 1, 0), N_RI - 1)
                    pair = jnp.where(low, u_ref[a_lo], u_ref[a_hi])
                    blk = jnp.where(masks[ok], pair, NEG)
                o_ref[var, qr * GRID_W:(qr + 1) * GRID_W, jj * LANES:(jj + 1) * LANES] = blk


def _na_bias_tables(rpb, rows):
    h = rpb.shape[0]
    assert 2 * GRID_W == LANES and rows >= 3 * NATTN_ROWS and NATTN_ROWS == NA_KH // 2
    return pl.pallas_call(
        functools.partial(_na_bias_kernel, rows=rows),
        out_shape=jax.ShapeDtypeStruct((h, 3, NATTN_TQ, NATTN_KROWS * GRID_W), F32),
        grid=(h,),
        in_specs=[pl.BlockSpec(memory_space=pltpu.SMEM)],
        out_specs=pl.BlockSpec((None, 3, NATTN_TQ, NATTN_KROWS * GRID_W), lambda hh: (hh, 0, 0, 0)),
        scratch_shapes=[pltpu.VMEM((N_RI, GRID_W, LANES), F32)],
        compiler_params=_params("arbitrary"),
        name="na_bias",
    )(rpb.reshape(-1))


def _nattn(qkv, kv_c, bias):
    b, _, s, _ = qkv.shape
    c = kv_c.shape[2]
    h = C_HEADS
    ng = h // NATTN_HG
    gw = NATTN_HG * HEAD_DIM
    nk = NATTN_KROWS * GRID_W
    assert s % NATTN_TQ == 0

    def head_map(first_group):
        return lambda hg, bi: (bi, first_group + hg, 0, 0)

    return pl.pallas_call(
        _nattn_kernel,
        out_shape=jax.ShapeDtypeStruct((b, s, h * HEAD_DIM), BF16),
        grid=(ng, b),
        in_specs=[
            pl.BlockSpec((None, NATTN_HG, s, HEAD_DIM), head_map(0)),
            pl.BlockSpec((None, NATTN_HG, s, HEAD_DIM), head_map(ng)),
            pl.BlockSpec((None, NATTN_HG, s, HEAD_DIM), head_map(2 * ng)),
            pl.BlockSpec((None, NATTN_HG, c, HEAD_DIM), head_map(0)),
            pl.BlockSpec((None, NATTN_HG, c, HEAD_DIM), head_map(ng)),
            pl.BlockSpec((NATTN_HG, 3, NATTN_TQ, nk), lambda hg, bi: (hg, 0, 0, 0),
                         pipeline_mode=pl.Buffered(1)),
        ],
        out_specs=pl.BlockSpec((None, s, gw), lambda hg, bi: (bi, 0, hg)),
        scratch_shapes=[
            pltpu.VMEM((NATTN_HG, s, 2 * HEAD_DIM), BF16),
            pltpu.VMEM((NATTN_HG, c, 2 * HEAD_DIM), BF16),
            pltpu.VMEM((2, NATTN_HG, NATTN_TQ, nk), F32),
            pltpu.VMEM((2, NATTN_HG, NATTN_TQ, c), F32),
            pltpu.VMEM((2, NATTN_HG, NATTN_TQ, nk), BF16),
            pltpu.VMEM((2, NATTN_HG, NATTN_TQ, c), BF16),
        ],
        compiler_params=_params("arbitrary", "arbitrary"),
        name="nattn",
    )(qkv, qkv, qkv, kv_c, kv_c, bias)


OUTPROJ_ROWS = 256


def _resid_mlp_kernel(*refs, widths, final_norm, n_cast):
    x_ref, wo_ref, g1_ref, nw_ref, sh_ref, sc_ref, g2_ref, w1_ref, w2_ref = refs[:9]
    rest = refs[9:]
    o_refs = rest[:len(widths)]
    rest = rest[len(widths):]
    if final_norm:
        fw_ref = rest[0]
        rest = rest[1:]
    cast_in = rest[:n_cast]
    y_ref = rest[n_cast]
    cast_out = rest[n_cast + 1:2 * n_cast + 1]
    x1_ref, h_ref = rest[2 * n_cast + 1:]
    f = pl.program_id(2)
    tm = x_ref.shape[0]

    for src_ref, dst_ref in zip(cast_in, cast_out):
        dst_ref[...] = src_ref[...].astype(dst_ref.dtype)

    last = pl.num_programs(2) - 1

    def mlp_partial(rows=slice(None)):
        a = jnp.dot(h_ref[rows, :], w1_ref[...], preferred_element_type=F32)
        a = jnp.square(jnp.maximum(a, 0.0)).astype(BF16)
        return jnp.dot(a, w2_ref[...], preferred_element_type=F32)

    @pl.when(f == 0)
    def _():
        for r0 in range(0, tm, OUTPROJ_ROWS):
            rows = slice(r0, min(r0 + OUTPROJ_ROWS, tm))
            y = None
            k0 = 0
            for o_ref, kw in zip(o_refs, widths):
                part = jnp.dot(o_ref[rows, :], wo_ref[k0:k0 + kw, :], preferred_element_type=F32)
                y = part if y is None else y + part
                k0 += kw
            x1 = x_ref[rows, :] + g1_ref[...] * y
            x1_ref[rows, :] = x1
            h_ref[rows, :] = _norm_modulate(x1, nw_ref[...], sh_ref[...], sc_ref[...]).astype(BF16)
        y_ref[...] = mlp_partial()

    @pl.when((f > 0) & (f < last))
    def _():
        y_ref[...] += mlp_partial()

    @pl.when(f == last)
    def _():
        for r0 in range(0, tm, OUTPROJ_ROWS):
            rows = slice(r0, min(r0 + OUTPROJ_ROWS, tm))
            y = x1_ref[rows, :] + g2_ref[...] * (y_ref[rows, :] + mlp_partial(rows))
            if final_norm:
                y = y * lax.rsqrt(jnp.mean(y * y, axis=-1, keepdims=True) + NORM_EPS) * fw_ref[...]
            y_ref[rows, :] = y


def _resid_mlp(x, o_parts, w_out, g1, nw, shift, scale, g2, w1, w2, *, tm, tf, final_w=None,
               cast_jobs=()):
    b, s, d = x.shape
    dff = w1.shape[1]
    widths = tuple(o.shape[2] for o in o_parts)
    assert sum(widths) == w_out.shape[0] and dff // tf >= 2
    n_i, n_f = s // tm, dff // tf
    steps = b * n_i * n_f
    vec = pl.BlockSpec((None, 1, d), lambda bi, i, f: (bi, 0, 0))
    in_specs = [
        pl.BlockSpec((None, tm, d), lambda bi, i, f: (bi, i, 0)),
        pl.BlockSpec(w_out.shape, lambda bi, i, f: (0, 0), pipeline_mode=pl.Buffered(1)),
        vec,
        pl.BlockSpec((1, d), lambda bi, i, f: (0, 0)),
        vec, vec, vec,
        pl.BlockSpec((d, tf), lambda bi, i, f: (0, f)),
        pl.BlockSpec((tf, d), lambda bi, i, f: (f, 0)),
    ] + [pl.BlockSpec((None, tm, kw), lambda bi, i, f: (bi, i, 0)) for kw in widths]
    args = [x, w_out, g1, nw.reshape(1, d), shift, scale, g2, w1, w2, *o_parts]
    if final_w is not None:
        in_specs.append(pl.BlockSpec((1, d), lambda bi, i, f: (0, 0)))
        args.append(final_w.reshape(1, d))
    out_shapes = [jax.ShapeDtypeStruct(x.shape, x.dtype)]
    out_specs = [pl.BlockSpec((None, tm, d), lambda bi, i, f: (bi, i, 0))]
    for w_stack, idx in cast_jobs:
        _, r, c = w_stack.shape
        tr = r // steps
        assert tr * steps == r and tr % 16 == 0
        in_specs.append(pl.BlockSpec((None, tr, c),
                                     lambda bi, i, f, idx=idx: (idx, (bi * n_i + i) * n_f + f, 0)))
        args.append(w_stack)
        out_shapes.append(jax.ShapeDtypeStruct((r, c), BF16))
        out_specs.append(pl.BlockSpec((tr, c), lambda bi, i, f: ((bi * n_i + i) * n_f + f, 0)))
    outs = pl.pallas_call(
        functools.partial(_resid_mlp_kernel, widths=widths, final_norm=final_w is not None,
                          n_cast=len(cast_jobs)),
        out_shape=out_shapes,
        grid=(b, n_i, n_f),
        in_specs=in_specs,
        out_specs=out_specs,
        scratch_shapes=[pltpu.VMEM((tm, d), F32), pltpu.VMEM((tm, d), BF16)],
        compiler_params=_params("arbitrary", "arbitrary", "arbitrary"),
        name="resid_mlp",
    )(*args)
    return outs if cast_jobs else outs[0]


EVEN_SECTIONS = (
    (0, A_Q_HEADS, "q_norm"),
    (A_Q_HEADS * HEAD_DIM, A_KV_HEADS, "k_norm"),
    ((A_Q_HEADS + A_KV_HEADS) * HEAD_DIM, A_KV_HEADS, "v"),
    ((A_Q_HEADS + 2 * A_KV_HEADS) * HEAD_DIM, B_Q_HEADS, "q"),
    ((A_Q_HEADS + 2 * A_KV_HEADS + B_Q_HEADS) * HEAD_DIM, B_KV_HEADS, "k"),
    ((A_Q_HEADS + 2 * A_KV_HEADS + B_Q_HEADS + B_KV_HEADS) * HEAD_DIM, B_KV_HEADS, "v"),
)
LAT_TM = 512
MLP_TF = 1024


def kernel(x, c, ctx, c_ctx, ada_w, ada_b, norm_w, mlp_w1, mlp_w2, ev_w_in, ev_w_out, ev_q_norm,
           ev_k_norm, ev_sink, od_w_in, od_w_out, od_rpb, final_norm_w):
    b, s, d = x.shape
    n_ctx = ctx.shape[1]
    depth = ada_w.shape[0]
    rows = s // GRID_W

    mod = _ada_mod(jnp.concatenate([c, c_ctx[None, :]], axis=0), ada_w, ada_b)
    rope_tabs = _rope_tables(s)

    def mod_vecs(layer):
        lat = [mod[layer, :b, k * d:(k + 1) * d][:, None, :] for k in range(6)]
        cx = [mod[layer, b:b + 1, k * d:(k + 1) * d][:, None, :] for k in range(6)]
        return lat, cx

    ctx = ctx.reshape(1, b * n_ctx, d)

    def weight_sources(layer):
        w_in, w_out = (ev_w_in, ev_w_out) if layer % 2 == 0 else (od_w_in, od_w_out)
        return ((mlp_w1, layer), (mlp_w2, layer), (w_in, layer // 2), (w_out, layer // 2))

    converted = {}

    def layer_weights(layer):
        if layer in converted:
            return converted[layer]
        return tuple(_to_bf16(w, idx) for w, idx in weight_sources(layer))

    for layer in range(depth):
        need_ctx = layer < depth - 1
        (sh1, sc1, g1, sh2, sc2, g2), (sh1c, sc1c, g1c, sh2c, sc2c, g2c) = mod_vecs(layer)
        j = layer // 2
        if layer % 2 == 0:
            side = ()
            if layer in converted:
                w1, w2, w_in, w_out = converted[layer]
            else:
                src = weight_sources(layer)
                w_in = _to_bf16(*src[2])
                side = (src[0], src[1], src[3])
            n_in = w_in.shape[1]
            norms = (ev_q_norm[j], ev_k_norm[j])
            qkv = _proj(x, norm_w[layer, 0], sh1, sc1, w_in, tm=LAT_TM, tn=n_in,
                        sections=EVEN_SECTIONS, head_norms=norms, rope_tabs=rope_tabs,
                        cast_jobs=side, name="proj_even")
            if side:
                qkv, w1, w2, w_out = qkv
            qkv_c = _proj(ctx, norm_w[layer, 0], sh1c, sc1c, w_in, tm=b * n_ctx, tn=n_in,
                          sections=EVEN_SECTIONS, head_norms=norms, name="proj_even_ctx")
            qkv_c = qkv_c.reshape(b, n_ctx, n_in)
            oa = _gattn(qkv, qkv_c, q_blk=0, k_blk=A_Q_HEADS, v_blk=A_Q_HEADS + A_KV_HEADS)
            sink_kg = ev_sink[j].reshape(B_KV_HEADS, GQA_GROUP).astype(F32)
            qb_blk = (A_Q_HEADS + 2 * A_KV_HEADS) // GQA_GROUP
            kb_blk = A_Q_HEADS + 2 * A_KV_HEADS + B_Q_HEADS
            ob = _wattn(sink_kg, qkv, qkv_c, q_blk=qb_blk, k_blk=kb_blk, v_blk=kb_blk + B_KV_HEADS)
            o_parts = [oa, ob]
            if need_ctx:
                sink_all = jnp.concatenate([jnp.full((A_KV_HEADS, GQA_GROUP), NEG, F32), sink_kg], axis=0)
                oc_parts = [_cattn(sink_all, qkv_c).reshape(1, b * n_ctx, N_HEADS * HEAD_DIM)]
        else:
            w1, w2, w_in, w_out = layer_weights(layer)
            hd = C_HEADS * HEAD_DIM
            qkv = _proj(x, norm_w[layer, 0], sh1, sc1, w_in, tm=LAT_TM, tn=hd, head_major=True,
                        name="proj_odd")
            per_batch = lambda v: jnp.broadcast_to(v, (b, 1, d))
            kv_c = _proj(ctx.reshape(b, n_ctx, d), norm_w[layer, 0], per_batch(sh1c), per_batch(sc1c),
                         w_in, tm=n_ctx, tn=hd, n_off=1, n_out=2 * hd, head_major=True,
                         name="proj_odd_ctx")
            o_parts = [_nattn(qkv, kv_c, _na_bias_tables(od_rpb[j], rows))]
            assert not need_ctx, "the odd layer is the last one at this depth"
        jobs = weight_sources(layer + 1) if layer + 1 < depth else ()
        outs = _resid_mlp(x, o_parts, w_out, g1, norm_w[layer, 1], sh2, sc2, g2, w1, w2, tm=LAT_TM,
                          tf=MLP_TF, final_w=None if need_ctx else final_norm_w, cast_jobs=jobs)
        if jobs:
            x, converted[layer + 1] = outs[0], tuple(outs[1:])
        else:
            x = outs
        if need_ctx:
            ctx = _resid_mlp(ctx, oc_parts, w_out, g1c, norm_w[layer, 1], sh2c, sc2c, g2c, w1, w2,
                             tm=b * n_ctx, tf=MLP_TF)
    return x
```

```python
import functools

import jax
import jax.numpy as jnp
import numpy as np
from jax import lax
from jax.experimental import pallas as pl
from jax.experimental.pallas import tpu as pltpu

D_MODEL = 2048
GRID_W = 64
HEAD_DIM = 128
N_HEADS = D_MODEL // HEAD_DIM
A_Q_HEADS = N_HEADS // 2
A_KV_HEADS = max(1, A_Q_HEADS // 4)
B_Q_HEADS = N_HEADS - A_Q_HEADS
B_KV_HEADS = max(1, B_Q_HEADS // 4)
GQA_GROUP = A_Q_HEADS // A_KV_HEADS
C_HEADS = N_HEADS
WINDOW = 128
NA_KH = 8
NA_KW = 16
D_FF = 4 * D_MODEL
ROPE_THETA = 10000.0
ROPE_PAIRS = HEAD_DIM // 4
NORM_EPS = 1e-6
NEG = -1e30
ATTN_SCALE = HEAD_DIM ** -0.5

V7X_VMEM_BYTES = 64 * 1024 * 1024
VMEM_LIMIT = V7X_VMEM_BYTES - 6 * 1024 * 1024
LANES = 128
SUBLANES = 8

F32 = jnp.float32
BF16 = jnp.bfloat16


def _params(*sem):
    return pltpu.CompilerParams(dimension_semantics=sem, vmem_limit_bytes=VMEM_LIMIT)


ADA_ROWS = 3
ADA_TN = 1024


def _ada_kernel(cb_ref, w_ref, b_ref, o_ref, s_ref):
    @pl.when((pl.program_id(0) == 0) & (pl.program_id(1) == 0))
    def _():
        cvals = cb_ref[...]
        s_ref[...] = cvals * (1.0 / (1.0 + jnp.exp(-cvals)))

    nchunk = ADA_TN // LANES

    def body(kk, accs):
        k0 = pl.multiple_of(kk * SUBLANES, SUBLANES)
        w = w_ref[pl.ds(k0, SUBLANES), :]
        out = []
        for r in range(ADA_ROWS):
            s = s_ref[r, pl.ds(k0, SUBLANES), :]
            for cch in range(nchunk):
                out.append(accs[r * nchunk + cch] + w[:, cch * LANES:(cch + 1) * LANES] * s)
        return tuple(out)

    zeros = tuple(jnp.zeros((SUBLANES, LANES), F32) for _ in range(ADA_ROWS * nchunk))
    accs = lax.fori_loop(0, D_MODEL // SUBLANES, body, zeros, unroll=4)
    o_ref[...] = jnp.broadcast_to(b_ref[...], o_ref.shape)
    for r in range(ADA_ROWS):
        row = jnp.concatenate(
            [jnp.sum(accs[r * nchunk + cch], axis=0, keepdims=True) for cch in range(nchunk)], axis=1)
        o_ref[r:r + 1, :] = row + b_ref[...]


def _ada_mod(cc, ada_w, ada_b):
    depth, d, n = ada_w.shape
    cb = jnp.broadcast_to(cc[:, :, None], (ADA_ROWS, d, LANES))
    return pl.pallas_call(
        _ada_kernel,
        out_shape=jax.ShapeDtypeStruct((depth, SUBLANES, n), F32),
        grid=(depth, n // ADA_TN),
        in_specs=[
            pl.BlockSpec((ADA_ROWS, d, LANES), lambda l, j: (0, 0, 0)),
            pl.BlockSpec((None, d, ADA_TN), lambda l, j: (l, 0, j)),
            pl.BlockSpec((None, 1, ADA_TN), lambda l, j: (l, 0, j)),
        ],
        out_specs=pl.BlockSpec((None, SUBLANES, ADA_TN), lambda l, j: (l, 0, j)),
        scratch_shapes=[pltpu.VMEM((ADA_ROWS, d, LANES), F32)],
        compiler_params=_params("arbitrary", "arbitrary"),
        name="ada_mod",
    )(cb, ada_w, ada_b.reshape(depth, 1, n))


CAST_BLOCK_BYTES = 8 * 1024 * 1024


def _cast_kernel(w_ref, o_ref):
    o_ref[...] = w_ref[...].astype(o_ref.dtype)


def _to_bf16(w_stack, idx):
    _, r, c = w_stack.shape
    tr = r
    while tr * c * w_stack.dtype.itemsize > CAST_BLOCK_BYTES and tr % 32 == 0:
        tr //= 2
    return pl.pallas_call(
        _cast_kernel,
        out_shape=jax.ShapeDtypeStruct((r, c), BF16),
        grid=(r // tr,),
        in_specs=[pl.BlockSpec((None, tr, c), lambda i: (idx, i, 0))],
        out_specs=pl.BlockSpec((tr, c), lambda i: (i, 0)),
        compiler_params=_params("arbitrary"),
        name="to_bf16",
    )(w_stack)


def _norm_modulate(x, nw, shift, scale):
    ms = jnp.mean(x * x, axis=-1, keepdims=True)
    y = x * lax.rsqrt(ms + NORM_EPS) * nw
    return y * (1.0 + scale) + shift


def _rope(y, cos, sin_lo, sin_hi):
    return (y * cos + pltpu.roll(y, HEAD_DIM - ROPE_PAIRS, 1) * sin_lo
            + pltpu.roll(y, ROPE_PAIRS, 1) * sin_hi)


def _rope_tables(seq):
    t = jnp.arange(seq)
    row = (t // GRID_W).astype(F32)
    col = (t % GRID_W).astype(F32)
    inv = ROPE_THETA ** (-jnp.arange(ROPE_PAIRS, dtype=F32) / ROPE_PAIRS)
    ang_r = row[:, None] * inv
    ang_c = col[:, None] * inv
    ang = jnp.concatenate([ang_r, ang_r, ang_c, ang_c], axis=-1)
    cos, sin = jnp.cos(ang), jnp.sin(ang)
    first = (jnp.arange(HEAD_DIM) % (2 * ROPE_PAIRS)) < ROPE_PAIRS
    sin_lo = jnp.where(first, -sin, 0.0)
    sin_hi = jnp.where(first, 0.0, sin)
    return cos, sin_lo, sin_hi


PROJ_ROWS = 256


def _proj_kernel(*refs, sections, rope, head_major, n_cast):
    x_ref, nw_ref, sh_ref, sc_ref, w_ref = refs[:5]
    rest = refs[5:]
    if sections is not None:
        qn_ref, kn_ref = rest[:2]
        rest = rest[2:]
    if rope:
        cos_ref, slo_ref, shi_ref = rest[:3]
        rest = rest[3:]
    cast_in = rest[:n_cast]
    o_ref = rest[n_cast]
    cast_out = rest[n_cast + 1:2 * n_cast + 1]
    h_ref = rest[2 * n_cast + 1] if sections is None else None
    tm = x_ref.shape[0]

    for src_ref, dst_ref in zip(cast_in, cast_out):
        dst_ref[...] = src_ref[...].astype(dst_ref.dtype)

    def project(h, rows):
        if sections is None:
            y = jnp.dot(h, w_ref[...], preferred_element_type=F32).astype(o_ref.dtype)
            if head_major:
                for hh in range(y.shape[1] // HEAD_DIM):
                    o_ref[hh, rows, :] = y[:, hh * HEAD_DIM:(hh + 1) * HEAD_DIM]
            else:
                o_ref[rows, :] = y
            return
        for c0, nh, kind in sections:
            y = jnp.dot(h, w_ref[:, c0:c0 + nh * HEAD_DIM], preferred_element_type=F32)
            for hh in range(nh):
                yh = y[:, hh * HEAD_DIM:(hh + 1) * HEAD_DIM]
                if kind in ("q_norm", "k_norm"):
                    hw = (qn_ref if kind == "q_norm" else kn_ref)[...]
                    yh = yh * lax.rsqrt(jnp.mean(yh * yh, axis=-1, keepdims=True) + NORM_EPS) * hw
                if rope and kind != "v":
                    yh = _rope(yh, cos_ref[rows, :], slo_ref[rows, :], shi_ref[rows, :])
                lo = c0 + hh * HEAD_DIM
                o_ref[rows, lo:lo + HEAD_DIM] = yh.astype(o_ref.dtype)

    @pl.when(pl.program_id(2) == 0)
    def _():
        for r0 in range(0, tm, PROJ_ROWS):
            rows = slice(r0, min(r0 + PROJ_ROWS, tm))
            h = _norm_modulate(x_ref[rows, :], nw_ref[...], sh_ref[...], sc_ref[...]).astype(BF16)
            if h_ref is not None:
                h_ref[rows, :] = h
            project(h, rows)

    if h_ref is not None:
        @pl.when(pl.program_id(2) > 0)
        def _():
            project(h_ref[...], slice(0, tm))


def _proj(x, nw, shift, scale, w, *, tm, tn, n_off=0, n_out=None, sections=None,
          head_norms=None, rope_tabs=None, head_major=False, cast_jobs=(), name="proj"):
    b, s, d = x.shape
    n_out = w.shape[1] if n_out is None else n_out
    nj = n_out // tn
    n_i = s // tm
    steps = b * n_i * nj
    assert not (head_major and sections is not None)
    if head_major:
        hpt = tn // HEAD_DIM
        out_shape = jax.ShapeDtypeStruct((b, n_out // HEAD_DIM, s, HEAD_DIM), BF16)
        out_spec = pl.BlockSpec((None, hpt, tm, HEAD_DIM), lambda bi, i, j: (bi, j, i, 0))
    else:
        out_shape = jax.ShapeDtypeStruct((b, s, n_out), BF16)
        out_spec = pl.BlockSpec((None, tm, tn), lambda bi, i, j: (bi, i, j))
    in_specs = [
        pl.BlockSpec((None, tm, d), lambda bi, i, j: (bi, i, 0)),
        pl.BlockSpec((1, d), lambda bi, i, j: (0, 0)),
        pl.BlockSpec((None, 1, d), lambda bi, i, j: (bi, 0, 0)),
        pl.BlockSpec((None, 1, d), lambda bi, i, j: (bi, 0, 0)),
        pl.BlockSpec((d, tn), lambda bi, i, j: (0, j + n_off),
                     pipeline_mode=pl.Buffered(1) if nj == 1 else None),
    ]
    args = [x, nw.reshape(1, d), shift, scale, w]
    if sections is not None:
        assert nj == 1
        in_specs += [pl.BlockSpec((1, HEAD_DIM), lambda bi, i, j: (0, 0))] * 2
        args += [head_norms[0].reshape(1, HEAD_DIM), head_norms[1].reshape(1, HEAD_DIM)]
    if rope_tabs is not None:
        in_specs += [pl.BlockSpec((tm, HEAD_DIM), lambda bi, i, j: (i, 0))] * 3
        args += list(rope_tabs)
    out_shapes, out_specs = [out_shape], [out_spec]
    for w_stack, idx in cast_jobs:
        _, r, c = w_stack.shape
        tr = r // steps
        assert tr * steps == r and tr % 16 == 0
        in_specs.append(pl.BlockSpec((None, tr, c),
                                     lambda bi, i, j, idx=idx: (idx, (bi * n_i + i) * nj + j, 0)))
        args.append(w_stack)
        out_shapes.append(jax.ShapeDtypeStruct((r, c), BF16))
        out_specs.append(pl.BlockSpec((tr, c), lambda bi, i, j: ((bi * n_i + i) * nj + j, 0)))
    outs = pl.pallas_call(
        functools.partial(_proj_kernel, sections=sections, rope=rope_tabs is not None,
                          head_major=head_major, n_cast=len(cast_jobs)),
        out_shape=out_shapes,
        grid=(b, n_i, nj),
        in_specs=in_specs,
        out_specs=out_specs,
        scratch_shapes=[pltpu.VMEM((tm, d), BF16)] if sections is None else [],
        compiler_params=_params("arbitrary", "arbitrary", "arbitrary"),
        name=name,
    )(*args)
    return outs if cast_jobs else outs[0]


def _qk(q, k):
    return lax.dot_general(q, k, (((1,), (1,)), ((), ())), preferred_element_type=F32)


EXP2_SCALE = ATTN_SCALE * float(np.log2(np.e))
INV_ATTN_SCALE = 1.0 / ATTN_SCALE


def _softmax_pv(parts, sink=None):
    m = functools.reduce(jnp.maximum, [jnp.max(s, axis=-1, keepdims=True) for s, _ in parts])
    if sink is not None:
        m = jnp.maximum(m, sink)
    l = None
    o = None
    for s, v in parts:
        p = jnp.exp2((s - m) * EXP2_SCALE)
        ls = jnp.sum(p, axis=-1, keepdims=True)
        os_ = jnp.dot(p.astype(BF16), v, preferred_element_type=F32)
        l = ls if l is None else l + ls
        o = os_ if o is None else o + os_
    if sink is not None:
        l = l + jnp.exp2((sink - m) * EXP2_SCALE)
    return o / l


def _run_chains(n_blk, n_inner, scores, probs, weighted_values, unroll=1):
    per_body = unroll * n_inner
    assert per_body % 2 == 0 and n_blk % unroll == 0

    def chain(it, n):
        return it * unroll + n // n_inner, n % n_inner

    scores(0, 0, 0)

    def body(it, carry):
        for n in range(per_body):
            blk, g = chain(it, n + 1)
            scores(jnp.minimum(blk, n_blk - 1) if n + 1 == per_body else blk, g, (n + 1) % 2)
            probs(n % n_inner, n % 2)
            blk, g = chain(it, n - 1)
            weighted_values(jnp.maximum(blk, 0) if n == 0 else blk, g, (n - 1) % 2)
        return carry

    lax.fori_loop(0, n_blk // unroll, body, 0)
    weighted_values(n_blk - 1, n_inner - 1, 1)


GATTN_ROWS = 512


def _gattn_kernel(q_ref, k_ref, v_ref, kc_ref, vc_ref, o_ref, kall_ref, vall_ref, s_ref, p_ref):
    c = kc_ref.shape[0]

    n_blk = q_ref.shape[0] // GATTN_ROWS

    kall_ref[:c, :] = kc_ref[...]
    kall_ref[c:, :] = k_ref[...]
    vall_ref[:c, :HEAD_DIM] = vc_ref[...]
    vall_ref[c:, :HEAD_DIM] = v_ref[...]
    lane = lax.broadcasted_iota(jnp.int32, (vall_ref.shape[0], HEAD_DIM), 1)
    vall_ref[:, HEAD_DIM:] = jnp.where(lane == 0, 1.0, 0.0).astype(BF16)

    def rows_of(blk):
        return pl.ds(pl.multiple_of(blk * GATTN_ROWS, GATTN_ROWS), GATTN_ROWS)

    def cols_of(g):
        return slice(g * HEAD_DIM, (g + 1) * HEAD_DIM)

    def scores(blk, g, slot):
        s_ref[slot] = _qk(q_ref[rows_of(blk), cols_of(g)], kall_ref[...])

    def probs(g, slot):
        m = jnp.max(s_ref[slot], axis=-1, keepdims=True)
        p_ref[slot] = jnp.exp2((s_ref[slot] - m) * EXP2_SCALE).astype(BF16)

    def weighted_values(blk, g, slot):
        ol = jnp.dot(p_ref[slot], vall_ref[...], preferred_element_type=F32)
        o = ol[:, :HEAD_DIM] / ol[:, HEAD_DIM:HEAD_DIM + 1]
        o_ref[rows_of(blk), cols_of(g)] = o.astype(o_ref.dtype)

    p_ref[1] = jnp.ones(p_ref.shape[1:], BF16)
    _run_chains(n_blk, GQA_GROUP, scores, probs, weighted_values)


def _gattn(qkv, qkv_c, *, q_blk, k_blk, v_blk):
    b, s, _ = qkv.shape
    c = qkv_c.shape[1]
    gw = GQA_GROUP * HEAD_DIM
    assert GQA_GROUP % 2 == 0 and s % GATTN_ROWS == 0
    single = pl.Buffered(1)
    return pl.pallas_call(
        _gattn_kernel,
        out_shape=jax.ShapeDtypeStruct((b, s, A_Q_HEADS * HEAD_DIM), BF16),
        grid=(b, A_KV_HEADS),
        in_specs=[
            pl.BlockSpec((None, s, gw), lambda bi, kv: (bi, 0, q_blk + kv), pipeline_mode=single),
            pl.BlockSpec((None, s, HEAD_DIM), lambda bi, kv: (bi, 0, k_blk + kv), pipeline_mode=single),
            pl.BlockSpec((None, s, HEAD_DIM), lambda bi, kv: (bi, 0, v_blk + kv), pipeline_mode=single),
            pl.BlockSpec((None, c, HEAD_DIM), lambda bi, kv: (bi, 0, k_blk + kv)),
            pl.BlockSpec((None, c, HEAD_DIM), lambda bi, kv: (bi, 0, v_blk + kv)),
        ],
        out_specs=pl.BlockSpec((None, s, gw), lambda bi, kv: (bi, 0, kv)),
        scratch_shapes=[
            pltpu.VMEM((c + s, HEAD_DIM), BF16),
            pltpu.VMEM((c + s, 2 * HEAD_DIM), BF16),
            pltpu.VMEM((2, GATTN_ROWS, c + s), F32),
            pltpu.VMEM((2, GATTN_ROWS, c + s), BF16),
        ],
        compiler_params=_params("arbitrary", "arbitrary"),
        name="gattn",
    )(qkv, qkv, qkv, qkv_c, qkv_c)


WATTN_TQ = 4 * WINDOW
WATTN_NK = WATTN_TQ + 2 * WINDOW


def _wattn_kernel(sink_ref, q_ref, k_ref, v_ref, kc_ref, vc_ref, o_ref,
                  mask_ref, s_ref, sc_ref, p_ref, pc_ref, l_ref):
    kv = pl.program_id(1)
    seq = q_ref.shape[0]
    n_tile = seq // WATTN_TQ

    def rows_of(tile):
        return pl.ds(pl.multiple_of(tile * WATTN_TQ, WATTN_TQ), WATTN_TQ)

    def key0_of(tile):
        lo, hi = 0, seq - WATTN_NK
        k0 = tile * WATTN_TQ - WINDOW
        return min(max(k0, lo), hi) if isinstance(tile, int) else jnp.clip(k0, lo, hi)

    def key_rows_of(tile):
        return pl.ds(pl.multiple_of(key0_of(tile), WINDOW), WATTN_NK)

    def cols_of(g):
        return slice(g * HEAD_DIM, (g + 1) * HEAD_DIM)

    for var, tile in enumerate((0, 1, n_tile - 1)):
        shift = key0_of(tile) - tile * WATTN_TQ
        dist = (shift + lax.broadcasted_iota(jnp.int32, (WATTN_TQ, WATTN_NK), 1)
                - lax.broadcasted_iota(jnp.int32, (WATTN_TQ, WATTN_NK), 0))
        mask_ref[var] = jnp.where(jnp.abs(dist) <= WINDOW, 0.0, NEG)

    def scores(tile, g, slot):
        q = q_ref[rows_of(tile), cols_of(g)]
        var = jnp.where(tile == 0, 0, jnp.where(tile == n_tile - 1, 2, 1))
        s_ref[slot] = _qk(q, k_ref[key_rows_of(tile), :]) + mask_ref[var]
        sc_ref[slot] = _qk(q, kc_ref[...])

    def probs(g, slot):
        s, sc = s_ref[slot], sc_ref[slot]
        sink = sink_ref[kv, g] * INV_ATTN_SCALE
        m = jnp.maximum(jnp.max(s, axis=-1, keepdims=True), jnp.max(sc, axis=-1, keepdims=True))
        m = jnp.maximum(m, sink)
        p = jnp.exp2((s - m) * EXP2_SCALE)
        pc = jnp.exp2((sc - m) * EXP2_SCALE)
        l_ref[slot] = (jnp.sum(p, axis=-1, keepdims=True) + jnp.sum(pc, axis=-1, keepdims=True)
                       + jnp.exp2((sink - m) * EXP2_SCALE))
        p_ref[slot] = p.astype(BF16)
        pc_ref[slot] = pc.astype(BF16)

    def weighted_values(tile, g, slot):
        o = (jnp.dot(p_ref[slot], v_ref[key_rows_of(tile), :], preferred_element_type=F32)
             + jnp.dot(pc_ref[slot], vc_ref[...], preferred_element_type=F32))
        o_ref[rows_of(tile), cols_of(g)] = (o / l_ref[slot]).astype(o_ref.dtype)

    p_ref[1] = jnp.ones(p_ref.shape[1:], BF16)
    pc_ref[1] = jnp.ones(pc_ref.shape[1:], BF16)
    l_ref[1] = jnp.ones(l_ref.shape[1:], F32)
    _run_chains(seq // WATTN_TQ, GQA_GROUP, scores, probs, weighted_values)


def _wattn(sink_kg, qkv, qkv_c, *, q_blk, k_blk, v_blk):
    b, s, _ = qkv.shape
    c = qkv_c.shape[1]
    gw = GQA_GROUP * HEAD_DIM
    assert s % WATTN_TQ == 0 and s >= 3 * WATTN_TQ

    def col_map(col):
        return lambda bi, kv: (bi, 0, col + kv)

    return pl.pallas_call(
        _wattn_kernel,
        out_shape=jax.ShapeDtypeStruct((b, s, B_Q_HEADS * HEAD_DIM), BF16),
        grid=(b, B_KV_HEADS),
        in_specs=[
            pl.BlockSpec(memory_space=pltpu.SMEM),
            pl.BlockSpec((None, s, gw), col_map(q_blk)),
            pl.BlockSpec((None, s, HEAD_DIM), col_map(k_blk)),
            pl.BlockSpec((None, s, HEAD_DIM), col_map(v_blk)),
            pl.BlockSpec((None, c, HEAD_DIM), col_map(k_blk)),
            pl.BlockSpec((None, c, HEAD_DIM), col_map(v_blk)),
        ],
        out_specs=pl.BlockSpec((None, s, gw), col_map(0)),
        scratch_shapes=[
            pltpu.VMEM((3, WATTN_TQ, WATTN_NK), F32),
            pltpu.VMEM((2, WATTN_TQ, WATTN_NK), F32),
            pltpu.VMEM((2, WATTN_TQ, c), F32),
            pltpu.VMEM((2, WATTN_TQ, WATTN_NK), BF16),
            pltpu.VMEM((2, WATTN_TQ, c), BF16),
            pltpu.VMEM((2, WATTN_TQ, 1), F32),
        ],
        compiler_params=_params("arbitrary", "arbitrary"),
        name="wattn",
    )(sink_kg, qkv, qkv, qkv, qkv_c, qkv_c)


def _cattn_kernel(sink_ref, q_ref, k_ref, v_ref, o_ref):
    grp = pl.program_id(1)
    k, v = k_ref[...], v_ref[...]
    for g in range(GQA_GROUP):
        q = q_ref[:, g * HEAD_DIM:(g + 1) * HEAD_DIM]
        o = _softmax_pv([(_qk(q, k), v)], sink=sink_ref[grp, g] * INV_ATTN_SCALE)
        o_ref[:, g * HEAD_DIM:(g + 1) * HEAD_DIM] = o.astype(o_ref.dtype)


def _cattn(sink_all, qkv_c):
    b, c, _ = qkv_c.shape
    gw = GQA_GROUP * HEAD_DIM
    n_grp = A_KV_HEADS + B_KV_HEADS
    q_map = lambda bi, g: (bi, 0, g + g // A_KV_HEADS)
    k_map = lambda bi, g: (bi, 0, 8 + g + 10 * (g // A_KV_HEADS))
    v_map = lambda bi, g: (bi, 0, 10 + g + 10 * (g // A_KV_HEADS))
    return pl.pallas_call(
        _cattn_kernel,
        out_shape=jax.ShapeDtypeStruct((b, c, N_HEADS * HEAD_DIM), BF16),
        grid=(b, n_grp),
        in_specs=[
            pl.BlockSpec(memory_space=pltpu.SMEM),
            pl.BlockSpec((None, c, gw), q_map),
            pl.BlockSpec((None, c, HEAD_DIM), k_map),
            pl.BlockSpec((None, c, HEAD_DIM), v_map),
        ],
        out_specs=pl.BlockSpec((None, c, gw), lambda bi, g: (bi, 0, g)),
        compiler_params=_params("arbitrary", "arbitrary"),
        name="cattn",
    )(sink_all, qkv_c, qkv_c, qkv_c)


NATTN_ROWS = 4
NATTN_TQ = NATTN_ROWS * GRID_W
NATTN_KROWS = 3 * NATTN_ROWS


NATTN_HG = 2
NATTN_UNROLL = 2
N_RI = 2 * NA_KH - 1
N_CI = 2 * NA_KW - 1


def _na_key_row0(tile, rows):
    lo, hi = 0, rows - NATTN_KROWS
    r0 = (tile - 1) * NATTN_ROWS
    if isinstance(tile, int):
        return min(max(r0, lo), hi)
    return jnp.clip(r0, lo, hi)


def _nattn_kernel(q_ref, k_ref, v_ref, kc_ref, vc_ref, bias_ref, o_ref,
                  vs_ref, vcs_ref, s_ref, sc_ref, p_ref, pc_ref):
    seq = q_ref.shape[1]
    c = kc_ref.shape[1]
    n_tile = seq // NATTN_TQ
    nk = NATTN_KROWS * GRID_W

    def rows_of(tile):
        return pl.ds(pl.multiple_of(tile * NATTN_TQ, NATTN_TQ), NATTN_TQ)

    def key_rows_of(tile):
        return pl.ds(pl.multiple_of(_na_key_row0(tile, seq // GRID_W) * GRID_W, NATTN_TQ), nk)

    def cols_of(g):
        return slice(g * HEAD_DIM, (g + 1) * HEAD_DIM)

    def ones_col(n):
        return jnp.where(lax.broadcasted_iota(jnp.int32, (n, HEAD_DIM), 1) == 0, 1.0, 0.0).astype(BF16)

    for g in range(NATTN_HG):
        vs_ref[g] = jnp.concatenate([v_ref[g], ones_col(seq)], axis=1)
        vcs_ref[g] = jnp.concatenate([vc_ref[g], ones_col(c)], axis=1)

    def scores(tile, _, slot):
        var = jnp.where(tile == 0, 0, jnp.where(tile == n_tile - 1, 2, 1))
        for g in range(NATTN_HG):
            q = q_ref[g, rows_of(tile), :]
            s_ref[slot, g] = _qk(q, k_ref[g, key_rows_of(tile), :]) + bias_ref[g, var]
            sc_ref[slot, g] = _qk(q, kc_ref[g])

    def probs(_, slot):
        for g in range(NATTN_HG):
            s, sc = s_ref[slot, g], sc_ref[slot, g]
            folded = jnp.concatenate([jnp.maximum(s[:, :c], sc), s[:, c:]], axis=1)
            m = jnp.max(folded, axis=-1, keepdims=True)
            p_ref[slot, g] = jnp.exp2((s - m) * EXP2_SCALE).astype(BF16)
            pc_ref[slot, g] = jnp.exp2((sc - m) * EXP2_SCALE).astype(BF16)

    def weighted_values(tile, _, slot):
        for g in range(NATTN_HG):
            ol = (jnp.dot(p_ref[slot, g], vs_ref[g, key_rows_of(tile), :], preferred_element_type=F32)
                  + jnp.dot(pc_ref[slot, g], vcs_ref[g], preferred_element_type=F32))
            o = ol[:, :HEAD_DIM] / ol[:, HEAD_DIM:HEAD_DIM + 1]
            o_ref[rows_of(tile), cols_of(g)] = o.astype(o_ref.dtype)

    p_ref[1] = jnp.ones(p_ref.shape[1:], BF16)
    pc_ref[1] = jnp.ones(pc_ref.shape[1:], BF16)
    _run_chains(n_tile, 1, scores, probs, weighted_values, unroll=NATTN_UNROLL)


def _na_bias_kernel(rpb_ref, o_ref, u_ref, *, rows):
    base = pl.program_id(0) * (N_RI * N_CI)
    kh = min(NA_KH, rows)

    qc = lax.broadcasted_iota(jnp.int32, (GRID_W, LANES), 0)
    lane = lax.broadcasted_iota(jnp.int32, (GRID_W, LANES), 1)
    kc = lane & (GRID_W - 1)
    diag = kc - qc + (NA_KW - 1)

    def fill(a, carry):
        row = base + a * N_CI
        u = jnp.zeros((GRID_W, LANES), F32)
        for ci in range(N_CI):
            u = jnp.where(diag == ci, rpb_ref[row + ci], u)
        u_ref[a] = u * INV_ATTN_SCALE
        return carry

    lax.fori_loop(0, N_RI, fill, 0)

    cs = jnp.clip(qc - NA_KW // 2, 0, GRID_W - NA_KW)
    col_valid = (kc >= cs) & (kc < cs + NA_KW)
    low = lane < GRID_W
    masks = {(True, True): col_valid, (True, False): col_valid & low,
             (False, True): col_valid & jnp.logical_not(low)}
    neg_blk = jnp.full((GRID_W, LANES), NEG, F32)
    n_tile = rows // NATTN_ROWS
    for var, tile in enumerate((0, 1, n_tile - 1)):
        r0 = tile * NATTN_ROWS
        k0 = _na_key_row0(tile, rows)
        for qr in range(NATTN_ROWS):
            rs = min(max(r0 + qr - kh // 2, 0), rows - kh)
            for jj in range(NATTN_KROWS // 2):
                kr = k0 + 2 * jj
                ok = (rs <= kr < rs + kh, rs <= kr + 1 < rs + kh)
                if ok == (False, False):
                    blk = neg_blk
                else:
                    a = kr - (r0 + qr) + NA_KH - 1
                    a_lo = min(max(a, 0), N_RI - 1)
                    a_hi = min(max(a + 1, 0), N_RI - 1)
                    pair = jnp.where(low, u_ref[a_lo], u_ref[a_hi])
                    blk = jnp.where(masks[ok], pair, NEG)
                o_ref[var, qr * GRID_W:(qr + 1) * GRID_W, jj * LANES:(jj + 1) * LANES] = blk


def _na_bias_tables(rpb, rows):
    h = rpb.shape[0]
    assert 2 * GRID_W == LANES and rows >= 3 * NATTN_ROWS and NATTN_ROWS == NA_KH // 2
    return pl.pallas_call(
        functools.partial(_na_bias_kernel, rows=rows),
        out_shape=jax.ShapeDtypeStruct((h, 3, NATTN_TQ, NATTN_KROWS * GRID_W), F32),
        grid=(h,),
        in_specs=[pl.BlockSpec(memory_space=pltpu.SMEM)],
        out_specs=pl.BlockSpec((None, 3, NATTN_TQ, NATTN_KROWS * GRID_W), lambda hh: (hh, 0, 0, 0)),
        scratch_shapes=[pltpu.VMEM((N_RI, GRID_W, LANES), F32)],
        compiler_params=_params("arbitrary"),
        name="na_bias",
    )(rpb.reshape(-1))


def _nattn(qkv, kv_c, bias):
    b, _, s, _ = qkv.shape
    c = kv_c.shape[2]
    h = C_HEADS
    ng = h // NATTN_HG
    gw = NATTN_HG * HEAD_DIM
    nk = NATTN_KROWS * GRID_W
    assert s % NATTN_TQ == 0

    def head_map(first_group):
        return lambda hg, bi: (bi, first_group + hg, 0, 0)

    return pl.pallas_call(
        _nattn_kernel,
        out_shape=jax.ShapeDtypeStruct((b, s, h * HEAD_DIM), BF16),
        grid=(ng, b),
        in_specs=[
            pl.BlockSpec((None, NATTN_HG, s, HEAD_DIM), head_map(0)),
            pl.BlockSpec((None, NATTN_HG, s, HEAD_DIM), head_map(ng)),
            pl.BlockSpec((None, NATTN_HG, s, HEAD_DIM), head_map(2 * ng)),
            pl.BlockSpec((None, NATTN_HG, c, HEAD_DIM), head_map(0)),
            pl.BlockSpec((None, NATTN_HG, c, HEAD_DIM), head_map(ng)),
            pl.BlockSpec((NATTN_HG, 3, NATTN_TQ, nk), lambda hg, bi: (hg, 0, 0, 0),
                         pipeline_mode=pl.Buffered(1)),
        ],
        out_specs=pl.BlockSpec((None, s, gw), lambda hg, bi: (bi, 0, hg)),
        scratch_shapes=[
            pltpu.VMEM((NATTN_HG, s, 2 * HEAD_DIM), BF16),
            pltpu.VMEM((NATTN_HG, c, 2 * HEAD_DIM), BF16),
            pltpu.VMEM((2, NATTN_HG, NATTN_TQ, nk), F32),
            pltpu.VMEM((2, NATTN_HG, NATTN_TQ, c), F32),
            pltpu.VMEM((2, NATTN_HG, NATTN_TQ, nk), BF16),
            pltpu.VMEM((2, NATTN_HG, NATTN_TQ, c), BF16),
        ],
        compiler_params=_params("arbitrary", "arbitrary"),
        name="nattn",
    )(qkv, qkv, qkv, kv_c, kv_c, bias)


OUTPROJ_ROWS = 256


def _resid_mlp_kernel(*refs, widths, final_norm, n_cast):
    x_ref, wo_ref, g1_ref, nw_ref, sh_ref, sc_ref, g2_ref, w1_ref, w2_ref = refs[:9]
    rest = refs[9:]
    o_refs = rest[:len(widths)]
    rest = rest[len(widths):]
    if final_norm:
        fw_ref = rest[0]
        rest = rest[1:]
    cast_in = rest[:n_cast]
    y_ref = rest[n_cast]
    cast_out = rest[n_cast + 1:2 * n_cast + 1]
    x1_ref, h_ref = rest[2 * n_cast + 1:]
    f = pl.program_id(2)
    tm = x_ref.shape[0]

    for src_ref, dst_ref in zip(cast_in, cast_out):
        dst_ref[...] = src_ref[...].astype(dst_ref.dtype)

    last = pl.num_programs(2) - 1

    def mlp_partial(rows=slice(None)):
        a = jnp.dot(h_ref[rows, :], w1_ref[...], preferred_element_type=F32)
        a = jnp.square(jnp.maximum(a, 0.0)).astype(BF16)
        return jnp.dot(a, w2_ref[...], preferred_element_type=F32)

    @pl.when(f == 0)
    def _():
        for r0 in range(0, tm, OUTPROJ_ROWS):
            rows = slice(r0, min(r0 + OUTPROJ_ROWS, tm))
            y = None
            k0 = 0
            for o_ref, kw in zip(o_refs, widths):
                part = jnp.dot(o_ref[rows, :], wo_ref[k0:k0 + kw, :], preferred_element_type=F32)
                y = part if y is None else y + part
                k0 += kw
            x1 = x_ref[rows, :] + g1_ref[...] * y
            x1_ref[rows, :] = x1
            h_ref[rows, :] = _norm_modulate(x1, nw_ref[...], sh_ref[...], sc_ref[...]).astype(BF16)
        y_ref[...] = mlp_partial()

    @pl.when((f > 0) & (f < last))
    def _():
        y_ref[...] += mlp_partial()

    @pl.when(f == last)
    def _():
        for r0 in range(0, tm, OUTPROJ_ROWS):
            rows = slice(r0, min(r0 + OUTPROJ_ROWS, tm))
            y = x1_ref[rows, :] + g2_ref[...] * (y_ref[rows, :] + mlp_partial(rows))
            if final_norm:
                y = y * lax.rsqrt(jnp.mean(y * y, axis=-1, keepdims=True) + NORM_EPS) * fw_ref[...]
            y_ref[rows, :] = y


def _resid_mlp(x, o_parts, w_out, g1, nw, shift, scale, g2, w1, w2, *, tm, tf, final_w=None,
               cast_jobs=()):
    b, s, d = x.shape
    dff = w1.shape[1]
    widths = tuple(o.shape[2] for o in o_parts)
    assert sum(widths) == w_out.shape[0] and dff // tf >= 2
    n_i, n_f = s // tm, dff // tf
    steps = b * n_i * n_f
    vec = pl.BlockSpec((None, 1, d), lambda bi, i, f: (bi, 0, 0))
    in_specs = [
        pl.BlockSpec((None, tm, d), lambda bi, i, f: (bi, i, 0)),
        pl.BlockSpec(w_out.shape, lambda bi, i, f: (0, 0), pipeline_mode=pl.Buffered(1)),
        vec,
        pl.BlockSpec((1, d), lambda bi, i, f: (0, 0)),
        vec, vec, vec,
        pl.BlockSpec((d, tf), lambda bi, i, f: (0, f)),
        pl.BlockSpec((tf, d), lambda bi, i, f: (f, 0)),
    ] + [pl.BlockSpec((None, tm, kw), lambda bi, i, f: (bi, i, 0)) for kw in widths]
    args = [x, w_out, g1, nw.reshape(1, d), shift, scale, g2, w1, w2, *o_parts]
    if final_w is not None:
        in_specs.append(pl.BlockSpec((1, d), lambda bi, i, f: (0, 0)))
        args.append(final_w.reshape(1, d))
    out_shapes = [jax.ShapeDtypeStruct(x.shape, x.dtype)]
    out_specs = [pl.BlockSpec((None, tm, d), lambda bi, i, f: (bi, i, 0))]
    for w_stack, idx in cast_jobs:
        _, r, c = w_stack.shape
        tr = r // steps
        assert tr * steps == r and tr % 16 == 0
        in_specs.append(pl.BlockSpec((None, tr, c),
                                     lambda bi, i, f, idx=idx: (idx, (bi * n_i + i) * n_f + f, 0)))
        args.append(w_stack)
        out_shapes.append(jax.ShapeDtypeStruct((r, c), BF16))
        out_specs.append(pl.BlockSpec((tr, c), lambda bi, i, f: ((bi * n_i + i) * n_f + f, 0)))
    outs = pl.pallas_call(
        functools.partial(_resid_mlp_kernel, widths=widths, final_norm=final_w is not None,
                          n_cast=len(cast_jobs)),
        out_shape=out_shapes,
        grid=(b, n_i, n_f),
        in_specs=in_specs,
        out_specs=out_specs,
        scratch_shapes=[pltpu.VMEM((tm, d), F32), pltpu.VMEM((tm, d), BF16)],
        compiler_params=_params("arbitrary", "arbitrary", "arbitrary"),
        name="resid_mlp",
    )(*args)
    return outs if cast_jobs else outs[0]


EVEN_SECTIONS = (
    (0, A_Q_HEADS, "q_norm"),
    (A_Q_HEADS * HEAD_DIM, A_KV_HEADS, "k_norm"),
    ((A_Q_HEADS + A_KV_HEADS) * HEAD_DIM, A_KV_HEADS, "v"),
    ((A_Q_HEADS + 2 * A_KV_HEADS) * HEAD_DIM, B_Q_HEADS, "q"),
    ((A_Q_HEADS + 2 * A_KV_HEADS + B_Q_HEADS) * HEAD_DIM, B_KV_HEADS, "k"),
    ((A_Q_HEADS + 2 * A_KV_HEADS + B_Q_HEADS + B_KV_HEADS) * HEAD_DIM, B_KV_HEADS, "v"),
)
LAT_TM = 512
MLP_TF = 1024


def kernel(x, c, ctx, c_ctx, ada_w, ada_b, norm_w, mlp_w1, mlp_w2, ev_w_in, ev_w_out, ev_q_norm,
           ev_k_norm, ev_sink, od_w_in, od_w_out, od_rpb, final_norm_w):
    b, s, d = x.shape
    n_ctx = ctx.shape[1]
    depth = ada_w.shape[0]
    rows = s // GRID_W

    mod = _ada_mod(jnp.concatenate([c, c_ctx[None, :]], axis=0), ada_w, ada_b)
    rope_tabs = _rope_tables(s)

    def mod_vecs(layer):
        lat = [mod[layer, :b, k * d:(k + 1) * d][:, None, :] for k in range(6)]
        cx = [mod[layer, b:b + 1, k * d:(k + 1) * d][:, None, :] for k in range(6)]
        return lat, cx

    ctx = ctx.reshape(1, b * n_ctx, d)

    def weight_sources(layer):
        w_in, w_out = (ev_w_in, ev_w_out) if layer % 2 == 0 else (od_w_in, od_w_out)
        return ((mlp_w1, layer), (mlp_w2, layer), (w_in, layer // 2), (w_out, layer // 2))

    converted = {}

    def layer_weights(layer):
        if layer in converted:
            return converted[layer]
        return tuple(_to_bf16(w, idx) for w, idx in weight_sources(layer))

    for layer in range(depth):
        need_ctx = layer < depth - 1
        (sh1, sc1, g1, sh2, sc2, g2), (sh1c, sc1c, g1c, sh2c, sc2c, g2c) = mod_vecs(layer)
        j = layer // 2
        if layer % 2 == 0:
            side = ()
            if layer in converted:
                w1, w2, w_in, w_out = converted[layer]
            else:
                src = weight_sources(layer)
                w_in = _to_bf16(*src[2])
                side = (src[0], src[1], src[3])
            n_in = w_in.shape[1]
            norms = (ev_q_norm[j], ev_k_norm[j])
            qkv = _proj(x, norm_w[layer, 0], sh1, sc1, w_in, tm=LAT_TM, tn=n_in,
                        sections=EVEN_SECTIONS, head_norms=norms, rope_tabs=rope_tabs,
                        cast_jobs=side, name="proj_even")
            if side:
                qkv, w1, w2, w_out = qkv
            qkv_c = _proj(ctx, norm_w[layer, 0], sh1c, sc1c, w_in, tm=b * n_ctx, tn=n_in,
                          sections=EVEN_SECTIONS, head_norms=norms, name="proj_even_ctx")
            qkv_c = qkv_c.reshape(b, n_ctx, n_in)
            oa = _gattn(qkv, qkv_c, q_blk=0, k_blk=A_Q_HEADS, v_blk=A_Q_HEADS + A_KV_HEADS)
            sink_kg = ev_sink[j].reshape(B_KV_HEADS, GQA_GROUP).astype(F32)
            qb_blk = (A_Q_HEADS + 2 * A_KV_HEADS) // GQA_GROUP
            kb_blk = A_Q_HEADS + 2 * A_KV_HEADS + B_Q_HEADS
            ob = _wattn(sink_kg, qkv, qkv_c, q_blk=qb_blk, k_blk=kb_blk, v_blk=kb_blk + B_KV_HEADS)
            o_parts = [oa, ob]
            if need_ctx:
                sink_all = jnp.concatenate([jnp.full((A_KV_HEADS, GQA_GROUP), NEG, F32), sink_kg], axis=0)
                oc_parts = [_cattn(sink_all, qkv_c).reshape(1, b * n_ctx, N_HEADS * HEAD_DIM)]
        else:
            w1, w2, w_in, w_out = layer_weights(layer)
            hd = C_HEADS * HEAD_DIM
            qkv = _proj(x, norm_w[layer, 0], sh1, sc1, w_in, tm=LAT_TM, tn=hd, head_major=True,
                        name="proj_odd")
            per_batch = lambda v: jnp.broadcast_to(v, (b, 1, d))
            kv_c = _proj(ctx.reshape(b, n_ctx, d), norm_w[layer, 0], per_batch(sh1c), per_batch(sc1c),
                         w_in, tm=n_ctx, tn=hd, n_off=1, n_out=2 * hd, head_major=True,
                         name="proj_odd_ctx")
            o_parts = [_nattn(qkv, kv_c, _na_bias_tables(od_rpb[j], rows))]
            assert not need_ctx, "the odd layer is the last one at this depth"
        jobs = weight_sources(layer + 1) if layer + 1 < depth else ()
        outs = _resid_mlp(x, o_parts, w_out, g1, norm_w[layer, 1], sh2, sc2, g2, w1, w2, tm=LAT_TM,
                          tf=MLP_TF, final_w=None if need_ctx else final_norm_w, cast_jobs=jobs)
        if jobs:
            x, converted[layer + 1] = outs[0], tuple(outs[1:])
        else:
            x = outs
        if need_ctx:
            ctx = _resid_mlp(ctx, oc_parts, w_out, g1c, norm_w[layer, 1], sh2c, sc2c, g2c, w1, w2,
                             tm=b * n_ctx, tf=MLP_TF)
    return x
```
